```python
import math
import jax
import jax.numpy as jnp
from jax import lax
import numpy as np

D_MODEL = 2048
BATCH = 2
SEQ = 4096
DEPTH = 2
DEC_BATCH = 128
DEC_SEQ = 4
PAST_LEN = 2048
PAGE_SIZE = 128

N_EVEN = (DEPTH + 1) // 2
N_ODD = DEPTH // 2
MIX_WIDTH = D_MODEL
S5_WIDTH = MIX_WIDTH // 2
S5_GROUP = 16
S5_GROUPS = S5_WIDTH // S5_GROUP
S5_STATE = 64
S5_DT_MIN = 1e-3
S5_DT_MAX = 1e-1
FOX_WIDTH = MIX_WIDTH - S5_WIDTH
FOX_HEAD_DIM = 128
FOX_HEADS = FOX_WIDTH // FOX_HEAD_DIM
Q_BLOCK = 128
FORGET_W_SCALE = 0.1
NEG_INF = -1e30
RET_HEADS = 8
RET_KD = D_MODEL // RET_HEADS
RET_VD = 2 * D_MODEL // RET_HEADS
RET_CHUNK = 128
ROPE_BASE = 10000.0
D_FF = ((8 * D_MODEL // 3 + 127) // 128) * 128
EPS = 1e-6
EVEN_IN = S5_WIDTH + 3 * FOX_WIDTH + FOX_HEADS
ODD_IN = 2 * RET_HEADS * RET_KD + 2 * RET_HEADS * RET_VD

kernel_name = 'hybrid_s5_fox_retention_macaron_step'


def rmsnorm(x, g):
    xf = x.astype(jnp.float32)
    y = xf * lax.rsqrt(jnp.mean(xf * xf, axis=-1, keepdims=True) + EPS)
    return (y * g.astype(jnp.float32)).astype(x.dtype)


def swiglu(h, w_gate, w_up, w_down):
    return (jax.nn.silu(h @ w_gate) * (h @ w_up)) @ w_down


def _diag_combine(e1, e2):
    a1, b1 = e1
    a2, b2 = e2
    return a1 * a2, a2 * b1 + b2


def s5_branch(u, h0, lam_re, lam_im, b_re, b_im, c_re, c_im, d_skip, log_dt, w_glu, b_glu):
    f32 = jnp.float32
    bsz, seq_len, _ = u.shape
    ug = u.astype(f32).reshape(bsz, seq_len, S5_GROUPS, S5_GROUP)
    lam = lax.complex(lam_re.astype(f32), lam_im.astype(f32))
    dt = jnp.exp(log_dt.astype(f32))[:, None]
    lam_bar = jnp.exp(lam * dt)
    b_bar = ((lam_bar - 1.0) / lam)[..., None] * lax.complex(b_re.astype(f32), b_im.astype(f32))
    bu = jnp.einsum('gnp,blgp->blgn', b_bar, ug.astype(jnp.complex64))
    bu = bu.at[:, 0].add(lam_bar[None] * h0)
    a = jnp.broadcast_to(lam_bar, bu.shape)
    _, h = lax.associative_scan(_diag_combine, (a, bu), axis=1)
    c = lax.complex(c_re.astype(f32), c_im.astype(f32))
    y = jnp.real(jnp.einsum('gpn,blgn->blgp', c, h)) + d_skip.astype(f32).reshape(S5_GROUPS, S5_GROUP) * ug
    y = jax.nn.gelu(y.reshape(bsz, seq_len, S5_WIDTH))
    out = y * jax.nn.sigmoid(y @ w_glu.astype(f32) + b_glu.astype(f32))
    return out.astype(u.dtype), h[:, -1]


def fox_attend(q, k, v, cq, ck, q_pos, k_pos):
    s = jnp.einsum('bqhd,bkhd->bhqk', q, k).astype(jnp.float32) * FOX_HEAD_DIM ** -0.5
    s = s + jnp.swapaxes(cq, 1, 2)[..., :, None] - jnp.swapaxes(ck, 1, 2)[..., None, :]
    mask = q_pos[:, None] >= k_pos[None, :]
    p = jax.nn.softmax(jnp.where(mask[None, None], s, NEG_INF), axis=-1)
    return jnp.einsum('bhqk,bkhd->bqhd', p.astype(v.dtype), v)


def even_mixer(hn, past, w_in, b_f, lam_re, lam_im, b_re, b_im, c_re, c_im, d_skip, log_dt,
               w_glu, b_glu, w_out):
    f32 = jnp.float32
    bsz, seq_len, _ = hn.shape
    z = hn @ w_in
    o1 = S5_WIDTH
    o2 = o1 + FOX_WIDTH
    o3 = o2 + FOX_WIDTH
    o4 = o3 + FOX_WIDTH
    u = z[..., :o1]
    q = z[..., o1:o2].reshape(bsz, seq_len, FOX_HEADS, FOX_HEAD_DIM)
    k = z[..., o2:o3].reshape(bsz, seq_len, FOX_HEADS, FOX_HEAD_DIM)
    v = z[..., o3:o4].reshape(bsz, seq_len, FOX_HEADS, FOX_HEAD_DIM)
    logf = jax.nn.log_sigmoid(z[..., o4:].astype(f32) + b_f.astype(f32))
    if past is None:
        h0 = jnp.zeros((bsz, S5_GROUPS, S5_STATE), jnp.complex64)
        c = jnp.cumsum(logf, axis=1)
        n_blocks = seq_len // Q_BLOCK
        pos = jnp.arange(seq_len)
        qb = q.reshape(bsz, n_blocks, Q_BLOCK, FOX_HEADS, FOX_HEAD_DIM).swapaxes(0, 1)
        cb = c.reshape(bsz, n_blocks, Q_BLOCK, FOX_HEADS).swapaxes(0, 1)
        pb = pos.reshape(n_blocks, Q_BLOCK)
        att = lax.map(lambda blk: fox_attend(blk[0], k, v, blk[1], c, blk[2], pos), (qb, cb, pb))
        att = att.swapaxes(0, 1).reshape(bsz, seq_len, FOX_WIDTH)
    else:
        k_past, v_past, logf_past, h0 = past
        past_len = k_past.shape[1]
        c_past = jnp.cumsum(logf_past.astype(f32), axis=1)
        c_new = c_past[:, -1:] + jnp.cumsum(logf, axis=1)
        k_all = jnp.concatenate([k_past.astype(k.dtype), k], axis=1)
        v_all = jnp.concatenate([v_past.astype(v.dtype), v], axis=1)
        c_all = jnp.concatenate([c_past, c_new], axis=1)
        q_pos = past_len + jnp.arange(seq_len)
        k_pos = jnp.arange(past_len + seq_len)
        att = fox_attend(q, k_all, v_all, c_new, c_all, q_pos, k_pos).reshape(bsz, seq_len, FOX_WIDTH)
    s5_out, h_last = s5_branch(u, h0, lam_re, lam_im, b_re, b_im, c_re, c_im, d_skip, log_dt, w_glu, b_glu)
    mixed = jnp.concatenate([s5_out, att.astype(hn.dtype)], axis=-1) @ w_out
    return mixed, (k, v, logf, jnp.real(h_last), jnp.imag(h_last))


def rotary(x, pos):
    half = x.shape[-1] // 2
    inv_freq = ROPE_BASE ** (-jnp.arange(half, dtype=jnp.float32) / half)
    ang = pos.astype(jnp.float32)[:, None] * inv_freq[None, :]
    cos = jnp.cos(ang)[None, :, None, :]
    sin = jnp.sin(ang)[None, :, None, :]
    x1, x2 = x[..., :half], x[..., half:]
    return jnp.concatenate([x1 * cos - x2 * sin, x1 * sin + x2 * cos], axis=-1)


def ret_log_gamma():
    return jnp.log1p(-(2.0 ** (-5.0 - jnp.arange(RET_HEADS, dtype=jnp.float32))))


def retention_chunk(state, qkv):
    q, k, v = qkv
    cl = q.shape[1]
    lg = ret_log_gamma()
    i = jnp.arange(cl, dtype=jnp.float32)
    diff = i[:, None] - i[None, :]
    dmask = jnp.where(diff >= 0, jnp.exp(jnp.maximum(diff, 0.0)[None] * lg[:, None, None]), 0.0)
    inner = jnp.einsum('bihd,bjhd->bhij', q, k) * dmask[None]
    out = jnp.einsum('bhij,bjhe->bihe', inner, v)
    out = out + jnp.einsum('bihd,bhde->bihe', q, state) * jnp.exp((i + 1.0)[:, None] * lg[None, :])[None, :, :, None]
    k_dec = k * jnp.exp((cl - 1.0 - i)[:, None] * lg[None, :])[None, :, :, None]
    new_state = jnp.exp(cl * lg)[None, :, None, None] * state + jnp.einsum('bjhd,bjhe->bhde', k_dec, v)
    return new_state, out


def odd_mixer(hn, s0, start, w_in, gn_gain, w_out):
    f32 = jnp.float32
    bsz, seq_len, _ = hn.shape
    z = hn @ w_in
    qk = RET_HEADS * RET_KD
    vd = RET_HEADS * RET_VD
    pos = start + jnp.arange(seq_len)
    q = rotary(z[..., :qk].astype(f32).reshape(bsz, seq_len, RET_HEADS, RET_KD), pos)
    k = rotary(z[..., qk:2 * qk].astype(f32).reshape(bsz, seq_len, RET_HEADS, RET_KD), pos) * RET_KD ** -0.5
    v = z[..., 2 * qk:2 * qk + vd].astype(f32).reshape(bsz, seq_len, RET_HEADS, RET_VD)
    g = z[..., 2 * qk + vd:]
    cl = min(RET_CHUNK, seq_len)
    nc = seq_len // cl
    to_chunks = lambda t: t.reshape((bsz, nc, cl) + t.shape[2:]).swapaxes(0, 1)
    s_final, oc = lax.scan(retention_chunk, s0, (to_chunks(q), to_chunks(k), to_chunks(v)))
    o = oc.swapaxes(0, 1).reshape(bsz, seq_len, RET_HEADS, RET_VD)
    mu = jnp.mean(o, axis=-1, keepdims=True)
    var = jnp.mean(jnp.square(o - mu), axis=-1, keepdims=True)
    o = ((o - mu) * lax.rsqrt(var + EPS)).reshape(bsz, seq_len, vd) * gn_gain.astype(f32)
    out = (jax.nn.silu(g.astype(f32)) * o).astype(hn.dtype) @ w_out
    return out, s_final


def setup_inputs(seed: int = 0) -> dict:
    key = jax.random.key(seed)
    keys = iter(jax.random.split(key, 40))
    f32 = jnp.float32

    def normal(shape, scale=1.0):
        return jax.random.normal(next(keys), shape, f32) * scale

    n_pages = PAST_LEN // PAGE_SIZE
    n_pool = (5 * DEC_BATCH * n_pages + 3) // 4
    x_prompt = normal((BATCH, SEQ, D_MODEL))
    x_sample = normal((DEC_BATCH, DEC_SEQ, D_MODEL))
    cache_k = normal((N_EVEN, n_pool, PAGE_SIZE, FOX_HEADS, FOX_HEAD_DIM))
    cache_v = normal((N_EVEN, n_pool, PAGE_SIZE, FOX_HEADS, FOX_HEAD_DIM))
    cache_logf = jax.nn.log_sigmoid(normal((N_EVEN, n_pool, PAGE_SIZE, FOX_HEADS)) + 3.5)
    state_s5_re = normal((N_EVEN, DEC_BATCH, S5_GROUPS, S5_STATE), 0.5)
    state_s5_im = normal((N_EVEN, DEC_BATCH, S5_GROUPS, S5_STATE), 0.5)
    state_ret = normal((N_ODD, DEC_BATCH, RET_HEADS, RET_KD, RET_VD), 0.5)
    page_table = jax.random.permutation(next(keys), n_pool)[: DEC_BATCH * n_pages].reshape(DEC_BATCH, n_pages).astype(jnp.int32)
    norm_gain = 1.0 + normal((DEPTH, 3, D_MODEL), 0.02)
    w_ffn_gate = normal((DEPTH, 2, D_MODEL, D_FF), D_MODEL ** -0.5)
    w_ffn_up = normal((DEPTH, 2, D_MODEL, D_FF), D_MODEL ** -0.5)
    w_ffn_down = normal((DEPTH, 2, D_FF, D_MODEL), D_FF ** -0.5)
    w_in_even = normal((N_EVEN, D_MODEL, EVEN_IN), D_MODEL ** -0.5)
    w_in_even = w_in_even.at[..., EVEN_IN - FOX_HEADS:].multiply(FORGET_W_SCALE)
    b_forget = jax.random.uniform(next(keys), (N_EVEN, FOX_HEADS), f32, 2.0, 5.0)
    s5_lam_re = -0.5 + normal((N_EVEN, S5_GROUPS, S5_STATE), 0.01)
    s5_lam_im = math.pi * jnp.arange(S5_STATE, dtype=f32) + normal((N_EVEN, S5_GROUPS, S5_STATE), 0.01)
    s5_b_re = normal((N_EVEN, S5_GROUPS, S5_STATE, S5_GROUP), (2 * S5_GROUP) ** -0.5)
    s5_b_im = normal((N_EVEN, S5_GROUPS, S5_STATE, S5_GROUP), (2 * S5_GROUP) ** -0.5)
    s5_c_re = normal((N_EVEN, S5_GROUPS, S5_GROUP, S5_STATE), (2 * S5_STATE) ** -0.5)
    s5_c_im = normal((N_EVEN, S5_GROUPS, S5_GROUP, S5_STATE), (2 * S5_STATE) ** -0.5)
    s5_d = normal((N_EVEN, S5_WIDTH))
    s5_log_dt = jax.random.uniform(next(keys), (N_EVEN, S5_GROUPS), f32, math.log(S5_DT_MIN), math.log(S5_DT_MAX))
    w_glu = normal((N_EVEN, S5_WIDTH, S5_WIDTH), S5_WIDTH ** -0.5)
    b_glu = normal((N_EVEN, S5_WIDTH), 0.01)
    w_out_even = normal((N_EVEN, MIX_WIDTH, D_MODEL), MIX_WIDTH ** -0.5)
    w_in_odd = normal((N_ODD, D_MODEL, ODD_IN), D_MODEL ** -0.5)
    ret_gn_gain = 1.0 + normal((N_ODD, RET_HEADS * RET_VD), 0.02)
    w_out_odd = normal((N_ODD, RET_HEADS * RET_VD, D_MODEL), (RET_HEADS * RET_VD) ** -0.5)
    final_norm = 1.0 + normal((D_MODEL,), 0.02)
    return {'x_prompt': x_prompt, 'x_sample': x_sample, 'cache_k': cache_k, 'cache_v': cache_v,
            'cache_logf': cache_logf, 'state_s5_re': state_s5_re, 'state_s5_im': state_s5_im,
            'state_ret': state_ret, 'page_table': page_table, 'norm_gain': norm_gain,
            'w_ffn_gate': w_ffn_gate, 'w_ffn_up': w_ffn_up, 'w_ffn_down': w_ffn_down,
            'w_in_even': w_in_even, 'b_forget': b_forget, 's5_lam_re': s5_lam_re, 's5_lam_im': s5_lam_im,
            's5_b_re': s5_b_re, 's5_b_im': s5_b_im, 's5_c_re': s5_c_re, 's5_c_im': s5_c_im,
            's5_d': s5_d, 's5_log_dt': s5_log_dt, 'w_glu': w_glu, 'b_glu': b_glu,
            'w_out_even': w_out_even, 'w_in_odd': w_in_odd, 'ret_gn_gain': ret_gn_gain,
            'w_out_odd': w_out_odd, 'final_norm': final_norm}


def reference(x_prompt, x_sample, cache_k, cache_v, cache_logf, state_s5_re, state_s5_im, state_ret,
              page_table, norm_gain, w_ffn_gate, w_ffn_up, w_ffn_down, w_in_even, b_forget,
              s5_lam_re, s5_lam_im, s5_b_re, s5_b_im, s5_c_re, s5_c_im, s5_d, s5_log_dt, w_glu, b_glu,
              w_out_even, w_in_odd, ret_gn_gain, w_out_odd, final_norm):
    f32 = jnp.float32

    def gather_pages(pool):
        g = pool[page_table]
        return g.reshape((g.shape[0], g.shape[1] * g.shape[2]) + g.shape[3:])

    def trunk(x, sample):
        new_even = []
        new_odd = []
        for layer in range(DEPTH):
            h = x + 0.5 * swiglu(rmsnorm(x, norm_gain[layer, 0]), w_ffn_gate[layer, 0], w_ffn_up[layer, 0], w_ffn_down[layer, 0])
            hn = rmsnorm(h, norm_gain[layer, 1])
            if layer % 2 == 0:
                e = layer // 2
                past = None
                if sample:
                    past = (gather_pages(cache_k[e]), gather_pages(cache_v[e]), gather_pages(cache_logf[e]),
                            lax.complex(state_s5_re[e].astype(f32), state_s5_im[e].astype(f32)))
                m, st = even_mixer(hn, past, w_in_even[e], b_forget[e], s5_lam_re[e], s5_lam_im[e],
                                   s5_b_re[e], s5_b_im[e], s5_c_re[e], s5_c_im[e], s5_d[e], s5_log_dt[e],
                                   w_glu[e], b_glu[e], w_out_even[e])
                new_even.append(st)
            else:
                o = layer // 2
                if sample:
                    s0 = state_ret[o].astype(f32)
                    start = PAST_LEN
                else:
                    s0 = jnp.zeros((x.shape[0], RET_HEADS, RET_KD, RET_VD), f32)
                    start = 0
                m, st = odd_mixer(hn, s0, start, w_in_odd[o], ret_gn_gain[o], w_out_odd[o])
                new_odd.append(st)
            h = h + m
            x = h + 0.5 * swiglu(rmsnorm(h, norm_gain[layer, 2]), w_ffn_gate[layer, 1], w_ffn_up[layer, 1], w_ffn_down[layer, 1])
        return rmsnorm(x, final_norm), new_even, new_odd

    y_prompt, ev_p, od_p = trunk(x_prompt, False)
    y_sample, ev_s, od_s = trunk(x_sample, True)

    new_k_prompt = jnp.stack([s[0] for s in ev_p])
    new_v_prompt = jnp.stack([s[1] for s in ev_p])
    new_logf_prompt = jnp.stack([s[2] for s in ev_p])
    new_s5_re_prompt = jnp.stack([s[3] for s in ev_p])
    new_s5_im_prompt = jnp.stack([s[4] for s in ev_p])
    new_ret_prompt = jnp.stack(od_p)
    new_k_sample = jnp.stack([s[0] for s in ev_s])
    new_v_sample = jnp.stack([s[1] for s in ev_s])
    new_logf_sample = jnp.stack([s[2] for s in ev_s])
    new_s5_re_sample = jnp.stack([s[3] for s in ev_s])
    new_s5_im_sample = jnp.stack([s[4] for s in ev_s])
    new_ret_sample = jnp.stack(od_s)
    return (y_prompt, y_sample, new_k_prompt, new_v_prompt, new_logf_prompt, new_s5_re_prompt,
            new_s5_im_prompt, new_ret_prompt, new_k_sample, new_v_sample, new_logf_sample,
            new_s5_re_sample, new_s5_im_sample, new_ret_sample)
```

```python
import functools
import math

import jax
import jax.numpy as jnp
from jax import lax
from jax.experimental import pallas as pl
from jax.experimental.pallas import tpu as pltpu

D_MODEL = 2048
BATCH = 2
SEQ = 4096
DEPTH = 2
DEC_BATCH = 128
DEC_SEQ = 4
PAST_LEN = 2048
PAGE_SIZE = 128
S5_WIDTH = 1024
S5_GROUP = 16
S5_GROUPS = 64
S5_STATE = 64
FOX_WIDTH = 1024
FOX_HEAD_DIM = 128
FOX_HEADS = 8
Q_BLOCK = 128
NEG_INF = -1e30
RET_HEADS = 8
RET_KD = 256
RET_VD = 512
RET_CHUNK = 128
ROPE_BASE = 10000.0
D_FF = 5504
EPS = 1e-6

N_PROMPT = BATCH * SEQ
N_SAMPLE = DEC_BATCH * DEC_SEQ
N_TOK = N_PROMPT + N_SAMPLE

ROW_TILE = 1088
FF_TILE = 256
VMEM_LIMIT = 56 * 1024 * 1024

F32 = jnp.float32
BF16 = jnp.bfloat16


def _rms_rows(x, gain):
    return x * lax.rsqrt(jnp.mean(x * x, axis=-1, keepdims=True) + EPS) * gain


def _ffn_kernel(x_ref, g_ref, wg_ref, wu_ref, wd_ref, *rest, d_ff, final):
    if final:
        fg_ref, o_ref, y_ref, h_ref = rest
    else:
        o_ref, h_ref = rest
    f = pl.program_id(1)
    tf = wg_ref.shape[1]

    @pl.when(f == 0)
    def _():
        x = x_ref[...]
        h_ref[...] = _rms_rows(x, g_ref[...]).astype(BF16)
        o_ref[...] = x

    h = h_ref[...]
    a = jnp.dot(h, wg_ref[...].astype(BF16), preferred_element_type=F32)
    b = jnp.dot(h, wu_ref[...].astype(BF16), preferred_element_type=F32)
    c = a * jax.nn.sigmoid(a) * b
    col = f * tf + lax.broadcasted_iota(jnp.int32, (1, tf), 1)
    c = jnp.where(col < d_ff, c, 0.0).astype(BF16)
    row = f * tf + lax.broadcasted_iota(jnp.int32, (tf, 1), 0)
    wd = jnp.where(row < d_ff, wd_ref[...], 0.0).astype(BF16)
    o_ref[...] += 0.5 * jnp.dot(c, wd, preferred_element_type=F32)

    if final:
        @pl.when(f == pl.num_programs(1) - 1)
        def _():
            y_ref[...] = _rms_rows(o_ref[...], fg_ref[...])


def ffn(x, gain, wg, wu, wd, final_gain=None):
    m, d = x.shape
    d_ff = wg.shape[1]
    final = final_gain is not None
    grid = (m // ROW_TILE, pl.cdiv(d_ff, FF_TILE))
    row_spec = pl.BlockSpec((ROW_TILE, d), lambda i, f: (i, 0))
    vec_spec = pl.BlockSpec((1, d), lambda i, f: (0, 0))
    in_specs = [
        pl.BlockSpec((ROW_TILE, d), lambda i, f: (i, 0), pipeline_mode=pl.Buffered(1)),
        vec_spec,
        pl.BlockSpec((d, FF_TILE), lambda i, f: (0, f)),
        pl.BlockSpec((d, FF_TILE), lambda i, f: (0, f)),
        pl.BlockSpec((FF_TILE, d), lambda i, f: (f, 0)),
    ]
    args = [x, gain.reshape(1, d), wg, wu, wd]
    out_shape = jax.ShapeDtypeStruct((m, d), F32)
    out_specs = row_spec
    if final:
        in_specs.append(vec_spec)
        args.append(final_gain.reshape(1, d))
        out_shape = (out_shape, jax.ShapeDtypeStruct((m, d), F32))
        out_specs = (row_spec, pl.BlockSpec((ROW_TILE, d), lambda i, f: (i, 0), pipeline_mode=pl.Buffered(1)))
    return pl.pallas_call(
        functools.partial(_ffn_kernel, d_ff=d_ff, final=final),
        grid=grid,
        in_specs=in_specs,
        out_specs=out_specs,
        out_shape=out_shape,
        scratch_shapes=[pltpu.VMEM((ROW_TILE, d), BF16)],
        compiler_params=pltpu.CompilerParams(
            dimension_semantics=("parallel", "arbitrary"), vmem_limit_bytes=VMEM_LIMIT),
        name="ffn_final" if final else "ffn",
    )(*args)


PROJ_TILE = 1024


def _norm_proj_kernel(x_ref, g_ref, w_ref, *rest, narrow):
    if narrow:
        wn_ref, z_ref, zn_ref, h_ref = rest
    else:
        z_ref, h_ref = rest
    j = pl.program_id(1)

    @pl.when(j == 0)
    def _():
        h_ref[...] = _rms_rows(x_ref[...], g_ref[...]).astype(BF16)
        if narrow:
            zn_ref[...] = jnp.dot(h_ref[...], wn_ref[...].astype(BF16), preferred_element_type=F32)

    z_ref[0] = jnp.dot(h_ref[...], w_ref[...].astype(BF16), preferred_element_type=F32)


def norm_proj(x, gain, w, n_wide, narrow=False):
    m, d = x.shape
    nj = n_wide // PROJ_TILE
    in_specs = [
        pl.BlockSpec((ROW_TILE, d), lambda i, j: (i, 0), pipeline_mode=pl.Buffered(1)),
        pl.BlockSpec((1, d), lambda i, j: (0, 0)),
        pl.BlockSpec((d, PROJ_TILE), lambda i, j: (0, j)),
    ]
    args = [x, gain.reshape(1, d), w]
    out_shape = jax.ShapeDtypeStruct((nj, m, PROJ_TILE), F32)
    out_specs = pl.BlockSpec((1, ROW_TILE, PROJ_TILE), lambda i, j: (j, i, 0))
    if narrow:
        in_specs.append(pl.BlockSpec((d, 128), lambda i, j: (0, n_wide // 128)))
        args.append(w)
        out_shape = (out_shape, jax.ShapeDtypeStruct((m, 128), F32))
        out_specs = (out_specs, pl.BlockSpec((ROW_TILE, 128), lambda i, j: (i, 0)))
    return pl.pallas_call(
        functools.partial(_norm_proj_kernel, narrow=narrow),
        grid=(m // ROW_TILE, nj),
        in_specs=in_specs,
        out_specs=out_specs,
        out_shape=out_shape,
        scratch_shapes=[pltpu.VMEM((ROW_TILE, d), BF16)],
        compiler_params=pltpu.CompilerParams(
            dimension_semantics=("parallel", "arbitrary"), vmem_limit_bytes=VMEM_LIMIT),
        name="norm_proj_narrow" if narrow else "norm_proj",
    )(*args)


OUT_TILE = 512


def _out_proj_kernel(a_ref, w_ref, r_ref, o_ref):
    o_ref[...] = r_ref[...] + jnp.dot(a_ref[...], w_ref[...].astype(BF16), preferred_element_type=F32)


def out_proj(a, w, res):
    m, k = a.shape
    n = w.shape[1]
    return pl.pallas_call(
        _out_proj_kernel,
        grid=(m // ROW_TILE, n // OUT_TILE),
        in_specs=[
            pl.BlockSpec((ROW_TILE, k), lambda i, j: (i, 0)),
            pl.BlockSpec((k, OUT_TILE), lambda i, j: (0, j)),
            pl.BlockSpec((ROW_TILE, OUT_TILE), lambda i, j: (i, j)),
        ],
        out_specs=pl.BlockSpec((ROW_TILE, OUT_TILE), lambda i, j: (i, j)),
        out_shape=jax.ShapeDtypeStruct((m, n), F32),
        compiler_params=pltpu.CompilerParams(
            dimension_semantics=("parallel", "arbitrary"), vmem_limit_bytes=VMEM_LIMIT),
        name="out_proj",
    )(a, w, res)


def _diag_combine(e1, e2):
    a1, b1 = e1
    a2, b2 = e2
    return a1 * a2, a2 * b1 + b2


def s5_branch(u, h0, lam_re, lam_im, b_re, b_im, c_re, c_im, d_skip, log_dt, w_glu, b_glu):
    bsz, seq_len, _ = u.shape
    ug = u.astype(F32).reshape(bsz, seq_len, S5_GROUPS, S5_GROUP)
    lam = lax.complex(lam_re.astype(F32), lam_im.astype(F32))
    dt = jnp.exp(log_dt.astype(F32))[:, None]
    lam_bar = jnp.exp(lam * dt)
    b_bar = ((lam_bar - 1.0) / lam)[..., None] * lax.complex(b_re.astype(F32), b_im.astype(F32))
    bu = jnp.einsum('gnp,blgp->blgn', b_bar, ug.astype(jnp.complex64))
    bu = bu.at[:, 0].add(lam_bar[None] * h0)
    a = jnp.broadcast_to(lam_bar, bu.shape)
    _, h = lax.associative_scan(_diag_combine, (a, bu), axis=1)
    c = lax.complex(c_re.astype(F32), c_im.astype(F32))
    y = jnp.real(jnp.einsum('gpn,blgn->blgp', c, h)) + d_skip.astype(F32).reshape(S5_GROUPS, S5_GROUP) * ug
    y = jax.nn.gelu(y.reshape(bsz, seq_len, S5_WIDTH))
    out = y * jax.nn.sigmoid(y @ w_glu.astype(F32) + b_glu.astype(F32))
    return out.astype(u.dtype), h[:, -1]


def fox_attend(q, k, v, cq, ck, q_pos, k_pos):
    s = jnp.einsum('bqhd,bkhd->bhqk', q, k).astype(F32) * FOX_HEAD_DIM ** -0.5
    s = s + jnp.swapaxes(cq, 1, 2)[..., :, None] - jnp.swapaxes(ck, 1, 2)[..., None, :]
    mask = q_pos[:, None] >= k_pos[None, :]
    p = jax.nn.softmax(jnp.where(mask[None, None], s, NEG_INF), axis=-1)
    return jnp.einsum('bhqk,bkhd->bqhd', p.astype(v.dtype), v)


def even_mixer_core(u, q, k, v, zf, past, b_f, s5_params):
    bsz, seq_len, _ = u.shape
    q = q.reshape(bsz, seq_len, FOX_HEADS, FOX_HEAD_DIM)
    k = k.reshape(bsz, seq_len, FOX_HEADS, FOX_HEAD_DIM)
    v = v.reshape(bsz, seq_len, FOX_HEADS, FOX_HEAD_DIM)
    logf = jax.nn.log_sigmoid(zf + b_f.astype(F32))
    if past is None:
        h0 = jnp.zeros((bsz, S5_GROUPS, S5_STATE), jnp.complex64)
        c = jnp.cumsum(logf, axis=1)
        n_blocks = seq_len // Q_BLOCK
        pos = jnp.arange(seq_len)
        qb = q.reshape(bsz, n_blocks, Q_BLOCK, FOX_HEADS, FOX_HEAD_DIM).swapaxes(0, 1)
        cb = c.reshape(bsz, n_blocks, Q_BLOCK, FOX_HEADS).swapaxes(0, 1)
        pb = pos.reshape(n_blocks, Q_BLOCK)
        att = lax.map(lambda blk: fox_attend(blk[0], k, v, blk[1], c, blk[2], pos), (qb, cb, pb))
        att = att.swapaxes(0, 1).reshape(bsz, seq_len, FOX_WIDTH)
    else:
        k_past, v_past, logf_past, h0 = past
        past_len = k_past.shape[1]
        c_past = jnp.cumsum(logf_past.astype(F32), axis=1)
        c_new = c_past[:, -1:] + jnp.cumsum(logf, axis=1)
        k_all = jnp.concatenate([k_past, k], axis=1)
        v_all = jnp.concatenate([v_past, v], axis=1)
        c_all = jnp.concatenate([c_past, c_new], axis=1)
        q_pos = past_len + jnp.arange(seq_len)
        k_pos = jnp.arange(past_len + seq_len)
        att = fox_attend(q, k_all, v_all, c_new, c_all, q_pos, k_pos).reshape(bsz, seq_len, FOX_WIDTH)
    s5_out, h_last = s5_branch(u, h0, *s5_params)
    mixed = jnp.concatenate([s5_out, att], axis=-1)
    return mixed, (k, v, logf, jnp.real(h_last), jnp.imag(h_last))


def rotary(x, pos):
    half = x.shape[-1] // 2
    inv_freq = ROPE_BASE ** (-jnp.arange(half, dtype=F32) / half)
    ang = pos.astype(F32)[:, None] * inv_freq[None, :]
    cos = jnp.cos(ang)[None, :, None, :]
    sin = jnp.sin(ang)[None, :, None, :]
    x1, x2 = x[..., :half], x[..., half:]
    return jnp.concatenate([x1 * cos - x2 * sin, x1 * sin + x2 * cos], axis=-1)


def ret_log_gamma():
    return jnp.log1p(-(2.0 ** (-5.0 - jnp.arange(RET_HEADS, dtype=F32))))


def retention_chunk(state, qkv):
    q, k, v = qkv
    cl = q.shape[1]
    lg = ret_log_gamma()
    i = jnp.arange(cl, dtype=F32)
    diff = i[:, None] - i[None, :]
    dmask = jnp.where(diff >= 0, jnp.exp(jnp.maximum(diff, 0.0)[None] * lg[:, None, None]), 0.0)
    inner = jnp.einsum('bihd,bjhd->bhij', q, k) * dmask[None]
    out = jnp.einsum('bhij,bjhe->bihe', inner, v)
    out = out + jnp.einsum('bihd,bhde->bihe', q, state) * jnp.exp((i + 1.0)[:, None] * lg[None, :])[None, :, :, None]
    k_dec = k * jnp.exp((cl - 1.0 - i)[:, None] * lg[None, :])[None, :, :, None]
    new_state = jnp.exp(cl * lg)[None, :, None, None] * state + jnp.einsum('bjhd,bjhe->bhde', k_dec, v)
    return new_state, out


def odd_mixer_core(q, k, v, g, s0, start, gn_gain):
    bsz, seq_len, _ = q.shape
    vd = RET_HEADS * RET_VD
    pos = start + jnp.arange(seq_len)
    q = rotary(q.reshape(bsz, seq_len, RET_HEADS, RET_KD), pos)
    k = rotary(k.reshape(bsz, seq_len, RET_HEADS, RET_KD), pos) * RET_KD ** -0.5
    v = v.reshape(bsz, seq_len, RET_HEADS, RET_VD)
    cl = min(RET_CHUNK, seq_len)
    nc = seq_len // cl
    to_chunks = lambda t: t.reshape((bsz, nc, cl) + t.shape[2:]).swapaxes(0, 1)
    s_final, oc = lax.scan(retention_chunk, s0, (to_chunks(q), to_chunks(k), to_chunks(v)))
    o = oc.swapaxes(0, 1).reshape(bsz, seq_len, RET_HEADS, RET_VD)
    mu = jnp.mean(o, axis=-1, keepdims=True)
    var = jnp.mean(jnp.square(o - mu), axis=-1, keepdims=True)
    o = ((o - mu) * lax.rsqrt(var + EPS)).reshape(bsz, seq_len, vd) * gn_gain.astype(F32)
    return jax.nn.silu(g) * o, s_final


def _split(t, width):
    return t[:N_PROMPT].reshape(BATCH, SEQ, width), t[N_PROMPT:].reshape(DEC_BATCH, DEC_SEQ, width)


def _join(p, s):
    return jnp.concatenate([p.reshape(N_PROMPT, -1), s.reshape(N_SAMPLE, -1)], axis=0)


def kernel(x_prompt, x_sample, cache_k, cache_v, cache_logf, state_s5_re, state_s5_im, state_ret, page_table, norm_gain, w_ffn_gate, w_ffn_up, w_ffn_down, w_in_even, b_forget, s5_lam_re, s5_lam_im, s5_b_re, s5_b_im, s5_c_re, s5_c_im, s5_d, s5_log_dt, w_glu, b_glu, w_out_even, w_in_odd, ret_gn_gain, w_out_odd, final_norm):
    def gather_pages(pool):
        g = pool[page_table]
        return g.reshape((g.shape[0], g.shape[1] * g.shape[2]) + g.shape[3:])

    x = _join(x_prompt, x_sample)
    ev_p, ev_s, od_p, od_s = [], [], [], []
    y = None
    for layer in range(DEPTH):
        h = ffn(x, norm_gain[layer, 0], w_ffn_gate[layer, 0], w_ffn_up[layer, 0], w_ffn_down[layer, 0])
        if layer % 2 == 0:
            e = layer // 2
            z, zf = norm_proj(h, norm_gain[layer, 1], w_in_even[e], 4 * 1024, narrow=True)
            parts_p, parts_s = zip(*[_split(z[j], 1024) for j in range(4)])
            zf_p, zf_s = _split(zf[:, :FOX_HEADS], FOX_HEADS)
            s5_params = (s5_lam_re[e], s5_lam_im[e], s5_b_re[e], s5_b_im[e], s5_c_re[e], s5_c_im[e],
                         s5_d[e], s5_log_dt[e], w_glu[e], b_glu[e])
            past = (gather_pages(cache_k[e]), gather_pages(cache_v[e]), gather_pages(cache_logf[e]),
                    lax.complex(state_s5_re[e], state_s5_im[e]))
            m_p, st_p = even_mixer_core(*parts_p, zf_p, None, b_forget[e], s5_params)
            m_s, st_s = even_mixer_core(*parts_s, zf_s, past, b_forget[e], s5_params)
            ev_p.append(st_p)
            ev_s.append(st_s)
            w_out = w_out_even[e]
        else:
            o = layer // 2
            z = norm_proj(h, norm_gain[layer, 1], w_in_odd[o], 12 * 1024)
            cat = lambda lo, hi: jnp.concatenate([z[j] for j in range(lo, hi)], axis=-1)
            q_p, q_s = _split(cat(0, 2), 2048)
            k_p, k_s = _split(cat(2, 4), 2048)
            v_p, v_s = _split(cat(4, 8), 4096)
            g_p, g_s = _split(cat(8, 12), 4096)
            s0_p = jnp.zeros((BATCH, RET_HEADS, RET_KD, RET_VD), F32)
            m_p, st_p = odd_mixer_core(q_p, k_p, v_p, g_p, s0_p, 0, ret_gn_gain[o])
            m_s, st_s = odd_mixer_core(q_s, k_s, v_s, g_s, state_ret[o], PAST_LEN, ret_gn_gain[o])
            od_p.append(st_p)
            od_s.append(st_s)
            w_out = w_out_odd[o]
        h = out_proj(_join(m_p, m_s).astype(BF16), w_out, h)
        if layer == DEPTH - 1:
            x, y = ffn(h, norm_gain[layer, 2], w_ffn_gate[layer, 1], w_ffn_up[layer, 1], w_ffn_down[layer, 1],
                       final_gain=final_norm)
        else:
            x = ffn(h, norm_gain[layer, 2], w_ffn_gate[layer, 1], w_ffn_up[layer, 1], w_ffn_down[layer, 1])

    y_prompt, y_sample = _split(y, D_MODEL)
    stack = lambda sts, i: jnp.stack([s[i] for s in sts])
    return (y_prompt, y_sample,
            stack(ev_p, 0), stack(ev_p, 1), stack(ev_p, 2), stack(ev_p, 3), stack(ev_p, 4), jnp.stack(od_p),
            stack(ev_s, 0), stack(ev_s, 1), stack(ev_s, 2), stack(ev_s, 3), stack(ev_s, 4), jnp.stack(od_s))
```

```python
import functools
import math

import jax
import jax.numpy as jnp
import numpy as np
from jax import lax
from jax.experimental import pallas as pl
from jax.experimental.pallas import tpu as pltpu

D_MODEL = 2048
BATCH = 2
SEQ = 4096
DEPTH = 2
DEC_BATCH = 128
DEC_SEQ = 4
PAST_LEN = 2048
PAGE_SIZE = 128
N_PAGES = PAST_LEN // PAGE_SIZE
S5_WIDTH = 1024
S5_GROUP = 16
S5_GROUPS = 64
S5_STATE = 64
FOX_WIDTH = 1024
FOX_HEAD_DIM = 128
FOX_HEADS = 8
NEG_INF = -1e30
RET_HEADS = 8
RET_KD = 256
RET_VD = 512
RET_CHUNK = 128
ROPE_BASE = 10000.0
D_FF = 5504
EPS = 1e-6

N_PROMPT = BATCH * SEQ
N_SAMPLE = DEC_BATCH * DEC_SEQ
N_TOK = N_PROMPT + N_SAMPLE

ROW_TILE = 1088
FF_TILE = 256
LANES = 128
VMEM_LIMIT = 56 * 1024 * 1024

F32 = jnp.float32
BF16 = jnp.bfloat16
NT_DIMS = (((1,), (1,)), ((), ()))


def _rms_rows(x, gain):
    return x * lax.rsqrt(jnp.mean(x * x, axis=-1, keepdims=True) + EPS) * gain


def _whole(shape):
    return pl.BlockSpec(shape, lambda *_: (0,) * len(shape))


def _params(*semantics):
    return pltpu.CompilerParams(dimension_semantics=semantics, vmem_limit_bytes=VMEM_LIMIT)


def _ffn_kernel(x_ref, g_ref, wg_ref, wu_ref, wd_ref, *rest, d_ff, final):
    if final:
        fg_ref, o_ref, y_ref, h_ref = rest
    else:
        o_ref, h_ref = rest
    f = pl.program_id(1)
    tf = wg_ref.shape[1]

    @pl.when(f == 0)
    def _():
        x = x_ref[...]
        h_ref[...] = _rms_rows(x, g_ref[...]).astype(BF16)
        o_ref[...] = x

    h = h_ref[...]
    a = jnp.dot(h, wg_ref[...].astype(BF16), preferred_element_type=F32)
    b = jnp.dot(h, wu_ref[...].astype(BF16), preferred_element_type=F32)
    c = a * jax.nn.sigmoid(a) * b
    col = f * tf + lax.broadcasted_iota(jnp.int32, (1, tf), 1)
    c = jnp.where(col < d_ff, c, 0.0).astype(BF16)
    row = f * tf + lax.broadcasted_iota(jnp.int32, (tf, 1), 0)
    wd = jnp.where(row < d_ff, wd_ref[...], 0.0).astype(BF16)
    o_ref[...] += 0.5 * jnp.dot(c, wd, preferred_element_type=F32)

    if final:
        @pl.when(f == pl.num_programs(1) - 1)
        def _():
            y_ref[...] = _rms_rows(o_ref[...], fg_ref[...])


def ffn(x, gain, wg, wu, wd, final_gain=None):
    m, d = x.shape
    d_ff = wg.shape[1]
    final = final_gain is not None
    grid = (m // ROW_TILE, pl.cdiv(d_ff, FF_TILE))
    row_spec = pl.BlockSpec((ROW_TILE, d), lambda i, f: (i, 0))
    single = pl.BlockSpec((ROW_TILE, d), lambda i, f: (i, 0), pipeline_mode=pl.Buffered(1))
    vec_spec = pl.BlockSpec((1, d), lambda i, f: (0, 0))
    in_specs = [
        single,
        vec_spec,
        pl.BlockSpec((d, FF_TILE), lambda i, f: (0, f)),
        pl.BlockSpec((d, FF_TILE), lambda i, f: (0, f)),
        pl.BlockSpec((FF_TILE, d), lambda i, f: (f, 0)),
    ]
    args = [x, gain.reshape(1, d), wg, wu, wd]
    out_shape = jax.ShapeDtypeStruct((m, d), F32)
    out_specs = row_spec
    if final:
        in_specs.append(vec_spec)
        args.append(final_gain.reshape(1, d))
        out_shape = (out_shape, jax.ShapeDtypeStruct((m, d), F32))
        out_specs = (row_spec, single)
    return pl.pallas_call(
        functools.partial(_ffn_kernel, d_ff=d_ff, final=final),
        grid=grid,
        in_specs=in_specs,
        out_specs=out_specs,
        out_shape=out_shape,
        scratch_shapes=[pltpu.VMEM((ROW_TILE, d), BF16)],
        compiler_params=_params("parallel", "arbitrary"),
        name="ffn_final" if final else "ffn",
    )(*args)


PROJ_TILE = 1024


def _norm_proj_kernel(x_ref, g_ref, w_ref, *rest, narrow):
    if narrow:
        wn_ref, z_ref, zn_ref, h_ref = rest
    else:
        z_ref, h_ref = rest
    j = pl.program_id(1)

    @pl.when(j == 0)
    def _():
        h_ref[...] = _rms_rows(x_ref[...], g_ref[...]).astype(BF16)
        if narrow:
            zn_ref[...] = jnp.dot(h_ref[...], wn_ref[...].astype(BF16), preferred_element_type=F32)

    z_ref[0] = jnp.dot(h_ref[...], w_ref[...].astype(BF16), preferred_element_type=F32)


def norm_proj(x, gain, w, n_wide, narrow=False):
    m, d = x.shape
    nj = n_wide // PROJ_TILE
    in_specs = [
        pl.BlockSpec((ROW_TILE, d), lambda i, j: (i, 0), pipeline_mode=pl.Buffered(1)),
        pl.BlockSpec((1, d), lambda i, j: (0, 0)),
        pl.BlockSpec((d, PROJ_TILE), lambda i, j: (0, j)),
    ]
    args = [x, gain.reshape(1, d), w]
    out_shape = jax.ShapeDtypeStruct((nj, m, PROJ_TILE), F32)
    out_specs = pl.BlockSpec((1, ROW_TILE, PROJ_TILE), lambda i, j: (j, i, 0))
    if narrow:
        in_specs.append(pl.BlockSpec((d, LANES), lambda i, j: (0, n_wide // LANES)))
        args.append(w)
        out_shape = (out_shape, jax.ShapeDtypeStruct((m, LANES), F32))
        out_specs = (out_specs, pl.BlockSpec((ROW_TILE, LANES), lambda i, j: (i, 0)))
    return pl.pallas_call(
        functools.partial(_norm_proj_kernel, narrow=narrow),
        grid=(m // ROW_TILE, nj),
        in_specs=in_specs,
        out_specs=out_specs,
        out_shape=out_shape,
        scratch_shapes=[pltpu.VMEM((ROW_TILE, d), BF16)],
        compiler_params=_params("parallel", "arbitrary"),
        name="norm_proj_narrow" if narrow else "norm_proj",
    )(*args)


OUT_TILE = 512


def _out_proj_kernel(*refs, n_parts):
    a_refs, w_refs, (r_ref, o_ref) = refs[:n_parts], refs[n_parts:2 * n_parts], refs[2 * n_parts:]
    acc = r_ref[...]
    for a_ref, w_ref in zip(a_refs, w_refs):
        acc = acc + jnp.dot(a_ref[...], w_ref[...].astype(BF16), preferred_element_type=F32)
    o_ref[...] = acc


def out_proj(parts, w, res):
    m, k = parts[0].shape
    n = w.shape[1]
    n_parts = len(parts)
    return pl.pallas_call(
        functools.partial(_out_proj_kernel, n_parts=n_parts),
        grid=(m // ROW_TILE, n // OUT_TILE),
        in_specs=([pl.BlockSpec((ROW_TILE, k), lambda i, j: (i, 0))] * n_parts
                  + [pl.BlockSpec((k, OUT_TILE), functools.partial(lambda i, j, p: (p, j), p=p))
                     for p in range(n_parts)]
                  + [pl.BlockSpec((ROW_TILE, OUT_TILE), lambda i, j: (i, j))]),
        out_specs=pl.BlockSpec((ROW_TILE, OUT_TILE), lambda i, j: (i, j)),
        out_shape=jax.ShapeDtypeStruct((m, n), F32),
        compiler_params=_params("parallel", "arbitrary"),
        name="out_proj",
    )(*parts, *([w] * n_parts), res)


S5_LANES = S5_GROUPS * S5_STATE
S5_BLOCKS = 4
S5_BLOCK_CH = S5_WIDTH // S5_BLOCKS
S5_BLOCK_ST = S5_LANES // S5_BLOCKS
SCAN_LANES = 256
S5_TILE = 256


def _s5_param_kernel(lr_ref, li_ref, ldt_ref, bre_ref, bim_ref, tab_ref, bbr_ref, bbi_ref):
    lr = lr_ref[...]
    li = li_ref[...]
    dt = jnp.exp(ldt_ref[...])
    dec = lr * dt
    ang = li * dt
    er = jnp.exp(dec)
    xr = er * jnp.cos(ang) - 1.0
    xi = er * jnp.sin(ang)
    den = lr * lr + li * li
    cr = (xr * lr + xi * li) / den
    ci = (xi * lr - xr * li) / den
    bre = bre_ref[...]
    bim = bim_ref[...]
    bbr_ref[...] = cr * bre - ci * bim
    bbi_ref[...] = cr * bim + ci * bre

    row = lax.broadcasted_iota(jnp.int32, (8, S5_LANES), 0)

    def power(k):
        mag = jnp.exp(k * dec)
        return mag * jnp.cos(k * ang), mag * jnp.sin(k * ang)

    for idx, s in enumerate((1, 2, 4)):
        pr, pi = power(float(s))
        tab_ref[2 * idx] = jnp.where(row >= s, pr, 0.0)
        tab_ref[2 * idx + 1] = jnp.where(row >= s, pi, 0.0)
    pr, pi = power((row + 1).astype(F32))
    tab_ref[6] = pr
    tab_ref[7] = pi


def s5_params(lam_re, lam_im, log_dt, b_re, b_im):
    flat = lambda t: t.reshape(1, S5_LANES)
    ldt = jnp.broadcast_to(log_dt[:, None], (S5_GROUPS, S5_STATE))
    to_rows = lambda t: t.reshape(S5_LANES, S5_GROUP).T
    return pl.pallas_call(
        _s5_param_kernel,
        out_shape=(jax.ShapeDtypeStruct((8, 8, S5_LANES), F32),
                   jax.ShapeDtypeStruct((S5_GROUP, S5_LANES), F32),
                   jax.ShapeDtypeStruct((S5_GROUP, S5_LANES), F32)),
        name="s5_params",
    )(flat(lam_re), flat(lam_im), flat(ldt), to_rows(b_re), to_rows(b_im))


def _block_diag_in(bbr, bbi):
    bb = jnp.stack([bbr, bbi]).reshape(2, S5_GROUP, S5_BLOCKS, 16, S5_STATE)
    eye = jnp.eye(16, dtype=F32)
    t = bb[:, :, :, :, None, :] * eye[None, None, None, :, :, None]
    return t.transpose(2, 3, 1, 0, 4, 5).reshape(S5_BLOCKS, S5_BLOCK_CH, 2 * S5_BLOCK_ST).astype(BF16)


def _block_diag_out(c):
    cc = c.reshape(S5_BLOCKS, 16, S5_GROUP, S5_STATE)
    eye = jnp.eye(16, dtype=F32)
    t = cc.transpose(0, 1, 3, 2)[:, :, :, None, :] * eye[None, :, None, :, None]
    return t.reshape(S5_BLOCKS, S5_BLOCK_ST, S5_BLOCK_CH).astype(BF16)


def _s5_drive(ub, bbd_ref, hr_ref, hi_ref):
    for blk in range(S5_BLOCKS):
        bu = jnp.dot(ub[:, blk * S5_BLOCK_CH:(blk + 1) * S5_BLOCK_CH], bbd_ref[blk], preferred_element_type=F32)
        hr_ref[:, blk * S5_BLOCK_ST:(blk + 1) * S5_BLOCK_ST] = bu[:, :S5_BLOCK_ST]
        hi_ref[:, blk * S5_BLOCK_ST:(blk + 1) * S5_BLOCK_ST] = bu[:, S5_BLOCK_ST:]


def _s5_readout(u, hr_ref, hi_ref, cr_ref, ci_ref, d_ref, wg_ref, bg_ref):
    ys = []
    for blk in range(S5_BLOCKS):
        sl = slice(blk * S5_BLOCK_ST, (blk + 1) * S5_BLOCK_ST)
        yr = jnp.dot(hr_ref[:, sl].astype(BF16), cr_ref[blk], preferred_element_type=F32)
        yi = jnp.dot(hi_ref[:, sl].astype(BF16), ci_ref[blk], preferred_element_type=F32)
        ys.append(yr - yi)
    y = jax.nn.gelu(jnp.concatenate(ys, axis=1) + d_ref[...] * u)
    gate = jax.nn.sigmoid(jnp.dot(y.astype(BF16), wg_ref[...], preferred_element_type=F32) + bg_ref[...])
    return (y * gate).astype(BF16)


def _s5_seq_kernel(u_ref, tab_ref, bbd_ref, cr_ref, ci_ref, d_ref, wg_ref, bg_ref,
                   o_ref, hr_out, hi_out, hr_ref, hi_ref, car_r, car_i):
    step = pl.program_id(1)

    @pl.when(step == 0)
    def _():
        car_r[...] = jnp.zeros_like(car_r)
        car_i[...] = jnp.zeros_like(car_i)

    u = u_ref[0]
    _s5_drive(u.astype(BF16), bbd_ref, hr_ref, hi_ref)

    n_tiles = u.shape[0] // 8
    for c in range(S5_LANES // SCAN_LANES):
        sl = slice(c * SCAN_LANES, (c + 1) * SCAN_LANES)
        tabs = [tab_ref[k, :, sl] for k in range(8)]

        def tile(i, carry, sl=sl, tabs=tabs):
            cr, ci = carry
            rows = pl.ds(pl.multiple_of(i * 8, 8), 8)
            xr = hr_ref[rows, sl]
            xi = hi_ref[rows, sl]
            for k, s in enumerate((1, 2, 4)):
                ar, ai = tabs[2 * k], tabs[2 * k + 1]
                rr = pltpu.roll(xr, s, 0)
                ri = pltpu.roll(xi, s, 0)
                xr, xi = xr + ar * rr - ai * ri, xi + ar * ri + ai * rr
            pr, pi = tabs[6], tabs[7]
            hr = xr + pr * cr - pi * ci
            hi = xi + pr * ci + pi * cr
            hr_ref[rows, sl] = hr
            hi_ref[rows, sl] = hi
            return (jnp.broadcast_to(hr[7:8], hr.shape), jnp.broadcast_to(hi[7:8], hi.shape))

        cr, ci = lax.fori_loop(0, n_tiles, tile, (car_r[:, sl], car_i[:, sl]), unroll=2)
        car_r[:, sl] = cr
        car_i[:, sl] = ci

    o_ref[...] = _s5_readout(u, hr_ref, hi_ref, cr_ref, ci_ref, d_ref, wg_ref, bg_ref)

    @pl.when(step == pl.num_programs(1) - 1)
    def _():
        hr_out[0] = car_r[0:1, :]
        hi_out[0] = car_i[0:1, :]


def _s5_weight_specs():
    return [_whole((8, 8, S5_LANES)), _whole((S5_BLOCKS, S5_BLOCK_CH, 2 * S5_BLOCK_ST)),
            _whole((S5_BLOCKS, S5_BLOCK_ST, S5_BLOCK_CH)), _whole((S5_BLOCKS, S5_BLOCK_ST, S5_BLOCK_CH)),
            _whole((1, S5_WIDTH)), _whole((S5_WIDTH, S5_WIDTH)), _whole((1, S5_WIDTH))]


def s5_sequences(z, weights, bsz, seq_len):
    steps = seq_len // S5_TILE
    state = jax.ShapeDtypeStruct((bsz, 1, S5_LANES), F32)
    state_spec = pl.BlockSpec((1, 1, S5_LANES), lambda b, s: (b, 0, 0))
    return pl.pallas_call(
        _s5_seq_kernel,
        grid=(bsz, steps),
        in_specs=[pl.BlockSpec((1, S5_TILE, S5_WIDTH), lambda b, s: (0, b * steps + s, 0))] + _s5_weight_specs(),
        out_specs=(pl.BlockSpec((S5_TILE, S5_WIDTH), lambda b, s: (b * steps + s, 0)), state_spec, state_spec),
        out_shape=(jax.ShapeDtypeStruct((bsz * seq_len, S5_WIDTH), BF16), state, state),
        scratch_shapes=[pltpu.VMEM((S5_TILE, S5_LANES), F32), pltpu.VMEM((S5_TILE, S5_LANES), F32),
                        pltpu.VMEM((8, S5_LANES), F32), pltpu.VMEM((8, S5_LANES), F32)],
        compiler_params=_params("parallel", "arbitrary"),
        name="s5_sequences",
    )(z, *weights)


def _s5_step_kernel(u_ref, h0r_ref, h0i_ref, tab_ref, bbd_ref, cr_ref, ci_ref, d_ref, wg_ref, bg_ref,
                    o_ref, hr_ref, hi_ref, bur_ref, bui_ref):
    @pl.when(pl.program_id(0) == 0)
    def _():
        hr_ref[...] = h0r_ref[...]
        hi_ref[...] = h0i_ref[...]

    u = u_ref[0]
    _s5_drive(u.astype(BF16), bbd_ref, bur_ref, bui_ref)
    ar = tab_ref[6, 0:1, :]
    ai = tab_ref[7, 0:1, :]
    hr = hr_ref[...]
    hi = hi_ref[...]
    hr_ref[...] = ar * hr - ai * hi + bur_ref[...]
    hi_ref[...] = ar * hi + ai * hr + bui_ref[...]
    o_ref[0] = _s5_readout(u, hr_ref, hi_ref, cr_ref, ci_ref, d_ref, wg_ref, bg_ref)


def s5_steps(u, h0_re, h0_im, weights):
    steps, rows, _ = u.shape
    state = jax.ShapeDtypeStruct((rows, S5_LANES), F32)
    return pl.pallas_call(
        _s5_step_kernel,
        grid=(steps,),
        in_specs=[pl.BlockSpec((1, rows, S5_WIDTH), lambda t: (t, 0, 0)),
                  _whole((rows, S5_LANES)), _whole((rows, S5_LANES))] + _s5_weight_specs(),
        out_specs=(pl.BlockSpec((1, rows, S5_WIDTH), lambda t: (t, 0, 0)),
                   _whole((rows, S5_LANES)), _whole((rows, S5_LANES))),
        out_shape=(jax.ShapeDtypeStruct((steps, rows, S5_WIDTH), BF16), state, state),
        scratch_shapes=[pltpu.VMEM((rows, S5_LANES), F32), pltpu.VMEM((rows, S5_LANES), F32)],
        compiler_params=_params("arbitrary"),
        name="s5_steps",
    )(u, h0_re, h0_im, *weights)


def s5_weights(lam_re, lam_im, log_dt, b_re, b_im, c_re, c_im, d_skip, w_glu, b_glu):
    tab, bbr, bbi = s5_params(lam_re, lam_im, log_dt, b_re, b_im)
    return (tab, _block_diag_in(bbr, bbi), _block_diag_out(c_re), _block_diag_out(c_im),
            d_skip.reshape(1, S5_WIDTH), w_glu.astype(BF16), b_glu.reshape(1, S5_WIDTH))


def _lane_pad_bias(b_f):
    return jnp.pad(b_f, (0, LANES - FOX_HEADS)).reshape(1, LANES)


def _log_forget(zf_ref, bf_ref):
    lane = lax.broadcasted_iota(jnp.int32, zf_ref.shape, 1)
    return jnp.where(lane < FOX_HEADS, jax.nn.log_sigmoid(zf_ref[...] + bf_ref[...]), 0.0)


def _gate_seq_kernel(zf_ref, bf_ref, lf_ref, c_ref, ct_ref):
    lf = _log_forget(zf_ref, bf_ref)
    lf_ref[...] = lf
    row = lax.broadcasted_iota(jnp.int32, lf.shape, 0)
    c = lf
    s = 1
    while s < lf.shape[0]:
        c = c + jnp.where(row >= s, pltpu.roll(c, s, 0), 0.0)
        s *= 2
    c_ref[0] = c
    ct_ref[0] = c.T[:FOX_HEADS]


def fox_gates_seq(zf, b_f, bsz, seq_len):
    return pl.pallas_call(
        _gate_seq_kernel,
        grid=(bsz,),
        in_specs=[pl.BlockSpec((seq_len, LANES), lambda b: (b, 0)), _whole((1, LANES))],
        out_specs=(pl.BlockSpec((seq_len, LANES), lambda b: (b, 0)),
                   pl.BlockSpec((1, seq_len, LANES), lambda b: (b, 0, 0)),
                   pl.BlockSpec((1, FOX_HEADS, seq_len), lambda b: (b, 0, 0))),
        out_shape=(jax.ShapeDtypeStruct((bsz * seq_len, LANES), F32),
                   jax.ShapeDtypeStruct((bsz, seq_len, LANES), F32),
                   jax.ShapeDtypeStruct((bsz, FOX_HEADS, seq_len), F32)),
        compiler_params=_params("parallel"),
        name="fox_gates_seq",
    )(zf, _lane_pad_bias(b_f))


def _gate_kernel(zf_ref, bf_ref, lf_ref):
    lf_ref[...] = _log_forget(zf_ref, bf_ref)


def fox_gates(zf, b_f):
    return pl.pallas_call(
        _gate_kernel,
        out_shape=jax.ShapeDtypeStruct(zf.shape, F32),
        name="fox_gates",
    )(zf, _lane_pad_bias(b_f))


FOX_TILE = 512
FOX_SCALE = FOX_HEAD_DIM ** -0.5


def _fox_seq_kernel(q_ref, k_ref, v_ref, c_ref, ct_ref, o_ref):
    h = pl.program_id(1)
    i = pl.program_id(2)
    q = q_ref[0].astype(BF16)
    lane = lax.broadcasted_iota(jnp.int32, c_ref.shape[1:], 1)
    cq = jnp.sum(jnp.where(lane == h, c_ref[0], 0.0), axis=1, keepdims=True)

    def chunk(j, carry, diagonal):
        m, l, acc = carry
        rows = pl.ds(pl.multiple_of(j * FOX_TILE, FOX_TILE), FOX_TILE)
        kc = k_ref[0, rows, :].astype(BF16)
        vc = v_ref[0, rows, :].astype(BF16)
        s = lax.dot_general(q, kc, NT_DIMS, preferred_element_type=F32) * FOX_SCALE
        s = s + cq - ct_ref[0, h, pl.ds(j, 1), :]
        if diagonal:
            qi = lax.broadcasted_iota(jnp.int32, s.shape, 0)
            ki = lax.broadcasted_iota(jnp.int32, s.shape, 1)
            s = jnp.where(qi >= ki, s, NEG_INF)
        m_new = jnp.maximum(m, jnp.max(s, axis=1, keepdims=True))
        alpha = jnp.exp(m - m_new)
        p = jnp.exp(s - m_new)
        l = alpha * l + jnp.sum(p, axis=1, keepdims=True)
        acc = alpha * acc + jnp.dot(p.astype(BF16), vc, preferred_element_type=F32)
        return m_new, l, acc

    init = (jnp.full((FOX_TILE, 1), NEG_INF, F32), jnp.zeros((FOX_TILE, 1), F32),
            jnp.zeros((FOX_TILE, FOX_HEAD_DIM), F32))
    carry = lax.fori_loop(0, i, functools.partial(chunk, diagonal=False), init)
    _, l, acc = chunk(i, carry, diagonal=True)
    o_ref[...] = (acc / l).astype(BF16)


def fox_sequences(z, c, ct, bsz, seq_len):
    nq = seq_len // FOX_TILE
    ct = ct.reshape(bsz, FOX_HEADS, nq, FOX_TILE)
    return pl.pallas_call(
        _fox_seq_kernel,
        grid=(bsz, FOX_HEADS, nq),
        in_specs=[
            pl.BlockSpec((1, FOX_TILE, FOX_HEAD_DIM), lambda b, h, i: (1, b * nq + i, h)),
            pl.BlockSpec((1, seq_len, FOX_HEAD_DIM), lambda b, h, i: (2, b, h)),
            pl.BlockSpec((1, seq_len, FOX_HEAD_DIM), lambda b, h, i: (3, b, h)),
            pl.BlockSpec((1, FOX_TILE, LANES), lambda b, h, i: (b, i, 0)),
            pl.BlockSpec((1, FOX_HEADS, nq, FOX_TILE), lambda b, h, i: (b, 0, 0, 0)),
        ],
        out_specs=pl.BlockSpec((FOX_TILE, FOX_HEAD_DIM), lambda b, h, i: (b * nq + i, h)),
        out_shape=jax.ShapeDtypeStruct((bsz * seq_len, FOX_WIDTH), BF16),
        compiler_params=_params("parallel", "parallel", "arbitrary"),
        name="fox_sequences",
    )(z, z, z, c, ct)


def _fox_paged_kernel(pt_ref, qkv_ref, ln_ref, *refs):
    k_pages = refs[:N_PAGES]
    v_pages = refs[N_PAGES:2 * N_PAGES]
    lf_pages = refs[2 * N_PAGES:3 * N_PAGES]
    o_ref, kpad_ref, vpad_ref = refs[3 * N_PAGES:]
    del pt_ref
    n_rows = DEC_SEQ * FOX_HEADS
    q = qkv_ref[0, 0]
    rows = jnp.concatenate([jnp.broadcast_to(q[t:t + 1], (FOX_HEADS, FOX_WIDTH)) for t in range(DEC_SEQ)], axis=0)
    row_head = jnp.bitwise_and(lax.broadcasted_iota(jnp.int32, (n_rows, FOX_WIDTH), 0), FOX_HEADS - 1)
    lane_head = jnp.right_shift(lax.broadcasted_iota(jnp.int32, (n_rows, FOX_WIDTH), 1), 7)
    own = row_head == lane_head
    qbd = jnp.where(own, rows, 0.0).astype(BF16)

    c = jnp.concatenate([r[0] for r in lf_pages], axis=1)
    lane = lax.broadcasted_iota(jnp.int32, c.shape, 1)
    s = 1
    while s < PAST_LEN:
        c = c + jnp.where(lane >= s, pltpu.roll(c, s, 1), 0.0)
        s *= 2
    ln = ln_ref[0]
    c_new = [c[:, PAST_LEN - 1:PAST_LEN] + ln[:, 0:1]]
    for t in range(1, DEC_SEQ):
        c_new.append(c_new[-1] + ln[:, t:t + 1])
    pad_lane = lax.broadcasted_iota(jnp.int32, (FOX_HEADS, PAGE_SIZE), 1)
    c_pad = jnp.zeros((FOX_HEADS, PAGE_SIZE), F32)
    for t in range(DEC_SEQ):
        c_pad = jnp.where(pad_lane == t, c_new[t], c_pad)
    ck = jnp.concatenate([c, c_pad], axis=1)

    kpad_ref[...] = jnp.zeros_like(kpad_ref)
    vpad_ref[...] = jnp.zeros_like(vpad_ref)
    kpad_ref[0:DEC_SEQ, :] = qkv_ref[1, 0]
    vpad_ref[0:DEC_SEQ, :] = qkv_ref[2, 0]

    s_pages = [lax.dot_general(qbd, kp[0].astype(BF16), NT_DIMS, preferred_element_type=F32) for kp in k_pages]
    s_pages.append(lax.dot_general(qbd, kpad_ref[...].astype(BF16), NT_DIMS, preferred_element_type=F32))
    scores = jnp.concatenate(s_pages, axis=1) * FOX_SCALE
    key_pos = lax.broadcasted_iota(jnp.int32, ck.shape, 1)
    probs, sums = [], []
    for t in range(DEC_SEQ):
        st = scores[t * FOX_HEADS:(t + 1) * FOX_HEADS] + c_new[t] - ck
        st = jnp.where(key_pos <= PAST_LEN + t, st, NEG_INF)
        p = jnp.exp(st - jnp.max(st, axis=1, keepdims=True))
        probs.append(p)
        sums.append(jnp.sum(p, axis=1, keepdims=True))
    p = jnp.concatenate(probs, axis=0).astype(BF16)
    acc = jnp.dot(p[:, PAST_LEN:], vpad_ref[...].astype(BF16), preferred_element_type=F32)
    for j, vp in enumerate(v_pages):
        acc = acc + jnp.dot(p[:, j * PAGE_SIZE:(j + 1) * PAGE_SIZE], vp[0].astype(BF16), preferred_element_type=F32)
    acc = jnp.where(own, acc / jnp.concatenate(sums, axis=0), 0.0)
    o_ref[0] = jnp.concatenate(
        [jnp.sum(acc[t * FOX_HEADS:(t + 1) * FOX_HEADS], axis=0, keepdims=True) for t in range(DEC_SEQ)], axis=0)


def fox_paged(qkv, logf_new_t, cache_k, cache_v, cache_logf_t, page_table):
    bsz = qkv.shape[1]

    def page(j):
        return lambda b, pt: (pt[b, j], 0, 0)

    in_specs = [pl.BlockSpec((3, 1, DEC_SEQ, FOX_WIDTH), lambda b, pt: (0, b, 0, 0)),
                pl.BlockSpec((1, FOX_HEADS, DEC_SEQ), lambda b, pt: (b, 0, 0))]
    in_specs += [pl.BlockSpec((1, PAGE_SIZE, FOX_WIDTH), page(j)) for j in range(N_PAGES)]
    in_specs += [pl.BlockSpec((1, PAGE_SIZE, FOX_WIDTH), page(j)) for j in range(N_PAGES)]
    in_specs += [pl.BlockSpec((1, FOX_HEADS, PAGE_SIZE), page(j)) for j in range(N_PAGES)]
    return pl.pallas_call(
        _fox_paged_kernel,
        grid_spec=pltpu.PrefetchScalarGridSpec(
            num_scalar_prefetch=1,
            grid=(bsz,),
            in_specs=in_specs,
            out_specs=pl.BlockSpec((1, DEC_SEQ, FOX_WIDTH), lambda b, pt: (b, 0, 0)),
            scratch_shapes=[pltpu.VMEM((PAGE_SIZE, FOX_WIDTH), F32), pltpu.VMEM((PAGE_SIZE, FOX_WIDTH), F32)],
        ),
        out_shape=jax.ShapeDtypeStruct((bsz, DEC_SEQ, FOX_WIDTH), F32),
        compiler_params=_params("parallel"),
        name="fox_paged",
    )(page_table, qkv, logf_new_t, *([cache_k] * N_PAGES), *([cache_v] * N_PAGES), *([cache_logf_t] * N_PAGES))


ROPE_HALF = RET_KD // 2
RET_SCALE = RET_KD ** -0.5
RET_LOG_GAMMA = [float(np.log1p(-np.float32(2.0 ** (-5.0 - h)))) for h in range(RET_HEADS)]


def _rope_kernel(f_ref, cos_ref, sin_ref, *, start):
    pos = (start + lax.broadcasted_iota(jnp.int32, cos_ref.shape, 0)).astype(F32)
    ang = pos * f_ref[...]
    cos_ref[...] = jnp.cos(ang)
    sin_ref[...] = jnp.sin(ang)


def rope_table(start, n):
    inv_freq = ROPE_BASE ** (-jnp.arange(ROPE_HALF, dtype=F32) / ROPE_HALF)
    shape = jax.ShapeDtypeStruct((n, ROPE_HALF), F32)
    return pl.pallas_call(
        functools.partial(_rope_kernel, start=start), out_shape=(shape, shape), name="rope_table",
    )(inv_freq.reshape(1, ROPE_HALF))


def _rotate(x, cos, sin):
    x1, x2 = x[:, :ROPE_HALF], x[:, ROPE_HALF:]
    return jnp.concatenate([x1 * cos - x2 * sin, x1 * sin + x2 * cos], axis=1)


def _group_norm_gate(o, g, gain):
    mu = jnp.mean(o, axis=-1, keepdims=True)
    var = jnp.mean(jnp.square(o - mu), axis=-1, keepdims=True)
    return g * jax.nn.sigmoid(g) * ((o - mu) * lax.rsqrt(var + EPS) * gain)


def _head_slabs(refs, h, per_slab):
    width = PROJ_TILE // per_slab
    return refs[h // per_slab], slice((h % per_slab) * width, (h % per_slab + 1) * width)


def _ret_seq_kernel(*refs):
    q_refs, k_refs, v_refs, g_refs = refs[0:2], refs[2:4], refs[4:8], refs[8:12]
    cos_ref, sin_ref, gain_ref, o_ref, s_out, s_ref = refs[12:]
    step = pl.program_id(1)

    @pl.when(step == 0)
    def _():
        s_ref[...] = jnp.zeros_like(s_ref)

    cos = cos_ref[...]
    sin = sin_ref[...]
    cl = RET_CHUNK
    i_col = lax.broadcasted_iota(jnp.int32, (cl, 1), 0).astype(F32)
    diff = i_col - lax.broadcasted_iota(jnp.int32, (1, cl), 1).astype(F32)
    for h in range(RET_HEADS):
        lg = RET_LOG_GAMMA[h]
        ref, sl = _head_slabs(q_refs, h, 4)
        q = _rotate(ref[0, :, sl], cos, sin).astype(BF16)
        ref, sl = _head_slabs(k_refs, h, 4)
        k = _rotate(ref[0, :, sl], cos, sin) * RET_SCALE
        ref, sl = _head_slabs(v_refs, h, 2)
        v = ref[0, :, sl].astype(BF16)
        ref, sl = _head_slabs(g_refs, h, 2)
        g = ref[0, :, sl]
        dmask = jnp.where(diff >= 0, jnp.exp(jnp.maximum(diff, 0.0) * lg), 0.0)
        inner = lax.dot_general(q, k.astype(BF16), NT_DIMS, preferred_element_type=F32) * dmask
        state = s_ref[h]
        out = jnp.dot(inner.astype(BF16), v, preferred_element_type=F32)
        out = out + jnp.dot(q, state.astype(BF16), preferred_element_type=F32) * jnp.exp((i_col + 1.0) * lg)
        k_dec = k * jnp.exp((cl - 1.0 - i_col) * lg)
        s_ref[h] = math.exp(cl * lg) * state + jnp.dot(k_dec.T.astype(BF16), v, preferred_element_type=F32)
        hs = slice(h * RET_VD, (h + 1) * RET_VD)
        o_ref[:, hs] = _group_norm_gate(out, g, gain_ref[:, hs]).astype(BF16)

    @pl.when(step == pl.num_programs(1) - 1)
    def _():
        s_out[0] = s_ref[...]


def ret_sequences(z, cos, sin, gain, bsz, seq_len):
    steps = seq_len // RET_CHUNK
    vd = RET_HEADS * RET_VD
    slab = lambda j: pl.BlockSpec((1, RET_CHUNK, PROJ_TILE), lambda b, c: (j, b * steps + c, 0))
    table = pl.BlockSpec((RET_CHUNK, ROPE_HALF), lambda b, c: (c, 0))
    return pl.pallas_call(
        _ret_seq_kernel,
        grid=(bsz, steps),
        in_specs=[slab(j) for j in range(12)] + [table, table, _whole((1, vd))],
        out_specs=(pl.BlockSpec((RET_CHUNK, vd), lambda b, c: (b * steps + c, 0)),
                   pl.BlockSpec((1, RET_HEADS, RET_KD, RET_VD), lambda b, c: (b, 0, 0, 0))),
        out_shape=(jax.ShapeDtypeStruct((bsz * seq_len, vd), BF16),
                   jax.ShapeDtypeStruct((bsz, RET_HEADS, RET_KD, RET_VD), F32)),
        scratch_shapes=[pltpu.VMEM((RET_HEADS, RET_KD, RET_VD), F32)],
        compiler_params=_params("parallel", "arbitrary"),
        name="ret_sequences",
    )(*([z] * 12), cos, sin, gain.reshape(1, vd))


RET_PAD = 128


def _ret_step_kernel(z_ref, s0_ref, cos_ref, sin_ref, gain_ref, o_ref, s_out, qpad, kpad, vpad, gpad):
    cl = DEC_SEQ
    for ref in (qpad, kpad, vpad, gpad):
        ref[...] = jnp.zeros_like(ref)
    cos = cos_ref[...]
    sin = sin_ref[...]
    i_col = lax.broadcasted_iota(jnp.int32, (RET_PAD, 1), 0).astype(F32)
    diff = i_col - lax.broadcasted_iota(jnp.int32, (1, RET_PAD), 1).astype(F32)
    valid = (diff >= 0) & (i_col < cl)
    outs = []
    for h in range(RET_HEADS):
        lg = RET_LOG_GAMMA[h]
        qpad[0:cl, :] = z_ref[h // 4, 0, :, (h % 4) * RET_KD:(h % 4 + 1) * RET_KD]
        kpad[0:cl, :] = z_ref[2 + h // 4, 0, :, (h % 4) * RET_KD:(h % 4 + 1) * RET_KD]
        vpad[0:cl, :] = z_ref[4 + h // 2, 0, :, (h % 2) * RET_VD:(h % 2 + 1) * RET_VD]
        gpad[0:cl, :] = z_ref[8 + h // 2, 0, :, (h % 2) * RET_VD:(h % 2 + 1) * RET_VD]
        q = _rotate(qpad[...], cos, sin).astype(BF16)
        k = _rotate(kpad[...], cos, sin) * RET_SCALE
        v = vpad[...].astype(BF16)
        dmask = jnp.where(valid, jnp.exp(jnp.maximum(diff, 0.0) * lg), 0.0)
        inner = lax.dot_general(q, k.astype(BF16), NT_DIMS, preferred_element_type=F32) * dmask
        state = s0_ref[0, h]
        out = jnp.dot(inner.astype(BF16), v, preferred_element_type=F32)
        out = out + jnp.dot(q, state.astype(BF16), preferred_element_type=F32) * jnp.exp((i_col + 1.0) * lg)
        k_dec = k * jnp.exp((cl - 1.0 - i_col) * lg)
        s_out[0, h] = math.exp(cl * lg) * state + jnp.dot(k_dec.T.astype(BF16), v, preferred_element_type=F32)
        hs = slice(h * RET_VD, (h + 1) * RET_VD)
        outs.append(_group_norm_gate(out[0:8], gpad[0:8, :], gain_ref[:, hs]))
    o_ref[0] = jnp.concatenate(outs, axis=1)[0:cl]


def ret_steps(z, s0, cos, sin, gain):
    bsz = z.shape[1]
    vd = RET_HEADS * RET_VD
    state_spec = pl.BlockSpec((1, RET_HEADS, RET_KD, RET_VD), lambda b: (b, 0, 0, 0))
    return pl.pallas_call(
        _ret_step_kernel,
        grid=(bsz,),
        in_specs=[pl.BlockSpec((12, 1, DEC_SEQ, PROJ_TILE), lambda b: (0, b, 0, 0)), state_spec,
                  _whole((RET_PAD, ROPE_HALF)), _whole((RET_PAD, ROPE_HALF)), _whole((1, vd))],
        out_specs=(pl.BlockSpec((1, DEC_SEQ, vd), lambda b: (b, 0, 0)), state_spec),
        out_shape=(jax.ShapeDtypeStruct((bsz, DEC_SEQ, vd), F32),
                   jax.ShapeDtypeStruct(s0.shape, F32)),
        scratch_shapes=[pltpu.VMEM((RET_PAD, RET_KD), F32), pltpu.VMEM((RET_PAD, RET_KD), F32),
                        pltpu.VMEM((RET_PAD, RET_VD), F32), pltpu.VMEM((RET_PAD, RET_VD), F32)],
        compiler_params=_params("parallel"),
        name="ret_steps",
    )(z, s0, cos, sin, gain.reshape(1, vd))


def _sample_rows(z):
    return z[:, N_PROMPT:].reshape(z.shape[0], DEC_BATCH, DEC_SEQ, z.shape[2])


def _even_mixer(h, gain, e, cache_k, cache_v, cache_logf, state_s5_re, state_s5_im, page_table, w_in, b_f,
                s5_w):
    z, zf = norm_proj(h, gain, w_in, 4 * PROJ_TILE, narrow=True)
    zs = _sample_rows(z)

    s5_p, hr_p, hi_p = s5_sequences(z, s5_w, BATCH, SEQ)
    s5_s, hr_s, hi_s = s5_steps(zs[0].transpose(1, 0, 2), state_s5_re.reshape(DEC_BATCH, S5_LANES),
                                state_s5_im.reshape(DEC_BATCH, S5_LANES), s5_w)
    s5_all = jnp.concatenate([s5_p, s5_s.transpose(1, 0, 2).reshape(N_SAMPLE, S5_WIDTH)], axis=0)

    lf_p, c_p, ct_p = fox_gates_seq(zf, b_f, BATCH, SEQ)
    lf_s = fox_gates(zf[N_PROMPT:], b_f)[:, :FOX_HEADS].reshape(DEC_BATCH, DEC_SEQ, FOX_HEADS)
    att_p = fox_sequences(z, c_p, ct_p, BATCH, SEQ)
    att_s = fox_paged(zs[1:4], lf_s.transpose(0, 2, 1),
                      cache_k.reshape(-1, PAGE_SIZE, FOX_WIDTH), cache_v.reshape(-1, PAGE_SIZE, FOX_WIDTH),
                      cache_logf.transpose(0, 2, 1), page_table)
    att_all = jnp.concatenate([att_p, att_s.reshape(N_SAMPLE, FOX_WIDTH).astype(BF16)], axis=0)

    heads = lambda t, b, s: t.reshape(b, s, FOX_HEADS, FOX_HEAD_DIM)
    grid = lambda t: t.reshape(-1, S5_GROUPS, S5_STATE)
    st_p = (heads(z[2, :N_PROMPT], BATCH, SEQ), heads(z[3, :N_PROMPT], BATCH, SEQ),
            lf_p[:N_PROMPT, :FOX_HEADS].reshape(BATCH, SEQ, FOX_HEADS), grid(hr_p), grid(hi_p))
    st_s = (heads(zs[2], DEC_BATCH, DEC_SEQ), heads(zs[3], DEC_BATCH, DEC_SEQ), lf_s, grid(hr_s), grid(hi_s))
    return [s5_all, att_all], st_p, st_s


def _odd_mixer(h, gain, w_in, gn_gain, state, tables):
    z = norm_proj(h, gain, w_in, 12 * PROJ_TILE)
    (cos_p, sin_p), (cos_s, sin_s) = tables
    o_p, st_p = ret_sequences(z, cos_p, sin_p, gn_gain, BATCH, SEQ)
    o_s, st_s = ret_steps(_sample_rows(z), state, cos_s, sin_s, gn_gain)
    o_all = jnp.concatenate([o_p, o_s.reshape(N_SAMPLE, RET_HEADS * RET_VD).astype(BF16)], axis=0)
    return [o_all], st_p, st_s


def kernel(x_prompt, x_sample, cache_k, cache_v, cache_logf, state_s5_re, state_s5_im, state_ret, page_table, norm_gain, w_ffn_gate, w_ffn_up, w_ffn_down, w_in_even, b_forget, s5_lam_re, s5_lam_im, s5_b_re, s5_b_im, s5_c_re, s5_c_im, s5_d, s5_log_dt, w_glu, b_glu, w_out_even, w_in_odd, ret_gn_gain, w_out_odd, final_norm):
    x = jnp.concatenate([x_prompt.reshape(N_PROMPT, D_MODEL), x_sample.reshape(N_SAMPLE, D_MODEL)], axis=0)
    tables = (rope_table(0, SEQ), rope_table(PAST_LEN, RET_PAD))
    ev_p, ev_s, od_p, od_s = [], [], [], []
    y = None
    for layer in range(DEPTH):
        h = ffn(x, norm_gain[layer, 0], w_ffn_gate[layer, 0], w_ffn_up[layer, 0], w_ffn_down[layer, 0])
        if layer % 2 == 0:
            e = layer // 2
            s5_w = s5_weights(s5_lam_re[e], s5_lam_im[e], s5_log_dt[e], s5_b_re[e], s5_b_im[e], s5_c_re[e],
                              s5_c_im[e], s5_d[e], w_glu[e], b_glu[e])
            parts, st_p, st_s = _even_mixer(h, norm_gain[layer, 1], e, cache_k[e], cache_v[e], cache_logf[e],
                                            state_s5_re[e], state_s5_im[e], page_table, w_in_even[e],
                                            b_forget[e], s5_w)
            ev_p.append(st_p)
            ev_s.append(st_s)
            w_out = w_out_even[e]
        else:
            o = layer // 2
            parts, st_p, st_s = _odd_mixer(h, norm_gain[layer, 1], w_in_odd[o], ret_gn_gain[o], state_ret[o],
                                           tables)
            od_p.append(st_p)
            od_s.append(st_s)
            w_out = w_out_odd[o]
        h = out_proj(parts, w_out, h)
        last = layer == DEPTH - 1
        x = ffn(h, norm_gain[layer, 2], w_ffn_gate[layer, 1], w_ffn_up[layer, 1], w_ffn_down[layer, 1],
                final_gain=final_norm if last else None)
        if last:
            x, y = x

    y_prompt = y[:N_PROMPT].reshape(BATCH, SEQ, D_MODEL)
    y_sample = y[N_PROMPT:].reshape(DEC_BATCH, DEC_SEQ, D_MODEL)
    stack = lambda sts, i: jnp.stack([s[i] for s in sts])
    return (y_prompt, y_sample,
            stack(ev_p, 0), stack(ev_p, 1), stack(ev_p, 2), stack(ev_p, 3), stack(ev_p, 4), jnp.stack(od_p),
            stack(ev_s, 0), stack(ev_s, 1), stack(ev_s, 2), stack(ev_s, 3), stack(ev_s, 4), jnp.stack(od_s))
```

```python
import functools
import math

import jax
import jax.numpy as jnp
import numpy as np
from jax import lax
from jax.experimental import pallas as pl
from jax.experimental.pallas import tpu as pltpu

D_MODEL = 2048
BATCH = 2
SEQ = 4096
DEPTH = 2
DEC_BATCH = 128
DEC_SEQ = 4
PAST_LEN = 2048
PAGE_SIZE = 128
N_PAGES = PAST_LEN // PAGE_SIZE
S5_WIDTH = 1024
S5_GROUP = 16
S5_GROUPS = 64
S5_STATE = 64
FOX_WIDTH = 1024
FOX_HEAD_DIM = 128
FOX_HEADS = 8
NEG_INF = -1e30
RET_HEADS = 8
RET_KD = 256
RET_VD = 512
RET_CHUNK = 128
ROPE_BASE = 10000.0
D_FF = 5504
EPS = 1e-6

N_PROMPT = BATCH * SEQ
N_SAMPLE = DEC_BATCH * DEC_SEQ
N_TOK = N_PROMPT + N_SAMPLE

ROW_TILE = 1088
FF_TILE = 256
LANES = 128
VMEM_LIMIT = 56 * 1024 * 1024
FFN_VMEM_LIMIT = 60 * 1024 * 1024

F32 = jnp.float32
BF16 = jnp.bfloat16
NT_DIMS = (((1,), (1,)), ((), ()))


def _rms_rows(x, gain):
    return x * lax.rsqrt(jnp.mean(x * x, axis=-1, keepdims=True) + EPS) * gain


def _whole(shape):
    return pl.BlockSpec(shape, lambda *_: (0,) * len(shape))


def _params(*semantics):
    return pltpu.CompilerParams(dimension_semantics=semantics, vmem_limit_bytes=VMEM_LIMIT)


FFN_GROUP = 2


def _ffn_kernel(x_ref, g_ref, wg_ref, wu_ref, wd_ref, o_ref, h_ref, *, d_ff):
    f = pl.program_id(1)
    rows = pl.ds(pl.multiple_of(pl.program_id(2) * ROW_TILE, ROW_TILE), ROW_TILE)
    tf = wg_ref.shape[1]

    @pl.when(f == 0)
    def _():
        x = x_ref[...]
        h_ref[rows, :] = _rms_rows(x, g_ref[...]).astype(BF16)
        o_ref[rows, :] = x

    h = h_ref[rows, :]
    a = jnp.dot(h, wg_ref[...].astype(BF16), preferred_element_type=F32)
    b = jnp.dot(h, wu_ref[...].astype(BF16), preferred_element_type=F32)
    c = a * jax.nn.sigmoid(a) * b
    col = f * tf + lax.broadcasted_iota(jnp.int32, (1, tf), 1)
    c = jnp.where(col < d_ff, c, 0.0).astype(BF16)
    row = f * tf + lax.broadcasted_iota(jnp.int32, (tf, 1), 0)
    wd = jnp.where(row < d_ff, wd_ref[...], 0.0).astype(BF16)
    o_ref[rows, :] += 0.5 * jnp.dot(c, wd, preferred_element_type=F32)


def ffn(x, gain, wg, wu, wd):
    m, d = x.shape
    d_ff = wg.shape[1]
    group_rows = FFN_GROUP * ROW_TILE
    x_index = lambda r, f, i: (jnp.where(f == 0, FFN_GROUP * r + i, FFN_GROUP * r + FFN_GROUP - 1), 0)
    return pl.pallas_call(
        functools.partial(_ffn_kernel, d_ff=d_ff),
        grid=(m // group_rows, pl.cdiv(d_ff, FF_TILE), FFN_GROUP),
        in_specs=[
            pl.BlockSpec((ROW_TILE, d), x_index, pipeline_mode=pl.Buffered(1)),
            pl.BlockSpec((1, d), lambda r, f, i: (0, 0)),
            pl.BlockSpec((d, FF_TILE), lambda r, f, i: (0, f)),
            pl.BlockSpec((d, FF_TILE), lambda r, f, i: (0, f)),
            pl.BlockSpec((FF_TILE, d), lambda r, f, i: (f, 0)),
        ],
        out_specs=pl.BlockSpec((group_rows, d), lambda r, f, i: (r, 0), pipeline_mode=pl.Buffered(1)),
        out_shape=jax.ShapeDtypeStruct((m, d), F32),
        scratch_shapes=[pltpu.VMEM((group_rows, d), BF16)],
        compiler_params=pltpu.CompilerParams(
            dimension_semantics=("parallel", "arbitrary", "arbitrary"), vmem_limit_bytes=FFN_VMEM_LIMIT),
        name="ffn",
    )(x, gain.reshape(1, d), wg, wu, wd)


SPLIT_TILE = 512
N_PROMPT_TILES = N_PROMPT // SPLIT_TILE


def _split_specs(block):
    zeros = (0,) * (len(block) - 1)
    return (pl.BlockSpec(block, lambda i: (jnp.minimum(i, N_PROMPT_TILES - 1),) + zeros),
            pl.BlockSpec(block, lambda i: (0,) + zeros))


def _final_norm_kernel(x_ref, g_ref, yp_ref, ys_ref):
    y = _rms_rows(x_ref[...], g_ref[...])

    @pl.when(pl.program_id(0) < N_PROMPT_TILES)
    def _():
        yp_ref[...] = y

    @pl.when(pl.program_id(0) == N_PROMPT_TILES)
    def _():
        ys_ref[...] = y


def final_rms(x, gain):
    d = x.shape[1]
    return pl.pallas_call(
        _final_norm_kernel,
        grid=(N_PROMPT_TILES + 1,),
        in_specs=[pl.BlockSpec((SPLIT_TILE, d), lambda i: (i, 0)), _whole((1, d))],
        out_specs=_split_specs((SPLIT_TILE, d)),
        out_shape=(jax.ShapeDtypeStruct((N_PROMPT, d), F32), jax.ShapeDtypeStruct((N_SAMPLE, d), F32)),
        compiler_params=_params("arbitrary"),
        name="final_norm",
    )(x, gain.reshape(1, d))


def _head_rows_kernel(k_ref, v_ref, kp_ref, vp_ref, ks_ref, vs_ref):
    def put(k_out, v_out):
        for h in range(FOX_HEADS):
            k_out[:, h, :] = k_ref[0, :, h * FOX_HEAD_DIM:(h + 1) * FOX_HEAD_DIM]
            v_out[:, h, :] = v_ref[0, :, h * FOX_HEAD_DIM:(h + 1) * FOX_HEAD_DIM]

    @pl.when(pl.program_id(0) < N_PROMPT_TILES)
    def _():
        put(kp_ref, vp_ref)

    @pl.when(pl.program_id(0) == N_PROMPT_TILES)
    def _():
        put(ks_ref, vs_ref)


def head_rows(z):
    block = (SPLIT_TILE, FOX_HEADS, FOX_HEAD_DIM)
    shape = lambda n: jax.ShapeDtypeStruct((n, FOX_HEADS, FOX_HEAD_DIM), F32)
    kp_spec, ks_spec = _split_specs(block)
    return pl.pallas_call(
        _head_rows_kernel,
        grid=(N_PROMPT_TILES + 1,),
        in_specs=[pl.BlockSpec((1, SPLIT_TILE, FOX_WIDTH), lambda i: (2, i, 0)),
                  pl.BlockSpec((1, SPLIT_TILE, FOX_WIDTH), lambda i: (3, i, 0))],
        out_specs=(kp_spec, kp_spec, ks_spec, ks_spec),
        out_shape=(shape(N_PROMPT), shape(N_PROMPT), shape(N_SAMPLE), shape(N_SAMPLE)),
        compiler_params=_params("arbitrary"),
        name="head_rows",
    )(z, z)


PROJ_TILE = 1024


def _norm_proj_kernel(x_ref, g_ref, w_ref, *rest, narrow):
    if narrow:
        wn_ref, z_ref, zn_ref, h_ref = rest
    else:
        z_ref, h_ref = rest
    j = pl.program_id(1)

    @pl.when(j == 0)
    def _():
        h_ref[...] = _rms_rows(x_ref[...], g_ref[...]).astype(BF16)
        if narrow:
            zn_ref[...] = jnp.dot(h_ref[...], wn_ref[...].astype(BF16), preferred_element_type=F32)

    z_ref[0] = jnp.dot(h_ref[...], w_ref[...].astype(BF16), preferred_element_type=F32)


def norm_proj(x, gain, w, n_wide, narrow=False):
    m, d = x.shape
    nj = n_wide // PROJ_TILE
    in_specs = [
        pl.BlockSpec((ROW_TILE, d), lambda i, j: (i, 0), pipeline_mode=pl.Buffered(1)),
        pl.BlockSpec((1, d), lambda i, j: (0, 0)),
        pl.BlockSpec((d, PROJ_TILE), lambda i, j: (0, j)),
    ]
    args = [x, gain.reshape(1, d), w]
    out_shape = jax.ShapeDtypeStruct((nj, m, PROJ_TILE), F32)
    out_specs = pl.BlockSpec((1, ROW_TILE, PROJ_TILE), lambda i, j: (j, i, 0))
    if narrow:
        in_specs.append(pl.BlockSpec((d, LANES), lambda i, j: (0, n_wide // LANES)))
        args.append(w)
        out_shape = (out_shape, jax.ShapeDtypeStruct((m, LANES), F32))
        out_specs = (out_specs, pl.BlockSpec((ROW_TILE, LANES), lambda i, j: (i, 0)))
    return pl.pallas_call(
        functools.partial(_norm_proj_kernel, narrow=narrow),
        grid=(m // ROW_TILE, nj),
        in_specs=in_specs,
        out_specs=out_specs,
        out_shape=out_shape,
        scratch_shapes=[pltpu.VMEM((ROW_TILE, d), BF16)],
        compiler_params=_params("parallel", "arbitrary"),
        name="norm_proj_narrow" if narrow else "norm_proj",
    )(*args)


OUT_TILE = 512


def _out_proj_kernel(*refs, n_parts):
    a_refs, w_refs, (r_ref, o_ref) = refs[:n_parts], refs[n_parts:2 * n_parts], refs[2 * n_parts:]
    acc = r_ref[...]
    for a_ref, w_ref in zip(a_refs, w_refs):
        acc = acc + jnp.dot(a_ref[...], w_ref[...].astype(BF16), preferred_element_type=F32)
    o_ref[...] = acc


def out_proj(parts, w, res):
    m, k = parts[0].shape
    n = w.shape[1]
    n_parts = len(parts)
    return pl.pallas_call(
        functools.partial(_out_proj_kernel, n_parts=n_parts),
        grid=(m // ROW_TILE, n // OUT_TILE),
        in_specs=([pl.BlockSpec((ROW_TILE, k), lambda i, j: (i, 0))] * n_parts
                  + [pl.BlockSpec((k, OUT_TILE), functools.partial(lambda i, j, p: (p, j), p=p))
                     for p in range(n_parts)]
                  + [pl.BlockSpec((ROW_TILE, OUT_TILE), lambda i, j: (i, j))]),
        out_specs=pl.BlockSpec((ROW_TILE, OUT_TILE), lambda i, j: (i, j)),
        out_shape=jax.ShapeDtypeStruct((m, n), F32),
        compiler_params=_params("parallel", "arbitrary"),
        name="out_proj",
    )(*parts, *([w] * n_parts), res)


S5_LANES = S5_GROUPS * S5_STATE
S5_BLOCKS = 4
S5_BLOCK_CH = S5_WIDTH // S5_BLOCKS
S5_BLOCK_ST = S5_LANES // S5_BLOCKS
SCAN_LANES = 256
S5_TILE = 256


def _s5_param_kernel(lr_ref, li_ref, ldt_ref, bre_ref, bim_ref, tab_ref, bbr_ref, bbi_ref):
    lr = lr_ref[...]
    li = li_ref[...]
    dt = jnp.exp(ldt_ref[...])
    dec = lr * dt
    ang = li * dt
    er = jnp.exp(dec)
    xr = er * jnp.cos(ang) - 1.0
    xi = er * jnp.sin(ang)
    den = lr * lr + li * li
    cr = (xr * lr + xi * li) / den
    ci = (xi * lr - xr * li) / den
    bre = bre_ref[...]
    bim = bim_ref[...]
    bbr_ref[...] = cr * bre - ci * bim
    bbi_ref[...] = cr * bim + ci * bre

    row = lax.broadcasted_iota(jnp.int32, (8, S5_LANES), 0)

    def power(k):
        mag = jnp.exp(k * dec)
        return mag * jnp.cos(k * ang), mag * jnp.sin(k * ang)

    for idx, s in enumerate((1, 2, 4)):
        pr, pi = power(float(s))
        tab_ref[2 * idx] = jnp.where(row >= s, pr, 0.0)
        tab_ref[2 * idx + 1] = jnp.where(row >= s, pi, 0.0)
    pr, pi = power((row + 1).astype(F32))
    tab_ref[6] = pr
    tab_ref[7] = pi


def s5_params(lam_re, lam_im, log_dt, b_re, b_im):
    flat = lambda t: t.reshape(1, S5_LANES)
    ldt = jnp.broadcast_to(log_dt[:, None], (S5_GROUPS, S5_STATE))
    to_rows = lambda t: t.reshape(S5_LANES, S5_GROUP).T
    return pl.pallas_call(
        _s5_param_kernel,
        out_shape=(jax.ShapeDtypeStruct((8, 8, S5_LANES), F32),
                   jax.ShapeDtypeStruct((S5_GROUP, S5_LANES), F32),
                   jax.ShapeDtypeStruct((S5_GROUP, S5_LANES), F32)),
        name="s5_params",
    )(flat(lam_re), flat(lam_im), flat(ldt), to_rows(b_re), to_rows(b_im))


def _block_diag_in(bbr, bbi):
    bb = jnp.stack([bbr, bbi]).reshape(2, S5_GROUP, S5_BLOCKS, 16, S5_STATE)
    eye = jnp.eye(16, dtype=F32)
    t = bb[:, :, :, :, None, :] * eye[None, None, None, :, :, None]
    return t.transpose(2, 3, 1, 0, 4, 5).reshape(S5_BLOCKS, S5_BLOCK_CH, 2 * S5_BLOCK_ST).astype(BF16)


def _block_diag_out(c):
    cc = c.reshape(S5_BLOCKS, 16, S5_GROUP, S5_STATE)
    eye = jnp.eye(16, dtype=F32)
    t = cc.transpose(0, 1, 3, 2)[:, :, :, None, :] * eye[None, :, None, :, None]
    return t.reshape(S5_BLOCKS, S5_BLOCK_ST, S5_BLOCK_CH).astype(BF16)


def _s5_drive(ub, bbd_ref, hr_ref, hi_ref):
    for blk in range(S5_BLOCKS):
        bu = jnp.dot(ub[:, blk * S5_BLOCK_CH:(blk + 1) * S5_BLOCK_CH], bbd_ref[blk], preferred_element_type=F32)
        hr_ref[:, blk * S5_BLOCK_ST:(blk + 1) * S5_BLOCK_ST] = bu[:, :S5_BLOCK_ST]
        hi_ref[:, blk * S5_BLOCK_ST:(blk + 1) * S5_BLOCK_ST] = bu[:, S5_BLOCK_ST:]


def _s5_readout(u, hr_ref, hi_ref, cr_ref, ci_ref, d_ref, wg_ref, bg_ref):
    ys = []
    for blk in range(S5_BLOCKS):
        sl = slice(blk * S5_BLOCK_ST, (blk + 1) * S5_BLOCK_ST)
        yr = jnp.dot(hr_ref[:, sl].astype(BF16), cr_ref[blk], preferred_element_type=F32)
        yi = jnp.dot(hi_ref[:, sl].astype(BF16), ci_ref[blk], preferred_element_type=F32)
        ys.append(yr - yi)
    y = jax.nn.gelu(jnp.concatenate(ys, axis=1) + d_ref[...] * u)
    gate = jax.nn.sigmoid(jnp.dot(y.astype(BF16), wg_ref[...], preferred_element_type=F32) + bg_ref[...])
    return (y * gate).astype(BF16)


def _s5_seq_kernel(u_ref, tab_ref, bbd_ref, cr_ref, ci_ref, d_ref, wg_ref, bg_ref,
                   o_ref, hr_out, hi_out, hr_ref, hi_ref, car_r, car_i):
    step = pl.program_id(1)

    @pl.when(step == 0)
    def _():
        car_r[...] = jnp.zeros_like(car_r)
        car_i[...] = jnp.zeros_like(car_i)

    u = u_ref[0]
    _s5_drive(u.astype(BF16), bbd_ref, hr_ref, hi_ref)

    n_tiles = u.shape[0] // 8
    for c in range(S5_LANES // SCAN_LANES):
        sl = slice(c * SCAN_LANES, (c + 1) * SCAN_LANES)
        tabs = [tab_ref[k, :, sl] for k in range(8)]

        def tile(i, carry, sl=sl, tabs=tabs):
            cr, ci = carry
            rows = pl.ds(pl.multiple_of(i * 8, 8), 8)
            xr = hr_ref[rows, sl]
            xi = hi_ref[rows, sl]
            for k, s in enumerate((1, 2, 4)):
                ar, ai = tabs[2 * k], tabs[2 * k + 1]
                rr = pltpu.roll(xr, s, 0)
                ri = pltpu.roll(xi, s, 0)
                xr, xi = xr + ar * rr - ai * ri, xi + ar * ri + ai * rr
            pr, pi = tabs[6], tabs[7]
            hr = xr + pr * cr - pi * ci
            hi = xi + pr * ci + pi * cr
            hr_ref[rows, sl] = hr
            hi_ref[rows, sl] = hi
            return (jnp.broadcast_to(hr[7:8], hr.shape), jnp.broadcast_to(hi[7:8], hi.shape))

        cr, ci = lax.fori_loop(0, n_tiles, tile, (car_r[:, sl], car_i[:, sl]), unroll=2)
        car_r[:, sl] = cr
        car_i[:, sl] = ci

    o_ref[...] = _s5_readout(u, hr_ref, hi_ref, cr_ref, ci_ref, d_ref, wg_ref, bg_ref)

    @pl.when(step == pl.num_programs(1) - 1)
    def _():
        hr_out[0] = car_r[0:1, :]
        hi_out[0] = car_i[0:1, :]


def _s5_weight_specs():
    return [_whole((8, 8, S5_LANES)), _whole((S5_BLOCKS, S5_BLOCK_CH, 2 * S5_BLOCK_ST)),
            _whole((S5_BLOCKS, S5_BLOCK_ST, S5_BLOCK_CH)), _whole((S5_BLOCKS, S5_BLOCK_ST, S5_BLOCK_CH)),
            _whole((1, S5_WIDTH)), _whole((S5_WIDTH, S5_WIDTH)), _whole((1, S5_WIDTH))]


def s5_sequences(z, weights, bsz, seq_len):
    steps = seq_len // S5_TILE
    state = jax.ShapeDtypeStruct((bsz, 1, S5_LANES), F32)
    state_spec = pl.BlockSpec((1, 1, S5_LANES), lambda b, s: (b, 0, 0))
    return pl.pallas_call(
        _s5_seq_kernel,
        grid=(bsz, steps),
        in_specs=[pl.BlockSpec((1, S5_TILE, S5_WIDTH), lambda b, s: (0, b * steps + s, 0))] + _s5_weight_specs(),
        out_specs=(pl.BlockSpec((S5_TILE, S5_WIDTH), lambda b, s: (b * steps + s, 0)), state_spec, state_spec),
        out_shape=(jax.ShapeDtypeStruct((bsz * seq_len, S5_WIDTH), BF16), state, state),
        scratch_shapes=[pltpu.VMEM((S5_TILE, S5_LANES), F32), pltpu.VMEM((S5_TILE, S5_LANES), F32),
                        pltpu.VMEM((8, S5_LANES), F32), pltpu.VMEM((8, S5_LANES), F32)],
        compiler_params=_params("parallel", "arbitrary"),
        name="s5_sequences",
    )(z, *weights)


def _s5_step_kernel(u_ref, h0r_ref, h0i_ref, tab_ref, bbd_ref, cr_ref, ci_ref, d_ref, wg_ref, bg_ref,
                    o_ref, hr_ref, hi_ref, bur_ref, bui_ref):
    @pl.when(pl.program_id(0) == 0)
    def _():
        hr_ref[...] = h0r_ref[...]
        hi_ref[...] = h0i_ref[...]

    u = u_ref[0]
    _s5_drive(u.astype(BF16), bbd_ref, bur_ref, bui_ref)
    ar = tab_ref[6, 0:1, :]
    ai = tab_ref[7, 0:1, :]
    hr = hr_ref[...]
    hi = hi_ref[...]
    hr_ref[...] = ar * hr - ai * hi + bur_ref[...]
    hi_ref[...] = ar * hi + ai * hr + bui_ref[...]
    o_ref[0] = _s5_readout(u, hr_ref, hi_ref, cr_ref, ci_ref, d_ref, wg_ref, bg_ref)


def s5_steps(u, h0_re, h0_im, weights):
    steps, rows, _ = u.shape
    state = jax.ShapeDtypeStruct((rows, S5_LANES), F32)
    return pl.pallas_call(
        _s5_step_kernel,
        grid=(steps,),
        in_specs=[pl.BlockSpec((1, rows, S5_WIDTH), lambda t: (t, 0, 0)),
                  _whole((rows, S5_LANES)), _whole((rows, S5_LANES))] + _s5_weight_specs(),
        out_specs=(pl.BlockSpec((1, rows, S5_WIDTH), lambda t: (t, 0, 0)),
                   _whole((rows, S5_LANES)), _whole((rows, S5_LANES))),
        out_shape=(jax.ShapeDtypeStruct((steps, rows, S5_WIDTH), BF16), state, state),
        scratch_shapes=[pltpu.VMEM((rows, S5_LANES), F32), pltpu.VMEM((rows, S5_LANES), F32)],
        compiler_params=_params("arbitrary"),
        name="s5_steps",
    )(u, h0_re, h0_im, *weights)


def s5_weights(lam_re, lam_im, log_dt, b_re, b_im, c_re, c_im, d_skip, w_glu, b_glu):
    tab, bbr, bbi = s5_params(lam_re, lam_im, log_dt, b_re, b_im)
    return (tab, _block_diag_in(bbr, bbi), _block_diag_out(c_re), _block_diag_out(c_im),
            d_skip.reshape(1, S5_WIDTH), w_glu.astype(BF16), b_glu.reshape(1, S5_WIDTH))


def _lane_pad_bias(b_f):
    return jnp.pad(b_f, (0, LANES - FOX_HEADS)).reshape(1, LANES)


def _log_forget(zf_ref, bf_ref):
    lane = lax.broadcasted_iota(jnp.int32, zf_ref.shape, 1)
    return jnp.where(lane < FOX_HEADS, jax.nn.log_sigmoid(zf_ref[...] + bf_ref[...]), 0.0)


def _gate_seq_kernel(zf_ref, bf_ref, lf_ref, c_ref, ct_ref):
    lf = _log_forget(zf_ref, bf_ref)
    lf_ref[...] = lf
    row = lax.broadcasted_iota(jnp.int32, lf.shape, 0)
    c = lf
    s = 1
    while s < lf.shape[0]:
        c = c + jnp.where(row >= s, pltpu.roll(c, s, 0), 0.0)
        s *= 2
    c_ref[0] = c
    ct_ref[0] = c.T[:FOX_HEADS]


def fox_gates_seq(zf, b_f, bsz, seq_len):
    return pl.pallas_call(
        _gate_seq_kernel,
        grid=(bsz,),
        in_specs=[pl.BlockSpec((seq_len, LANES), lambda b: (b, 0)), _whole((1, LANES))],
        out_specs=(pl.BlockSpec((seq_len, LANES), lambda b: (b, 0)),
                   pl.BlockSpec((1, seq_len, LANES), lambda b: (b, 0, 0)),
                   pl.BlockSpec((1, FOX_HEADS, seq_len), lambda b: (b, 0, 0))),
        out_shape=(jax.ShapeDtypeStruct((bsz * seq_len, LANES), F32),
                   jax.ShapeDtypeStruct((bsz, seq_len, LANES), F32),
                   jax.ShapeDtypeStruct((bsz, FOX_HEADS, seq_len), F32)),
        compiler_params=_params("parallel"),
        name="fox_gates_seq",
    )(zf, _lane_pad_bias(b_f))


def _gate_kernel(zf_ref, bf_ref, lf_ref):
    lf_ref[...] = _log_forget(zf_ref, bf_ref)


def fox_gates(zf, b_f):
    return pl.pallas_call(
        _gate_kernel,
        out_shape=jax.ShapeDtypeStruct(zf.shape, F32),
        name="fox_gates",
    )(zf, _lane_pad_bias(b_f))


FOX_TILE = 512
FOX_SCALE = FOX_HEAD_DIM ** -0.5


def _fox_seq_kernel(q_ref, k_ref, v_ref, c_ref, ct_ref, o_ref):
    h = pl.program_id(1)
    i = pl.program_id(2)
    q = q_ref[0].astype(BF16)
    lane = lax.broadcasted_iota(jnp.int32, c_ref.shape[1:], 1)
    cq = jnp.sum(jnp.where(lane == h, c_ref[0], 0.0), axis=1, keepdims=True)

    def chunk(j, carry, diagonal):
        m, l, acc = carry
        rows = pl.ds(pl.multiple_of(j * FOX_TILE, FOX_TILE), FOX_TILE)
        kc = k_ref[0, rows, :].astype(BF16)
        vc = v_ref[0, rows, :].astype(BF16)
        s = lax.dot_general(q, kc, NT_DIMS, preferred_element_type=F32) * FOX_SCALE
        s = s + cq - ct_ref[0, h, pl.ds(j, 1), :]
        if diagonal:
            qi = lax.broadcasted_iota(jnp.int32, s.shape, 0)
            ki = lax.broadcasted_iota(jnp.int32, s.shape, 1)
            s = jnp.where(qi >= ki, s, NEG_INF)
        m_new = jnp.maximum(m, jnp.max(s, axis=1, keepdims=True))
        alpha = jnp.exp(m - m_new)
        p = jnp.exp(s - m_new)
        l = alpha * l + jnp.sum(p, axis=1, keepdims=True)
        acc = alpha * acc + jnp.dot(p.astype(BF16), vc, preferred_element_type=F32)
        return m_new, l, acc

    init = (jnp.full((FOX_TILE, 1), NEG_INF, F32), jnp.zeros((FOX_TILE, 1), F32),
            jnp.zeros((FOX_TILE, FOX_HEAD_DIM), F32))
    carry = lax.fori_loop(0, i, functools.partial(chunk, diagonal=False), init)
    _, l, acc = chunk(i, carry, diagonal=True)
    o_ref[...] = (acc / l).astype(BF16)


def fox_sequences(z, c, ct, bsz, seq_len):
    nq = seq_len // FOX_TILE
    ct = ct.reshape(bsz, FOX_HEADS, nq, FOX_TILE)
    return pl.pallas_call(
        _fox_seq_kernel,
        grid=(bsz, FOX_HEADS, nq),
        in_specs=[
            pl.BlockSpec((1, FOX_TILE, FOX_HEAD_DIM), lambda b, h, i: (1, b * nq + i, h)),
            pl.BlockSpec((1, seq_len, FOX_HEAD_DIM), lambda b, h, i: (2, b, h)),
            pl.BlockSpec((1, seq_len, FOX_HEAD_DIM), lambda b, h, i: (3, b, h)),
            pl.BlockSpec((1, FOX_TILE, LANES), lambda b, h, i: (b, i, 0)),
            pl.BlockSpec((1, FOX_HEADS, nq, FOX_TILE), lambda b, h, i: (b, 0, 0, 0)),
        ],
        out_specs=pl.BlockSpec((FOX_TILE, FOX_HEAD_DIM), lambda b, h, i: (b * nq + i, h)),
        out_shape=jax.ShapeDtypeStruct((bsz * seq_len, FOX_WIDTH), BF16),
        compiler_params=_params("parallel", "parallel", "arbitrary"),
        name="fox_sequences",
    )(z, z, z, c, ct)


PAGE_ROWS = PAGE_SIZE * FOX_HEADS
NEW_ROWS = DEC_SEQ * FOX_HEADS


def _fox_paged_kernel(pt_ref, q_ref, kn_ref, vn_ref, ln_ref, *refs):
    k_pages = refs[:N_PAGES]
    v_pages = refs[N_PAGES:2 * N_PAGES]
    lf_pages = refs[2 * N_PAGES:3 * N_PAGES]
    o_ref, kpad_ref, vpad_ref = refs[3 * N_PAGES:]
    del pt_ref
    past = N_PAGES * PAGE_ROWS
    q = q_ref[0].astype(BF16)
    row = lax.broadcasted_iota(jnp.int32, (NEW_ROWS, 1), 0)
    row_head = jnp.bitwise_and(row, FOX_HEADS - 1)

    c = jnp.concatenate([r[0] for r in lf_pages], axis=1)
    lane = lax.broadcasted_iota(jnp.int32, c.shape, 1)
    s = FOX_HEADS
    while s < past:
        c = c + jnp.where(lane >= s, pltpu.roll(c, s, 1), 0.0)
        s *= 2
    tail_lane = lax.broadcasted_iota(jnp.int32, (NEW_ROWS, LANES), 1)
    tail = jnp.broadcast_to(c[:, past - LANES:], (NEW_ROWS, LANES))
    c_last = jnp.sum(jnp.where(tail_lane == LANES - FOX_HEADS + row_head, tail, 0.0), axis=1, keepdims=True)
    ln = ln_ref[0]
    blocks = [c_last[0:FOX_HEADS] + ln[0:FOX_HEADS]]
    for t in range(1, DEC_SEQ):
        blocks.append(blocks[-1] + ln[t * FOX_HEADS:(t + 1) * FOX_HEADS])
    c_new = jnp.concatenate(blocks, axis=0)
    c_new_row = jnp.sum(jnp.where(tail_lane == row, c_new, 0.0), axis=0, keepdims=True)

    kpad_ref[...] = jnp.zeros_like(kpad_ref)
    vpad_ref[...] = jnp.zeros_like(vpad_ref)
    kpad_ref[0:NEW_ROWS, :] = kn_ref[0]
    vpad_ref[0:NEW_ROWS, :] = vn_ref[0]

    s_past = jnp.concatenate(
        [lax.dot_general(q, kp[0].astype(BF16), NT_DIMS, preferred_element_type=F32) for kp in k_pages], axis=1)
    s_past = s_past * FOX_SCALE + c_new - c
    s_past = jnp.where(jnp.bitwise_and(lane, FOX_HEADS - 1) == row_head, s_past, NEG_INF)
    s_new = lax.dot_general(q, kpad_ref[...].astype(BF16), NT_DIMS, preferred_element_type=F32)
    s_new = s_new * FOX_SCALE + c_new - c_new_row
    visible = (jnp.bitwise_and(tail_lane, FOX_HEADS - 1) == row_head) & (tail_lane <= row)
    s_new = jnp.where(visible, s_new, NEG_INF)

    m = jnp.maximum(jnp.max(s_past, axis=1, keepdims=True), jnp.max(s_new, axis=1, keepdims=True))
    p_past = jnp.exp(s_past - m)
    p_new = jnp.exp(s_new - m)
    total = jnp.sum(p_past, axis=1, keepdims=True) + jnp.sum(p_new, axis=1, keepdims=True)
    p_past = p_past.astype(BF16)
    acc = jnp.dot(p_new.astype(BF16), vpad_ref[...].astype(BF16), preferred_element_type=F32)
    for j, vp in enumerate(v_pages):
        acc = acc + jnp.dot(p_past[:, j * PAGE_ROWS:(j + 1) * PAGE_ROWS], vp[0].astype(BF16),
                            preferred_element_type=F32)
    o_ref[0] = acc / total


def fox_paged(q, k_new, v_new, logf_new, cache_k, cache_v, cache_logf, page_table, first_page):
    bsz = q.shape[0]

    def page(j):
        return lambda b, pt: (first_page + pt[b, j], 0, 0)

    new_spec = pl.BlockSpec((1, NEW_ROWS, FOX_HEAD_DIM), lambda b, pt: (b, 0, 0))
    in_specs = [new_spec, new_spec, new_spec, pl.BlockSpec((1, NEW_ROWS, 1), lambda b, pt: (b, 0, 0))]
    in_specs += [pl.BlockSpec((1, PAGE_ROWS, FOX_HEAD_DIM), page(j)) for j in range(N_PAGES)]
    in_specs += [pl.BlockSpec((1, PAGE_ROWS, FOX_HEAD_DIM), page(j)) for j in range(N_PAGES)]
    in_specs += [pl.BlockSpec((1, 1, PAGE_ROWS), page(j)) for j in range(N_PAGES)]
    return pl.pallas_call(
        _fox_paged_kernel,
        grid_spec=pltpu.PrefetchScalarGridSpec(
            num_scalar_prefetch=1,
            grid=(bsz,),
            in_specs=in_specs,
            out_specs=new_spec,
            scratch_shapes=[pltpu.VMEM((LANES, FOX_HEAD_DIM), F32), pltpu.VMEM((LANES, FOX_HEAD_DIM), F32)],
        ),
        out_shape=jax.ShapeDtypeStruct((bsz, NEW_ROWS, FOX_HEAD_DIM), F32),
        compiler_params=_params("parallel"),
        name="fox_paged",
    )(page_table, q, k_new, v_new, logf_new,
      *([cache_k] * N_PAGES), *([cache_v] * N_PAGES), *([cache_logf] * N_PAGES))


ROPE_HALF = RET_KD // 2
RET_SCALE = RET_KD ** -0.5
RET_LOG_GAMMA = [float(np.log1p(-np.float32(2.0 ** (-5.0 - h)))) for h in range(RET_HEADS)]


def _rope_kernel(f_ref, cos_ref, sin_ref, *, start):
    pos = (start + lax.broadcasted_iota(jnp.int32, cos_ref.shape, 0)).astype(F32)
    ang = pos * f_ref[...]
    cos_ref[...] = jnp.cos(ang)
    sin_ref[...] = jnp.sin(ang)


def rope_table(start, n):
    inv_freq = ROPE_BASE ** (-jnp.arange(ROPE_HALF, dtype=F32) / ROPE_HALF)
    shape = jax.ShapeDtypeStruct((n, ROPE_HALF), F32)
    return pl.pallas_call(
        functools.partial(_rope_kernel, start=start), out_shape=(shape, shape), name="rope_table",
    )(inv_freq.reshape(1, ROPE_HALF))


def _rotate(x, cos, sin):
    x1, x2 = x[:, :ROPE_HALF], x[:, ROPE_HALF:]
    return jnp.concatenate([x1 * cos - x2 * sin, x1 * sin + x2 * cos], axis=1)


def _group_norm_gate(o, g, gain):
    mu = jnp.mean(o, axis=-1, keepdims=True)
    var = jnp.mean(jnp.square(o - mu), axis=-1, keepdims=True)
    return g * jax.nn.sigmoid(g) * ((o - mu) * lax.rsqrt(var + EPS) * gain)


def _head_slabs(refs, h, per_slab):
    width = PROJ_TILE // per_slab
    return refs[h // per_slab], slice((h % per_slab) * width, (h % per_slab + 1) * width)


def _ret_seq_kernel(*refs):
    q_refs, k_refs, v_refs, g_refs = refs[0:2], refs[2:4], refs[4:8], refs[8:12]
    cos_ref, sin_ref, gain_ref, o_ref, s_out, s_ref = refs[12:]
    step = pl.program_id(1)

    @pl.when(step == 0)
    def _():
        s_ref[...] = jnp.zeros_like(s_ref)

    cos = cos_ref[...]
    sin = sin_ref[...]
    cl = RET_CHUNK
    i_col = lax.broadcasted_iota(jnp.int32, (cl, 1), 0).astype(F32)
    diff = i_col - lax.broadcasted_iota(jnp.int32, (1, cl), 1).astype(F32)
    for h in range(RET_HEADS):
        lg = RET_LOG_GAMMA[h]
        ref, sl = _head_slabs(q_refs, h, 4)
        q = _rotate(ref[0, :, sl], cos, sin).astype(BF16)
        ref, sl = _head_slabs(k_refs, h, 4)
        k = _rotate(ref[0, :, sl], cos, sin) * RET_SCALE
        ref, sl = _head_slabs(v_refs, h, 2)
        v = ref[0, :, sl].astype(BF16)
        ref, sl = _head_slabs(g_refs, h, 2)
        g = ref[0, :, sl]
        dmask = jnp.where(diff >= 0, jnp.exp(jnp.maximum(diff, 0.0) * lg), 0.0)
        inner = lax.dot_general(q, k.astype(BF16), NT_DIMS, preferred_element_type=F32) * dmask
        state = s_ref[h]
        out = jnp.dot(inner.astype(BF16), v, preferred_element_type=F32)
        out = out + jnp.dot(q, state.astype(BF16), preferred_element_type=F32) * jnp.exp((i_col + 1.0) * lg)
        k_dec = k * jnp.exp((cl - 1.0 - i_col) * lg)
        s_ref[h] = math.exp(cl * lg) * state + jnp.dot(k_dec.T.astype(BF16), v, preferred_element_type=F32)
        hs = slice(h * RET_VD, (h + 1) * RET_VD)
        o_ref[:, hs] = _group_norm_gate(out, g, gain_ref[:, hs]).astype(BF16)

    @pl.when(step == pl.num_programs(1) - 1)
    def _():
        s_out[0] = s_ref[...]


def ret_sequences(z, cos, sin, gain, bsz, seq_len):
    steps = seq_len // RET_CHUNK
    vd = RET_HEADS * RET_VD
    slab = lambda j: pl.BlockSpec((1, RET_CHUNK, PROJ_TILE), lambda b, c: (j, b * steps + c, 0))
    table = pl.BlockSpec((RET_CHUNK, ROPE_HALF), lambda b, c: (c, 0))
    return pl.pallas_call(
        _ret_seq_kernel,
        grid=(bsz, steps),
        in_specs=[slab(j) for j in range(12)] + [table, table, _whole((1, vd))],
        out_specs=(pl.BlockSpec((RET_CHUNK, vd), lambda b, c: (b * steps + c, 0)),
                   pl.BlockSpec((1, RET_HEADS, RET_KD, RET_VD), lambda b, c: (b, 0, 0, 0))),
        out_shape=(jax.ShapeDtypeStruct((bsz * seq_len, vd), BF16),
                   jax.ShapeDtypeStruct((bsz, RET_HEADS, RET_KD, RET_VD), F32)),
        scratch_shapes=[pltpu.VMEM((RET_HEADS, RET_KD, RET_VD), F32)],
        compiler_params=_params("parallel", "arbitrary"),
        name="ret_sequences",
    )(*([z] * 12), cos, sin, gain.reshape(1, vd))


RET_PAD = 128


def _ret_step_kernel(z_ref, s0_ref, cos_ref, sin_ref, gain_ref, o_ref, s_out, qpad, kpad, vpad, gpad):
    cl = DEC_SEQ
    for ref in (qpad, kpad, vpad, gpad):
        ref[...] = jnp.zeros_like(ref)
    cos = cos_ref[...]
    sin = sin_ref[...]
    i_col = lax.broadcasted_iota(jnp.int32, (RET_PAD, 1), 0).astype(F32)
    diff = i_col - lax.broadcasted_iota(jnp.int32, (1, RET_PAD), 1).astype(F32)
    valid = (diff >= 0) & (i_col < cl)
    outs = []
    for h in range(RET_HEADS):
        lg = RET_LOG_GAMMA[h]
        qpad[0:cl, :] = z_ref[h // 4, 0, :, (h % 4) * RET_KD:(h % 4 + 1) * RET_KD]
        kpad[0:cl, :] = z_ref[2 + h // 4, 0, :, (h % 4) * RET_KD:(h % 4 + 1) * RET_KD]
        vpad[0:cl, :] = z_ref[4 + h // 2, 0, :, (h % 2) * RET_VD:(h % 2 + 1) * RET_VD]
        gpad[0:cl, :] = z_ref[8 + h // 2, 0, :, (h % 2) * RET_VD:(h % 2 + 1) * RET_VD]
        q = _rotate(qpad[...], cos, sin).astype(BF16)
        k = _rotate(kpad[...], cos, sin) * RET_SCALE
        v = vpad[...].astype(BF16)
        dmask = jnp.where(valid, jnp.exp(jnp.maximum(diff, 0.0) * lg), 0.0)
        inner = lax.dot_general(q, k.astype(BF16), NT_DIMS, preferred_element_type=F32) * dmask
        state = s0_ref[0, h]
        out = jnp.dot(inner.astype(BF16), v, preferred_element_type=F32)
        out = out + jnp.dot(q, state.astype(BF16), preferred_element_type=F32) * jnp.exp((i_col + 1.0) * lg)
        k_dec = k * jnp.exp((cl - 1.0 - i_col) * lg)
        s_out[0, h] = math.exp(cl * lg) * state + jnp.dot(k_dec.T.astype(BF16), v, preferred_element_type=F32)
        hs = slice(h * RET_VD, (h + 1) * RET_VD)
        outs.append(_group_norm_gate(out[0:8], gpad[0:8, :], gain_ref[:, hs]))
    o_ref[0] = jnp.concatenate(outs, axis=1)[0:cl]


def ret_steps(z, s0, cos, sin, gain):
    bsz = z.shape[1]
    vd = RET_HEADS * RET_VD
    state_spec = pl.BlockSpec((1, RET_HEADS, RET_KD, RET_VD), lambda b: (b, 0, 0, 0))
    return pl.pallas_call(
        _ret_step_kernel,
        grid=(bsz,),
        in_specs=[pl.BlockSpec((12, 1, DEC_SEQ, PROJ_TILE), lambda b: (0, b, 0, 0)), state_spec,
                  _whole((RET_PAD, ROPE_HALF)), _whole((RET_PAD, ROPE_HALF)), _whole((1, vd))],
        out_specs=(pl.BlockSpec((1, DEC_SEQ, vd), lambda b: (b, 0, 0)), state_spec),
        out_shape=(jax.ShapeDtypeStruct((bsz, DEC_SEQ, vd), F32),
                   jax.ShapeDtypeStruct(s0.shape, F32)),
        scratch_shapes=[pltpu.VMEM((RET_PAD, RET_KD), F32), pltpu.VMEM((RET_PAD, RET_KD), F32),
                        pltpu.VMEM((RET_PAD, RET_VD), F32), pltpu.VMEM((RET_PAD, RET_VD), F32)],
        compiler_params=_params("parallel"),
        name="ret_steps",
    )(z, s0, cos, sin, gain.reshape(1, vd))


def _sample_rows(z):
    return z[:, N_PROMPT:].reshape(z.shape[0], DEC_BATCH, DEC_SEQ, z.shape[2])


def _even_mixer(h, gain, e, cache_k, cache_v, cache_logf, state_s5_re, state_s5_im, page_table, w_in, b_f,
                s5_w):
    z, zf = norm_proj(h, gain, w_in, 4 * PROJ_TILE, narrow=True)
    zs = _sample_rows(z)

    s5_p, hr_p, hi_p = s5_sequences(z, s5_w, BATCH, SEQ)
    s5_s, hr_s, hi_s = s5_steps(zs[0].transpose(1, 0, 2), state_s5_re.reshape(DEC_BATCH, S5_LANES),
                                state_s5_im.reshape(DEC_BATCH, S5_LANES), s5_w)
    s5_all = jnp.concatenate([s5_p, s5_s.transpose(1, 0, 2).reshape(N_SAMPLE, S5_WIDTH)], axis=0)

    lf_p, c_p, ct_p = fox_gates_seq(zf, b_f, BATCH, SEQ)
    lf_s = fox_gates(zf[N_PROMPT:], b_f)[:, :FOX_HEADS]
    k_p, v_p, k_s, v_s = head_rows(z)
    att_p = fox_sequences(z, c_p, ct_p, BATCH, SEQ)
    pool = cache_k.shape[1]
    new_rows = lambda t: t.reshape(DEC_BATCH, NEW_ROWS, FOX_HEAD_DIM)
    att_s = fox_paged(new_rows(zs[1]), new_rows(k_s), new_rows(v_s), lf_s.reshape(DEC_BATCH, NEW_ROWS, 1),
                      cache_k.reshape(-1, PAGE_ROWS, FOX_HEAD_DIM), cache_v.reshape(-1, PAGE_ROWS, FOX_HEAD_DIM),
                      cache_logf.reshape(-1, 1, PAGE_ROWS), page_table, e * pool)
    att_all = jnp.concatenate([att_p, att_s.reshape(N_SAMPLE, FOX_WIDTH).astype(BF16)], axis=0)

    heads = lambda t, b, s: t.reshape(b, s, FOX_HEADS, FOX_HEAD_DIM)
    grid = lambda t: t.reshape(-1, S5_GROUPS, S5_STATE)
    st_p = (heads(k_p, BATCH, SEQ), heads(v_p, BATCH, SEQ),
            lf_p[:, :FOX_HEADS].reshape(BATCH, SEQ, FOX_HEADS), grid(hr_p), grid(hi_p))
    st_s = (heads(k_s, DEC_BATCH, DEC_SEQ), heads(v_s, DEC_BATCH, DEC_SEQ),
            lf_s.reshape(DEC_BATCH, DEC_SEQ, FOX_HEADS), grid(hr_s), grid(hi_s))
    return [s5_all, att_all], st_p, st_s


def _odd_mixer(h, gain, w_in, gn_gain, state, tables):
    z = norm_proj(h, gain, w_in, 12 * PROJ_TILE)
    (cos_p, sin_p), (cos_s, sin_s) = tables
    o_p, st_p = ret_sequences(z, cos_p, sin_p, gn_gain, BATCH, SEQ)
    o_s, st_s = ret_steps(_sample_rows(z), state, cos_s, sin_s, gn_gain)
    o_all = jnp.concatenate([o_p, o_s.reshape(N_SAMPLE, RET_HEADS * RET_VD).astype(BF16)], axis=0)
    return [o_all], st_p, st_s


def kernel(x_prompt, x_sample, cache_k, cache_v, cache_logf, state_s5_re, state_s5_im, state_ret, page_table, norm_gain, w_ffn_gate, w_ffn_up, w_ffn_down, w_in_even, b_forget, s5_lam_re, s5_lam_im, s5_b_re, s5_b_im, s5_c_re, s5_c_im, s5_d, s5_log_dt, w_glu, b_glu, w_out_even, w_in_odd, ret_gn_gain, w_out_odd, final_norm):
    x = jnp.concatenate([x_prompt.reshape(N_PROMPT, D_MODEL), x_sample.reshape(N_SAMPLE, D_MODEL)], axis=0)
    tables = (rope_table(0, SEQ), rope_table(PAST_LEN, RET_PAD))
    ev_p, ev_s, od_p, od_s = [], [], [], []
    for layer in range(DEPTH):
        h = ffn(x, norm_gain[layer, 0], w_ffn_gate[layer, 0], w_ffn_up[layer, 0], w_ffn_down[layer, 0])
        if layer % 2 == 0:
            e = layer // 2
            s5_w = s5_weights(s5_lam_re[e], s5_lam_im[e], s5_log_dt[e], s5_b_re[e], s5_b_im[e], s5_c_re[e],
                              s5_c_im[e], s5_d[e], w_glu[e], b_glu[e])
            parts, st_p, st_s = _even_mixer(h, norm_gain[layer, 1], e, cache_k, cache_v, cache_logf,
                                            state_s5_re[e], state_s5_im[e], page_table, w_in_even[e],
                                            b_forget[e], s5_w)
            ev_p.append(st_p)
            ev_s.append(st_s)
            w_out = w_out_even[e]
        else:
            o = layer // 2
            parts, st_p, st_s = _odd_mixer(h, norm_gain[layer, 1], w_in_odd[o], ret_gn_gain[o], state_ret[o],
                                           tables)
            od_p.append(st_p)
            od_s.append(st_s)
            w_out = w_out_odd[o]
        h = out_proj(parts, w_out, h)
        x = ffn(h, norm_gain[layer, 2], w_ffn_gate[layer, 1], w_ffn_up[layer, 1], w_ffn_down[layer, 1])

    y_prompt, y_sample = final_rms(x, final_norm)
    y_prompt = y_prompt.reshape(BATCH, SEQ, D_MODEL)
    y_sample = y_sample.reshape(DEC_BATCH, DEC_SEQ, D_MODEL)
    stack = lambda sts, i: jnp.stack([s[i] for s in sts])
    return (y_prompt, y_sample,
            stack(ev_p, 0), stack(ev_p, 1), stack(ev_p, 2), stack(ev_p, 3), stack(ev_p, 4), jnp.stack(od_p),
            stack(ev_s, 0), stack(ev_s, 1), stack(ev_s, 2), stack(ev_s, 3), stack(ev_s, 4), jnp.stack(od_s))
```

```python
import functools
import math

import jax
import jax.numpy as jnp
import numpy as np
from jax import lax
from jax.experimental import pallas as pl
from jax.experimental.pallas import tpu as pltpu

D_MODEL = 2048
BATCH = 2
SEQ = 4096
DEPTH = 2
DEC_BATCH = 128
DEC_SEQ = 4
PAST_LEN = 2048
PAGE_SIZE = 128
N_PAGES = PAST_LEN // PAGE_SIZE
S5_WIDTH = 1024
S5_GROUP = 16
S5_GROUPS = 64
S5_STATE = 64
FOX_WIDTH = 1024
FOX_HEAD_DIM = 128
FOX_HEADS = 8
NEG_INF = -1e30
RET_HEADS = 8
RET_KD = 256
RET_VD = 512
RET_CHUNK = 128
ROPE_BASE = 10000.0
D_FF = 5504
EPS = 1e-6

N_PROMPT = BATCH * SEQ
N_SAMPLE = DEC_BATCH * DEC_SEQ
N_TOK = N_PROMPT + N_SAMPLE

ROW_TILE = 1088
FF_TILE = 256
LANES = 128
VMEM_LIMIT = 56 * 1024 * 1024
FFN_VMEM_LIMIT = 60 * 1024 * 1024

F32 = jnp.float32
BF16 = jnp.bfloat16
NT_DIMS = (((1,), (1,)), ((), ()))


def _rms_rows(x, gain):
    return x * lax.rsqrt(jnp.mean(x * x, axis=-1, keepdims=True) + EPS) * gain


def _whole(shape):
    return pl.BlockSpec(shape, lambda *_: (0,) * len(shape))


def _params(*semantics):
    return pltpu.CompilerParams(dimension_semantics=semantics, vmem_limit_bytes=VMEM_LIMIT)


FFN_GROUP = 2


def _ffn_kernel(x_ref, g_ref, wg_ref, wu_ref, wd_ref, o_ref, h_ref, *, d_ff):
    step = pl.program_id(1)
    tf = wg_ref.shape[1]

    @pl.when(step < FFN_GROUP)
    def _():
        rows = pl.ds(pl.multiple_of(step * ROW_TILE, ROW_TILE), ROW_TILE)
        x = x_ref[...]
        h_ref[rows, :] = _rms_rows(x, g_ref[...]).astype(BF16)
        o_ref[rows, :] = x

    @pl.when(step >= FFN_GROUP)
    def _():
        first = (step - FFN_GROUP) * tf
        wg = wg_ref[...].astype(BF16)
        wu = wu_ref[...].astype(BF16)
        col_ok = first + lax.broadcasted_iota(jnp.int32, (1, tf), 1) < d_ff
        row_ok = first + lax.broadcasted_iota(jnp.int32, (tf, 1), 0) < d_ff
        wd = jnp.where(row_ok, wd_ref[...], 0.0).astype(BF16)
        for t in range(FFN_GROUP):
            rows = slice(t * ROW_TILE, (t + 1) * ROW_TILE)
            h = h_ref[rows, :]
            a = jnp.dot(h, wg, preferred_element_type=F32)
            b = jnp.dot(h, wu, preferred_element_type=F32)
            c = jnp.where(col_ok, a * jax.nn.sigmoid(a) * b, 0.0).astype(BF16)
            o_ref[rows, :] += 0.5 * jnp.dot(c, wd, preferred_element_type=F32)


def ffn(x, gain, wg, wu, wd):
    m, d = x.shape
    d_ff = wg.shape[1]
    group_rows = FFN_GROUP * ROW_TILE
    w_tile = lambda r, s: jnp.maximum(s - FFN_GROUP, 0)
    return pl.pallas_call(
        functools.partial(_ffn_kernel, d_ff=d_ff),
        grid=(m // group_rows, FFN_GROUP + pl.cdiv(d_ff, FF_TILE)),
        in_specs=[
            pl.BlockSpec((ROW_TILE, d), lambda r, s: (FFN_GROUP * r + jnp.minimum(s, FFN_GROUP - 1), 0),
                         pipeline_mode=pl.Buffered(1)),
            pl.BlockSpec((1, d), lambda r, s: (0, 0)),
            pl.BlockSpec((d, FF_TILE), lambda r, s: (0, w_tile(r, s))),
            pl.BlockSpec((d, FF_TILE), lambda r, s: (0, w_tile(r, s))),
            pl.BlockSpec((FF_TILE, d), lambda r, s: (w_tile(r, s), 0)),
        ],
        out_specs=pl.BlockSpec((group_rows, d), lambda r, s: (r, 0), pipeline_mode=pl.Buffered(1)),
        out_shape=jax.ShapeDtypeStruct((m, d), F32),
        scratch_shapes=[pltpu.VMEM((group_rows, d), BF16)],
        compiler_params=pltpu.CompilerParams(
            dimension_semantics=("parallel", "arbitrary"), vmem_limit_bytes=FFN_VMEM_LIMIT),
        name="ffn",
    )(x, gain.reshape(1, d), wg, wu, wd)


SPLIT_TILE = 512
N_PROMPT_TILES = N_PROMPT // SPLIT_TILE


def _split_specs(block):
    zeros = (0,) * (len(block) - 1)
    return (pl.BlockSpec(block, lambda i: (jnp.minimum(i, N_PROMPT_TILES - 1),) + zeros),
            pl.BlockSpec(block, lambda i: (0,) + zeros))


def _final_norm_kernel(x_ref, g_ref, yp_ref, ys_ref):
    y = _rms_rows(x_ref[...], g_ref[...])

    @pl.when(pl.program_id(0) < N_PROMPT_TILES)
    def _():
        yp_ref[...] = y

    @pl.when(pl.program_id(0) == N_PROMPT_TILES)
    def _():
        ys_ref[...] = y


def final_rms(x, gain):
    d = x.shape[1]
    return pl.pallas_call(
        _final_norm_kernel,
        grid=(N_PROMPT_TILES + 1,),
        in_specs=[pl.BlockSpec((SPLIT_TILE, d), lambda i: (i, 0)), _whole((1, d))],
        out_specs=_split_specs((SPLIT_TILE, d)),
        out_shape=(jax.ShapeDtypeStruct((N_PROMPT, d), F32), jax.ShapeDtypeStruct((N_SAMPLE, d), F32)),
        compiler_params=_params("arbitrary"),
        name="final_norm",
    )(x, gain.reshape(1, d))


def _head_rows_kernel(q_ref, k_ref, v_ref, kp_ref, vp_ref, qs_ref, ks_ref, vs_ref):
    def put(pairs):
        for src, dst in pairs:
            for h in range(FOX_HEADS):
                dst[:, h, :] = src[0, :, h * FOX_HEAD_DIM:(h + 1) * FOX_HEAD_DIM]

    @pl.when(pl.program_id(0) < N_PROMPT_TILES)
    def _():
        put([(k_ref, kp_ref), (v_ref, vp_ref)])

    @pl.when(pl.program_id(0) == N_PROMPT_TILES)
    def _():
        put([(q_ref, qs_ref), (k_ref, ks_ref), (v_ref, vs_ref)])


def head_rows(z):
    block = (SPLIT_TILE, FOX_HEADS, FOX_HEAD_DIM)
    shape = lambda n: jax.ShapeDtypeStruct((n, FOX_HEADS, FOX_HEAD_DIM), F32)
    kp_spec, ks_spec = _split_specs(block)
    return pl.pallas_call(
        _head_rows_kernel,
        grid=(N_PROMPT_TILES + 1,),
        in_specs=[pl.BlockSpec((1, SPLIT_TILE, FOX_WIDTH), lambda i: (1, N_PROMPT_TILES, 0)),
                  pl.BlockSpec((1, SPLIT_TILE, FOX_WIDTH), lambda i: (2, i, 0)),
                  pl.BlockSpec((1, SPLIT_TILE, FOX_WIDTH), lambda i: (3, i, 0))],
        out_specs=(kp_spec, kp_spec, ks_spec, ks_spec, ks_spec),
        out_shape=(shape(N_PROMPT), shape(N_PROMPT), shape(N_SAMPLE), shape(N_SAMPLE), shape(N_SAMPLE)),
        compiler_params=_params("arbitrary"),
        name="head_rows",
    )(z, z, z)


PROJ_TILE = 1024


def _norm_proj_kernel(x_ref, g_ref, w_ref, *rest, narrow):
    if narrow:
        wn_ref, z_ref, zn_ref, h_ref = rest
    else:
        z_ref, h_ref = rest
    j = pl.program_id(1)

    @pl.when(j == 0)
    def _():
        h_ref[...] = _rms_rows(x_ref[...], g_ref[...]).astype(BF16)
        if narrow:
            zn_ref[...] = jnp.dot(h_ref[...], wn_ref[...].astype(BF16), preferred_element_type=F32)

    z_ref[0] = jnp.dot(h_ref[...], w_ref[...].astype(BF16), preferred_element_type=F32)


def norm_proj(x, gain, w, n_wide, narrow=False):
    m, d = x.shape
    nj = n_wide // PROJ_TILE
    in_specs = [
        pl.BlockSpec((ROW_TILE, d), lambda i, j: (i, 0), pipeline_mode=pl.Buffered(1)),
        pl.BlockSpec((1, d), lambda i, j: (0, 0)),
        pl.BlockSpec((d, PROJ_TILE), lambda i, j: (0, j)),
    ]
    args = [x, gain.reshape(1, d), w]
    out_shape = jax.ShapeDtypeStruct((nj, m, PROJ_TILE), F32)
    out_specs = pl.BlockSpec((1, ROW_TILE, PROJ_TILE), lambda i, j: (j, i, 0))
    if narrow:
        in_specs.append(pl.BlockSpec((d, LANES), lambda i, j: (0, n_wide // LANES)))
        args.append(w)
        out_shape = (out_shape, jax.ShapeDtypeStruct((m, LANES), F32))
        out_specs = (out_specs, pl.BlockSpec((ROW_TILE, LANES), lambda i, j: (i, 0)))
    return pl.pallas_call(
        functools.partial(_norm_proj_kernel, narrow=narrow),
        grid=(m // ROW_TILE, nj),
        in_specs=in_specs,
        out_specs=out_specs,
        out_shape=out_shape,
        scratch_shapes=[pltpu.VMEM((ROW_TILE, d), BF16)],
        compiler_params=_params("parallel", "arbitrary"),
        name="norm_proj_narrow" if narrow else "norm_proj",
    )(*args)


OUT_TILE = 512


def _out_proj_kernel(*refs, n_parts):
    a_refs, w_refs, (r_ref, o_ref) = refs[:n_parts], refs[n_parts:2 * n_parts], refs[2 * n_parts:]
    acc = r_ref[...]
    for a_ref, w_ref in zip(a_refs, w_refs):
        acc = acc + jnp.dot(a_ref[...], w_ref[...].astype(BF16), preferred_element_type=F32)
    o_ref[...] = acc


def out_proj(parts, w, res):
    m, k = parts[0].shape
    n = w.shape[1]
    n_parts = len(parts)
    return pl.pallas_call(
        functools.partial(_out_proj_kernel, n_parts=n_parts),
        grid=(m // ROW_TILE, n // OUT_TILE),
        in_specs=([pl.BlockSpec((ROW_TILE, k), lambda i, j: (i, 0))] * n_parts
                  + [pl.BlockSpec((k, OUT_TILE), functools.partial(lambda i, j, p: (p, j), p=p))
                     for p in range(n_parts)]
                  + [pl.BlockSpec((ROW_TILE, OUT_TILE), lambda i, j: (i, j))]),
        out_specs=pl.BlockSpec((ROW_TILE, OUT_TILE), lambda i, j: (i, j)),
        out_shape=jax.ShapeDtypeStruct((m, n), F32),
        compiler_params=_params("parallel", "arbitrary"),
        name="out_proj",
    )(*parts, *([w] * n_parts), res)


S5_LANES = S5_GROUPS * S5_STATE
S5_BLOCKS = 4
S5_BLOCK_CH = S5_WIDTH // S5_BLOCKS
S5_BLOCK_ST = S5_LANES // S5_BLOCKS
SCAN_LANES = 256
S5_TILE = 256


def _s5_param_kernel(lr_ref, li_ref, ldt_ref, bre_ref, bim_ref, tab_ref, bbr_ref, bbi_ref):
    lr = lr_ref[...]
    li = li_ref[...]
    dt = jnp.exp(ldt_ref[...])
    dec = lr * dt
    ang = li * dt
    er = jnp.exp(dec)
    xr = er * jnp.cos(ang) - 1.0
    xi = er * jnp.sin(ang)
    den = lr * lr + li * li
    cr = (xr * lr + xi * li) / den
    ci = (xi * lr - xr * li) / den
    bre = bre_ref[...]
    bim = bim_ref[...]
    bbr_ref[...] = cr * bre - ci * bim
    bbi_ref[...] = cr * bim + ci * bre

    row = lax.broadcasted_iota(jnp.int32, (8, S5_LANES), 0)

    def power(k):
        mag = jnp.exp(k * dec)
        return mag * jnp.cos(k * ang), mag * jnp.sin(k * ang)

    for idx, s in enumerate((1, 2, 4)):
        pr, pi = power(float(s))
        tab_ref[2 * idx] = jnp.where(row >= s, pr, 0.0)
        tab_ref[2 * idx + 1] = jnp.where(row >= s, pi, 0.0)
    pr, pi = power((row + 1).astype(F32))
    tab_ref[6] = pr
    tab_ref[7] = pi


def s5_params(lam_re, lam_im, log_dt, b_re, b_im):
    flat = lambda t: t.reshape(1, S5_LANES)
    ldt = jnp.broadcast_to(log_dt[:, None], (S5_GROUPS, S5_STATE))
    to_rows = lambda t: t.reshape(S5_LANES, S5_GROUP).T
    return pl.pallas_call(
        _s5_param_kernel,
        out_shape=(jax.ShapeDtypeStruct((8, 8, S5_LANES), F32),
                   jax.ShapeDtypeStruct((S5_GROUP, S5_LANES), F32),
                   jax.ShapeDtypeStruct((S5_GROUP, S5_LANES), F32)),
        name="s5_params",
    )(flat(lam_re), flat(lam_im), flat(ldt), to_rows(b_re), to_rows(b_im))


def _block_diag_in(bbr, bbi):
    bb = jnp.stack([bbr, bbi]).reshape(2, S5_GROUP, S5_BLOCKS, 16, S5_STATE)
    eye = jnp.eye(16, dtype=F32)
    t = bb[:, :, :, :, None, :] * eye[None, None, None, :, :, None]
    return t.transpose(2, 3, 1, 0, 4, 5).reshape(S5_BLOCKS, S5_BLOCK_CH, 2 * S5_BLOCK_ST).astype(BF16)


def _block_diag_out(c):
    cc = c.reshape(S5_BLOCKS, 16, S5_GROUP, S5_STATE)
    eye = jnp.eye(16, dtype=F32)
    t = cc.transpose(0, 1, 3, 2)[:, :, :, None, :] * eye[None, :, None, :, None]
    return t.reshape(S5_BLOCKS, S5_BLOCK_ST, S5_BLOCK_CH).astype(BF16)


def _s5_drive(ub, bbd_ref, hr_ref, hi_ref):
    for blk in range(S5_BLOCKS):
        bu = jnp.dot(ub[:, blk * S5_BLOCK_CH:(blk + 1) * S5_BLOCK_CH], bbd_ref[blk], preferred_element_type=F32)
        hr_ref[:, blk * S5_BLOCK_ST:(blk + 1) * S5_BLOCK_ST] = bu[:, :S5_BLOCK_ST]
        hi_ref[:, blk * S5_BLOCK_ST:(blk + 1) * S5_BLOCK_ST] = bu[:, S5_BLOCK_ST:]


def _s5_readout(u, hr_ref, hi_ref, cr_ref, ci_ref, d_ref, wg_ref, bg_ref):
    ys = []
    for blk in range(S5_BLOCKS):
        sl = slice(blk * S5_BLOCK_ST, (blk + 1) * S5_BLOCK_ST)
        yr = jnp.dot(hr_ref[:, sl].astype(BF16), cr_ref[blk], preferred_element_type=F32)
        yi = jnp.dot(hi_ref[:, sl].astype(BF16), ci_ref[blk], preferred_element_type=F32)
        ys.append(yr - yi)
    y = jax.nn.gelu(jnp.concatenate(ys, axis=1) + d_ref[...] * u)
    gate = jax.nn.sigmoid(jnp.dot(y.astype(BF16), wg_ref[...], preferred_element_type=F32) + bg_ref[...])
    return (y * gate).astype(BF16)


def _s5_seq_kernel(u_ref, tab_ref, bbd_ref, cr_ref, ci_ref, d_ref, wg_ref, bg_ref,
                   o_ref, hr_out, hi_out, hr_ref, hi_ref, car_r, car_i):
    step = pl.program_id(1)

    @pl.when(step == 0)
    def _():
        car_r[...] = jnp.zeros_like(car_r)
        car_i[...] = jnp.zeros_like(car_i)

    u = u_ref[0]
    _s5_drive(u.astype(BF16), bbd_ref, hr_ref, hi_ref)

    n_tiles = u.shape[0] // 8
    for c in range(S5_LANES // SCAN_LANES):
        sl = slice(c * SCAN_LANES, (c + 1) * SCAN_LANES)
        tabs = [tab_ref[k, :, sl] for k in range(8)]

        def tile(i, carry, sl=sl, tabs=tabs):
            cr, ci = carry
            rows = pl.ds(pl.multiple_of(i * 8, 8), 8)
            xr = hr_ref[rows, sl]
            xi = hi_ref[rows, sl]
            for k, s in enumerate((1, 2, 4)):
                ar, ai = tabs[2 * k], tabs[2 * k + 1]
                rr = pltpu.roll(xr, s, 0)
                ri = pltpu.roll(xi, s, 0)
                xr, xi = xr + ar * rr - ai * ri, xi + ar * ri + ai * rr
            pr, pi = tabs[6], tabs[7]
            hr = xr + pr * cr - pi * ci
            hi = xi + pr * ci + pi * cr
            hr_ref[rows, sl] = hr
            hi_ref[rows, sl] = hi
            return (jnp.broadcast_to(hr[7:8], hr.shape), jnp.broadcast_to(hi[7:8], hi.shape))

        cr, ci = lax.fori_loop(0, n_tiles, tile, (car_r[:, sl], car_i[:, sl]), unroll=2)
        car_r[:, sl] = cr
        car_i[:, sl] = ci

    o_ref[...] = _s5_readout(u, hr_ref, hi_ref, cr_ref, ci_ref, d_ref, wg_ref, bg_ref)

    @pl.when(step == pl.num_programs(1) - 1)
    def _():
        hr_out[0] = car_r[0:1, :]
        hi_out[0] = car_i[0:1, :]


def _s5_weight_specs():
    return [_whole((8, 8, S5_LANES)), _whole((S5_BLOCKS, S5_BLOCK_CH, 2 * S5_BLOCK_ST)),
            _whole((S5_BLOCKS, S5_BLOCK_ST, S5_BLOCK_CH)), _whole((S5_BLOCKS, S5_BLOCK_ST, S5_BLOCK_CH)),
            _whole((1, S5_WIDTH)), _whole((S5_WIDTH, S5_WIDTH)), _whole((1, S5_WIDTH))]


def s5_sequences(z, weights, bsz, seq_len):
    steps = seq_len // S5_TILE
    state = jax.ShapeDtypeStruct((bsz, 1, S5_LANES), F32)
    state_spec = pl.BlockSpec((1, 1, S5_LANES), lambda b, s: (b, 0, 0))
    return pl.pallas_call(
        _s5_seq_kernel,
        grid=(bsz, steps),
        in_specs=[pl.BlockSpec((1, S5_TILE, S5_WIDTH), lambda b, s: (0, b * steps + s, 0))] + _s5_weight_specs(),
        out_specs=(pl.BlockSpec((S5_TILE, S5_WIDTH), lambda b, s: (b * steps + s, 0)), state_spec, state_spec),
        out_shape=(jax.ShapeDtypeStruct((bsz * seq_len, S5_WIDTH), BF16), state, state),
        scratch_shapes=[pltpu.VMEM((S5_TILE, S5_LANES), F32), pltpu.VMEM((S5_TILE, S5_LANES), F32),
                        pltpu.VMEM((8, S5_LANES), F32), pltpu.VMEM((8, S5_LANES), F32)],
        compiler_params=_params("parallel", "arbitrary"),
        name="s5_sequences",
    )(z, *weights)


def _s5_step_kernel(u_ref, h0r_ref, h0i_ref, tab_ref, bbd_ref, cr_ref, ci_ref, d_ref, wg_ref, bg_ref,
                    o_ref, hr_ref, hi_ref, bur_ref, bui_ref):
    @pl.when(pl.program_id(0) == 0)
    def _():
        hr_ref[...] = h0r_ref[...]
        hi_ref[...] = h0i_ref[...]

    u = u_ref[0]
    _s5_drive(u.astype(BF16), bbd_ref, bur_ref, bui_ref)
    ar = tab_ref[6, 0:1, :]
    ai = tab_ref[7, 0:1, :]
    hr = hr_ref[...]
    hi = hi_ref[...]
    hr_ref[...] = ar * hr - ai * hi + bur_ref[...]
    hi_ref[...] = ar * hi + ai * hr + bui_ref[...]
    o_ref[0] = _s5_readout(u, hr_ref, hi_ref, cr_ref, ci_ref, d_ref, wg_ref, bg_ref)


def s5_steps(u, h0_re, h0_im, weights):
    steps, rows, _ = u.shape
    state = jax.ShapeDtypeStruct((rows, S5_LANES), F32)
    return pl.pallas_call(
        _s5_step_kernel,
        grid=(steps,),
        in_specs=[pl.BlockSpec((1, rows, S5_WIDTH), lambda t: (t, 0, 0)),
                  _whole((rows, S5_LANES)), _whole((rows, S5_LANES))] + _s5_weight_specs(),
        out_specs=(pl.BlockSpec((1, rows, S5_WIDTH), lambda t: (t, 0, 0)),
                   _whole((rows, S5_LANES)), _whole((rows, S5_LANES))),
        out_shape=(jax.ShapeDtypeStruct((steps, rows, S5_WIDTH), BF16), state, state),
        scratch_shapes=[pltpu.VMEM((rows, S5_LANES), F32), pltpu.VMEM((rows, S5_LANES), F32)],
        compiler_params=_params("arbitrary"),
        name="s5_steps",
    )(u, h0_re, h0_im, *weights)


def s5_weights(lam_re, lam_im, log_dt, b_re, b_im, c_re, c_im, d_skip, w_glu, b_glu):
    tab, bbr, bbi = s5_params(lam_re, lam_im, log_dt, b_re, b_im)
    return (tab, _block_diag_in(bbr, bbi), _block_diag_out(c_re), _block_diag_out(c_im),
            d_skip.reshape(1, S5_WIDTH), w_glu.astype(BF16), b_glu.reshape(1, S5_WIDTH))


def _lane_pad_bias(b_f):
    return jnp.pad(b_f, (0, LANES - FOX_HEADS)).reshape(1, LANES)


def _log_forget(zf_ref, bf_ref):
    lane = lax.broadcasted_iota(jnp.int32, zf_ref.shape, 1)
    return jnp.where(lane < FOX_HEADS, jax.nn.log_sigmoid(zf_ref[...] + bf_ref[...]), 0.0)


def _gate_seq_kernel(zf_ref, bf_ref, lf_ref, c_ref, ct_ref):
    lf = _log_forget(zf_ref, bf_ref)
    lf_ref[...] = lf
    row = lax.broadcasted_iota(jnp.int32, lf.shape, 0)
    c = lf
    s = 1
    while s < lf.shape[0]:
        c = c + jnp.where(row >= s, pltpu.roll(c, s, 0), 0.0)
        s *= 2
    c_ref[0] = c
    ct_ref[0] = c.T[:FOX_HEADS]


def fox_gates_seq(zf, b_f, bsz, seq_len):
    return pl.pallas_call(
        _gate_seq_kernel,
        grid=(bsz,),
        in_specs=[pl.BlockSpec((seq_len, LANES), lambda b: (b, 0)), _whole((1, LANES))],
        out_specs=(pl.BlockSpec((seq_len, LANES), lambda b: (b, 0)),
                   pl.BlockSpec((1, seq_len, LANES), lambda b: (b, 0, 0)),
                   pl.BlockSpec((1, FOX_HEADS, seq_len), lambda b: (b, 0, 0))),
        out_shape=(jax.ShapeDtypeStruct((bsz * seq_len, LANES), F32),
                   jax.ShapeDtypeStruct((bsz, seq_len, LANES), F32),
                   jax.ShapeDtypeStruct((bsz, FOX_HEADS, seq_len), F32)),
        compiler_params=_params("parallel"),
        name="fox_gates_seq",
    )(zf, _lane_pad_bias(b_f))


def _gate_kernel(zf_ref, bf_ref, lf_ref):
    lf_ref[...] = _log_forget(zf_ref, bf_ref)


def fox_gates(zf, b_f):
    return pl.pallas_call(
        _gate_kernel,
        out_shape=jax.ShapeDtypeStruct(zf.shape, F32),
        name="fox_gates",
    )(zf, _lane_pad_bias(b_f))


FOX_TILE = 512
FOX_SCALE = FOX_HEAD_DIM ** -0.5


def _fox_seq_kernel(q_ref, k_ref, v_ref, c_ref, ct_ref, o_ref):
    h = pl.program_id(1)
    i = pl.program_id(2)
    q = q_ref[0].astype(BF16)
    lane = lax.broadcasted_iota(jnp.int32, c_ref.shape[1:], 1)
    cq = jnp.sum(jnp.where(lane == h, c_ref[0], 0.0), axis=1, keepdims=True)

    def chunk(j, carry, diagonal):
        m, l, acc = carry
        rows = pl.ds(pl.multiple_of(j * FOX_TILE, FOX_TILE), FOX_TILE)
        kc = k_ref[0, rows, :].astype(BF16)
        vc = v_ref[0, rows, :].astype(BF16)
        s = lax.dot_general(q, kc, NT_DIMS, preferred_element_type=F32) * FOX_SCALE
        s = s + cq - ct_ref[0, h, pl.ds(j, 1), :]
        if diagonal:
            qi = lax.broadcasted_iota(jnp.int32, s.shape, 0)
            ki = lax.broadcasted_iota(jnp.int32, s.shape, 1)
            s = jnp.where(qi >= ki, s, NEG_INF)
        m_new = jnp.maximum(m, jnp.max(s, axis=1, keepdims=True))
        alpha = jnp.exp(m - m_new)
        p = jnp.exp(s - m_new)
        l = alpha * l + jnp.sum(p, axis=1, keepdims=True)
        acc = alpha * acc + jnp.dot(p.astype(BF16), vc, preferred_element_type=F32)
        return m_new, l, acc

    init = (jnp.full((FOX_TILE, 1), NEG_INF, F32), jnp.zeros((FOX_TILE, 1), F32),
            jnp.zeros((FOX_TILE, FOX_HEAD_DIM), F32))
    carry = lax.fori_loop(0, i, functools.partial(chunk, diagonal=False), init)
    _, l, acc = chunk(i, carry, diagonal=True)
    o_ref[...] = (acc / l).astype(BF16)


def fox_sequences(z, c, ct, bsz, seq_len):
    nq = seq_len // FOX_TILE
    ct = ct.reshape(bsz, FOX_HEADS, nq, FOX_TILE)
    return pl.pallas_call(
        _fox_seq_kernel,
        grid=(bsz, FOX_HEADS, nq),
        in_specs=[
            pl.BlockSpec((1, FOX_TILE, FOX_HEAD_DIM), lambda b, h, i: (1, b * nq + i, h)),
            pl.BlockSpec((1, seq_len, FOX_HEAD_DIM), lambda b, h, i: (2, b, h)),
            pl.BlockSpec((1, seq_len, FOX_HEAD_DIM), lambda b, h, i: (3, b, h)),
            pl.BlockSpec((1, FOX_TILE, LANES), lambda b, h, i: (b, i, 0)),
            pl.BlockSpec((1, FOX_HEADS, nq, FOX_TILE), lambda b, h, i: (b, 0, 0, 0)),
        ],
        out_specs=pl.BlockSpec((FOX_TILE, FOX_HEAD_DIM), lambda b, h, i: (b * nq + i, h)),
        out_shape=jax.ShapeDtypeStruct((bsz * seq_len, FOX_WIDTH), BF16),
        compiler_params=_params("parallel", "parallel", "arbitrary"),
        name="fox_sequences",
    )(z, z, z, c, ct)


PAGE_ROWS = PAGE_SIZE * FOX_HEADS
NEW_ROWS = DEC_SEQ * FOX_HEADS


def _fox_paged_kernel(pt_ref, q_ref, kn_ref, vn_ref, ln_ref, *refs):
    k_pages = refs[:N_PAGES]
    v_pages = refs[N_PAGES:2 * N_PAGES]
    lf_pages = refs[2 * N_PAGES:3 * N_PAGES]
    o_ref, kpad_ref, vpad_ref = refs[3 * N_PAGES:]
    del pt_ref
    past = N_PAGES * PAGE_ROWS
    q = q_ref[0].astype(BF16)
    row = lax.broadcasted_iota(jnp.int32, (NEW_ROWS, 1), 0)
    row_head = jnp.bitwise_and(row, FOX_HEADS - 1)

    c = jnp.concatenate([r[0] for r in lf_pages], axis=1)
    lane = lax.broadcasted_iota(jnp.int32, c.shape, 1)
    s = FOX_HEADS
    while s < past:
        c = c + jnp.where(lane >= s, pltpu.roll(c, s, 1), 0.0)
        s *= 2
    tail_lane = lax.broadcasted_iota(jnp.int32, (NEW_ROWS, LANES), 1)
    tail = jnp.broadcast_to(c[:, past - LANES:], (NEW_ROWS, LANES))
    c_last = jnp.sum(jnp.where(tail_lane == LANES - FOX_HEADS + row_head, tail, 0.0), axis=1, keepdims=True)
    ln = ln_ref[0]
    blocks = [c_last[0:FOX_HEADS] + ln[0:FOX_HEADS]]
    for t in range(1, DEC_SEQ):
        blocks.append(blocks[-1] + ln[t * FOX_HEADS:(t + 1) * FOX_HEADS])
    c_new = jnp.concatenate(blocks, axis=0)
    c_new_row = jnp.sum(jnp.where(tail_lane == row, c_new, 0.0), axis=0, keepdims=True)

    kpad_ref[...] = jnp.zeros_like(kpad_ref)
    vpad_ref[...] = jnp.zeros_like(vpad_ref)
    kpad_ref[0:NEW_ROWS, :] = kn_ref[0]
    vpad_ref[0:NEW_ROWS, :] = vn_ref[0]

    s_past = jnp.concatenate(
        [lax.dot_general(q, kp[0].astype(BF16), NT_DIMS, preferred_element_type=F32) for kp in k_pages], axis=1)
    s_past = s_past * FOX_SCALE + c_new - c
    s_past = jnp.where(jnp.bitwise_and(lane, FOX_HEADS - 1) == row_head, s_past, NEG_INF)
    s_new = lax.dot_general(q, kpad_ref[...].astype(BF16), NT_DIMS, preferred_element_type=F32)
    s_new = s_new * FOX_SCALE + c_new - c_new_row
    visible = (jnp.bitwise_and(tail_lane, FOX_HEADS - 1) == row_head) & (tail_lane <= row)
    s_new = jnp.where(visible, s_new, NEG_INF)

    m = jnp.maximum(jnp.max(s_past, axis=1, keepdims=True), jnp.max(s_new, axis=1, keepdims=True))
    p_past = jnp.exp(s_past - m)
    p_new = jnp.exp(s_new - m)
    total = jnp.sum(p_past, axis=1, keepdims=True) + jnp.sum(p_new, axis=1, keepdims=True)
    p_past = p_past.astype(BF16)
    acc = jnp.dot(p_new.astype(BF16), vpad_ref[...].astype(BF16), preferred_element_type=F32)
    for j, vp in enumerate(v_pages):
        acc = acc + jnp.dot(p_past[:, j * PAGE_ROWS:(j + 1) * PAGE_ROWS], vp[0].astype(BF16),
                            preferred_element_type=F32)
    o_ref[0] = acc / total


def fox_paged(q, k_new, v_new, logf_new, cache_k, cache_v, cache_logf, page_table, first_page):
    bsz = q.shape[0]

    def page(j):
        return lambda b, pt: (first_page + pt[b, j], 0, 0)

    new_spec = pl.BlockSpec((1, NEW_ROWS, FOX_HEAD_DIM), lambda b, pt: (b, 0, 0))
    in_specs = [new_spec, new_spec, new_spec, pl.BlockSpec((1, NEW_ROWS, 1), lambda b, pt: (b, 0, 0))]
    in_specs += [pl.BlockSpec((1, PAGE_ROWS, FOX_HEAD_DIM), page(j)) for j in range(N_PAGES)]
    in_specs += [pl.BlockSpec((1, PAGE_ROWS, FOX_HEAD_DIM), page(j)) for j in range(N_PAGES)]
    in_specs += [pl.BlockSpec((1, 1, PAGE_ROWS), page(j)) for j in range(N_PAGES)]
    return pl.pallas_call(
        _fox_paged_kernel,
        grid_spec=pltpu.PrefetchScalarGridSpec(
            num_scalar_prefetch=1,
            grid=(bsz,),
            in_specs=in_specs,
            out_specs=new_spec,
            scratch_shapes=[pltpu.VMEM((LANES, FOX_HEAD_DIM), F32), pltpu.VMEM((LANES, FOX_HEAD_DIM), F32)],
        ),
        out_shape=jax.ShapeDtypeStruct((bsz, NEW_ROWS, FOX_HEAD_DIM), F32),
        compiler_params=_params("parallel"),
        name="fox_paged",
    )(page_table, q, k_new, v_new, logf_new,
      *([cache_k] * N_PAGES), *([cache_v] * N_PAGES), *([cache_logf] * N_PAGES))


ROPE_HALF = RET_KD // 2
RET_SCALE = RET_KD ** -0.5
RET_LOG_GAMMA = [float(np.log1p(-np.float32(2.0 ** (-5.0 - h)))) for h in range(RET_HEADS)]


def _rope_kernel(f_ref, cos_ref, sin_ref, *, start):
    pos = (start + lax.broadcasted_iota(jnp.int32, cos_ref.shape, 0)).astype(F32)
    ang = pos * f_ref[...]
    cos_ref[...] = jnp.cos(ang)
    sin_ref[...] = jnp.sin(ang)


def rope_table(start, n):
    inv_freq = ROPE_BASE ** (-jnp.arange(ROPE_HALF, dtype=F32) / ROPE_HALF)
    shape = jax.ShapeDtypeStruct((n, ROPE_HALF), F32)
    return pl.pallas_call(
        functools.partial(_rope_kernel, start=start), out_shape=(shape, shape), name="rope_table",
    )(inv_freq.reshape(1, ROPE_HALF))


def _rotate(x, cos, sin):
    x1, x2 = x[:, :ROPE_HALF], x[:, ROPE_HALF:]
    return jnp.concatenate([x1 * cos - x2 * sin, x1 * sin + x2 * cos], axis=1)


def _group_norm_gate(o, g, gain):
    mu = jnp.mean(o, axis=-1, keepdims=True)
    var = jnp.mean(jnp.square(o - mu), axis=-1, keepdims=True)
    return g * jax.nn.sigmoid(g) * ((o - mu) * lax.rsqrt(var + EPS) * gain)


def _head_slabs(refs, h, per_slab):
    width = PROJ_TILE // per_slab
    return refs[h // per_slab], slice((h % per_slab) * width, (h % per_slab + 1) * width)


def _ret_seq_kernel(*refs):
    q_refs, k_refs, v_refs, g_refs = refs[0:2], refs[2:4], refs[4:8], refs[8:12]
    cos_ref, sin_ref, gain_ref, o_ref, s_out, s_ref = refs[12:]
    step = pl.program_id(1)

    @pl.when(step == 0)
    def _():
        s_ref[...] = jnp.zeros_like(s_ref)

    cos = cos_ref[...]
    sin = sin_ref[...]
    cl = RET_CHUNK
    i_col = lax.broadcasted_iota(jnp.int32, (cl, 1), 0).astype(F32)
    diff = i_col - lax.broadcasted_iota(jnp.int32, (1, cl), 1).astype(F32)
    for h in range(RET_HEADS):
        lg = RET_LOG_GAMMA[h]
        ref, sl = _head_slabs(q_refs, h, 4)
        q = _rotate(ref[0, :, sl], cos, sin).astype(BF16)
        ref, sl = _head_slabs(k_refs, h, 4)
        k = _rotate(ref[0, :, sl], cos, sin) * RET_SCALE
        ref, sl = _head_slabs(v_refs, h, 2)
        v = ref[0, :, sl].astype(BF16)
        ref, sl = _head_slabs(g_refs, h, 2)
        g = ref[0, :, sl]
        dmask = jnp.where(diff >= 0, jnp.exp(jnp.maximum(diff, 0.0) * lg), 0.0)
        inner = lax.dot_general(q, k.astype(BF16), NT_DIMS, preferred_element_type=F32) * dmask
        state = s_ref[h]
        out = jnp.dot(inner.astype(BF16), v, preferred_element_type=F32)
        out = out + jnp.dot(q, state.astype(BF16), preferred_element_type=F32) * jnp.exp((i_col + 1.0) * lg)
        k_dec = k * jnp.exp((cl - 1.0 - i_col) * lg)
        s_ref[h] = math.exp(cl * lg) * state + jnp.dot(k_dec.T.astype(BF16), v, preferred_element_type=F32)
        hs = slice(h * RET_VD, (h + 1) * RET_VD)
        o_ref[:, hs] = _group_norm_gate(out, g, gain_ref[:, hs]).astype(BF16)

    @pl.when(step == pl.num_programs(1) - 1)
    def _():
        s_out[0] = s_ref[...]


def ret_sequences(z, cos, sin, gain, bsz, seq_len):
    steps = seq_len // RET_CHUNK
    vd = RET_HEADS * RET_VD
    slab = lambda j: pl.BlockSpec((1, RET_CHUNK, PROJ_TILE), lambda b, c: (j, b * steps + c, 0))
    table = pl.BlockSpec((RET_CHUNK, ROPE_HALF), lambda b, c: (c, 0))
    return pl.pallas_call(
        _ret_seq_kernel,
        grid=(bsz, steps),
        in_specs=[slab(j) for j in range(12)] + [table, table, _whole((1, vd))],
        out_specs=(pl.BlockSpec((RET_CHUNK, vd), lambda b, c: (b * steps + c, 0)),
                   pl.BlockSpec((1, RET_HEADS, RET_KD, RET_VD), lambda b, c: (b, 0, 0, 0))),
        out_shape=(jax.ShapeDtypeStruct((bsz * seq_len, vd), BF16),
                   jax.ShapeDtypeStruct((bsz, RET_HEADS, RET_KD, RET_VD), F32)),
        scratch_shapes=[pltpu.VMEM((RET_HEADS, RET_KD, RET_VD), F32)],
        compiler_params=_params("parallel", "arbitrary"),
        name="ret_sequences",
    )(*([z] * 12), cos, sin, gain.reshape(1, vd))


RET_PAD = 128


RET_STEP_SEQS = 8 // DEC_SEQ


def _ret_step_kernel(z_ref, s0_ref, cos_ref, sin_ref, gain_ref, o_ref, s_out, qpad, kpad, vpad, gpad):
    cl = DEC_SEQ
    for ref in (qpad, kpad, vpad, gpad):
        ref[...] = jnp.zeros_like(ref)
    cos = cos_ref[...]
    sin = sin_ref[...]
    i_col = lax.broadcasted_iota(jnp.int32, (RET_PAD, 1), 0).astype(F32)
    diff = i_col - lax.broadcasted_iota(jnp.int32, (1, RET_PAD), 1).astype(F32)
    valid = (diff >= 0) & (i_col < cl)
    for b in range(RET_STEP_SEQS):
        tok = slice(b * cl, (b + 1) * cl)
        outs = []
        for h in range(RET_HEADS):
            lg = RET_LOG_GAMMA[h]
            qpad[0:cl, :] = z_ref[h // 4, tok, (h % 4) * RET_KD:(h % 4 + 1) * RET_KD]
            kpad[0:cl, :] = z_ref[2 + h // 4, tok, (h % 4) * RET_KD:(h % 4 + 1) * RET_KD]
            vpad[0:cl, :] = z_ref[4 + h // 2, tok, (h % 2) * RET_VD:(h % 2 + 1) * RET_VD]
            gpad[0:cl, :] = z_ref[8 + h // 2, tok, (h % 2) * RET_VD:(h % 2 + 1) * RET_VD]
            q = _rotate(qpad[...], cos, sin).astype(BF16)
            k = _rotate(kpad[...], cos, sin) * RET_SCALE
            v = vpad[...].astype(BF16)
            dmask = jnp.where(valid, jnp.exp(jnp.maximum(diff, 0.0) * lg), 0.0)
            inner = lax.dot_general(q, k.astype(BF16), NT_DIMS, preferred_element_type=F32) * dmask
            state = s0_ref[b, h]
            out = jnp.dot(inner.astype(BF16), v, preferred_element_type=F32)
            out = out + jnp.dot(q, state.astype(BF16), preferred_element_type=F32) * jnp.exp((i_col + 1.0) * lg)
            k_dec = k * jnp.exp((cl - 1.0 - i_col) * lg)
            s_out[b, h] = math.exp(cl * lg) * state + jnp.dot(k_dec.T.astype(BF16), v, preferred_element_type=F32)
            hs = slice(h * RET_VD, (h + 1) * RET_VD)
            outs.append(_group_norm_gate(out[0:8], gpad[0:8, :], gain_ref[:, hs]))
        o_ref[tok, :] = jnp.concatenate(outs, axis=1)[0:cl]


def ret_steps(z, first_row, s0, cos, sin, gain):
    bsz = s0.shape[0]
    vd = RET_HEADS * RET_VD
    rows = RET_STEP_SEQS * DEC_SEQ
    state_spec = pl.BlockSpec((RET_STEP_SEQS, RET_HEADS, RET_KD, RET_VD), lambda i: (i, 0, 0, 0))
    return pl.pallas_call(
        _ret_step_kernel,
        grid=(bsz // RET_STEP_SEQS,),
        in_specs=[pl.BlockSpec((12, rows, PROJ_TILE), lambda i: (0, first_row // rows + i, 0)), state_spec,
                  _whole((RET_PAD, ROPE_HALF)), _whole((RET_PAD, ROPE_HALF)), _whole((1, vd))],
        out_specs=(pl.BlockSpec((rows, vd), lambda i: (i, 0)), state_spec),
        out_shape=(jax.ShapeDtypeStruct((bsz * DEC_SEQ, vd), F32),
                   jax.ShapeDtypeStruct(s0.shape, F32)),
        scratch_shapes=[pltpu.VMEM((RET_PAD, RET_KD), F32), pltpu.VMEM((RET_PAD, RET_KD), F32),
                        pltpu.VMEM((RET_PAD, RET_VD), F32), pltpu.VMEM((RET_PAD, RET_VD), F32)],
        compiler_params=_params("parallel"),
        name="ret_steps",
    )(z, s0, cos, sin, gain.reshape(1, vd))


def _even_mixer(h, gain, e, cache_k, cache_v, cache_logf, state_s5_re, state_s5_im, page_table, w_in, b_f,
                s5_w):
    z, zf = norm_proj(h, gain, w_in, 4 * PROJ_TILE, narrow=True)

    s5_p, hr_p, hi_p = s5_sequences(z, s5_w, BATCH, SEQ)
    u_s = jnp.stack([z[0, N_PROMPT + t::DEC_SEQ] for t in range(DEC_SEQ)])
    s5_s, hr_s, hi_s = s5_steps(u_s, state_s5_re.reshape(DEC_BATCH, S5_LANES),
                                state_s5_im.reshape(DEC_BATCH, S5_LANES), s5_w)
    s5_all = jnp.concatenate([s5_p, s5_s.transpose(1, 0, 2).reshape(N_SAMPLE, S5_WIDTH)], axis=0)

    lf_p, c_p, ct_p = fox_gates_seq(zf, b_f, BATCH, SEQ)
    lf_s = fox_gates(zf[N_PROMPT:], b_f)[:, :FOX_HEADS]
    k_p, v_p, q_s, k_s, v_s = head_rows(z)
    att_p = fox_sequences(z, c_p, ct_p, BATCH, SEQ)
    pool = cache_k.shape[1]
    new_rows = lambda t: t.reshape(DEC_BATCH, NEW_ROWS, FOX_HEAD_DIM)
    att_s = fox_paged(new_rows(q_s), new_rows(k_s), new_rows(v_s), lf_s.reshape(DEC_BATCH, NEW_ROWS, 1),
                      cache_k.reshape(-1, PAGE_ROWS, FOX_HEAD_DIM), cache_v.reshape(-1, PAGE_ROWS, FOX_HEAD_DIM),
                      cache_logf.reshape(-1, 1, PAGE_ROWS), page_table, e * pool)
    att_all = jnp.concatenate([att_p, att_s.reshape(N_SAMPLE, FOX_WIDTH).astype(BF16)], axis=0)

    heads = lambda t, b, s: t.reshape(b, s, FOX_HEADS, FOX_HEAD_DIM)
    grid = lambda t: t.reshape(-1, S5_GROUPS, S5_STATE)
    st_p = (heads(k_p, BATCH, SEQ), heads(v_p, BATCH, SEQ),
            lf_p[:, :FOX_HEADS].reshape(BATCH, SEQ, FOX_HEADS), grid(hr_p), grid(hi_p))
    st_s = (heads(k_s, DEC_BATCH, DEC_SEQ), heads(v_s, DEC_BATCH, DEC_SEQ),
            lf_s.reshape(DEC_BATCH, DEC_SEQ, FOX_HEADS), grid(hr_s), grid(hi_s))
    return [s5_all, att_all], st_p, st_s


def _odd_mixer(h, gain, w_in, gn_gain, state, tables):
    z = norm_proj(h, gain, w_in, 12 * PROJ_TILE)
    (cos_p, sin_p), (cos_s, sin_s) = tables
    o_p, st_p = ret_sequences(z, cos_p, sin_p, gn_gain, BATCH, SEQ)
    o_s, st_s = ret_steps(z, N_PROMPT, state, cos_s, sin_s, gn_gain)
    o_all = jnp.concatenate([o_p, o_s.astype(BF16)], axis=0)
    return [o_all], st_p, st_s


def kernel(x_prompt, x_sample, cache_k, cache_v, cache_logf, state_s5_re, state_s5_im, state_ret, page_table, norm_gain, w_ffn_gate, w_ffn_up, w_ffn_down, w_in_even, b_forget, s5_lam_re, s5_lam_im, s5_b_re, s5_b_im, s5_c_re, s5_c_im, s5_d, s5_log_dt, w_glu, b_glu, w_out_even, w_in_odd, ret_gn_gain, w_out_odd, final_norm):
    x = jnp.concatenate([x_prompt.reshape(N_PROMPT, D_MODEL), x_sample.reshape(N_SAMPLE, D_MODEL)], axis=0)
    tables = (rope_table(0, SEQ), rope_table(PAST_LEN, RET_PAD))
    ev_p, ev_s, od_p, od_s = [], [], [], []
    for layer in range(DEPTH):
        h = ffn(x, norm_gain[layer, 0], w_ffn_gate[layer, 0], w_ffn_up[layer, 0], w_ffn_down[layer, 0])
        if layer % 2 == 0:
            e = layer // 2
            s5_w = s5_weights(s5_lam_re[e], s5_lam_im[e], s5_log_dt[e], s5_b_re[e], s5_b_im[e], s5_c_re[e],
                              s5_c_im[e], s5_d[e], w_glu[e], b_glu[e])
            parts, st_p, st_s = _even_mixer(h, norm_gain[layer, 1], e, cache_k, cache_v, cache_logf,
                                            state_s5_re[e], state_s5_im[e], page_table, w_in_even[e],
                                            b_forget[e], s5_w)
            ev_p.append(st_p)
            ev_s.append(st_s)
            w_out = w_out_even[e]
        else:
            o = layer // 2
            parts, st_p, st_s = _odd_mixer(h, norm_gain[layer, 1], w_in_odd[o], ret_gn_gain[o], state_ret[o],
                                           tables)
            od_p.append(st_p)
            od_s.append(st_s)
            w_out = w_out_odd[o]
        h = out_proj(parts, w_out, h)
        x = ffn(h, norm_gain[layer, 2], w_ffn_gate[layer, 1], w_ffn_up[layer, 1], w_ffn_down[layer, 1])

    y_prompt, y_sample = final_rms(x, final_norm)
    y_prompt = y_prompt.reshape(BATCH, SEQ, D_MODEL)
    y_sample = y_sample.reshape(DEC_BATCH, DEC_SEQ, D_MODEL)
    stack = lambda sts, i: jnp.stack([s[i] for s in sts])
    return (y_prompt, y_sample,
            stack(ev_p, 0), stack(ev_p, 1), stack(ev_p, 2), stack(ev_p, 3), stack(ev_p, 4), jnp.stack(od_p),
            stack(ev_s, 0), stack(ev_s, 1), stack(ev_s, 2), stack(ev_s, 3), stack(ev_s, 4), jnp.stack(od_s))
```

```python
import functools
import math

import jax
import jax.numpy as jnp
import numpy as np
from jax import lax
from jax.experimental import pallas as pl
from jax.experimental.pallas import tpu as pltpu

D_MODEL = 2048
BATCH = 2
SEQ = 4096
DEPTH = 2
DEC_BATCH = 128
DEC_SEQ = 4
PAST_LEN = 2048
PAGE_SIZE = 128
N_PAGES = PAST_LEN // PAGE_SIZE
S5_WIDTH = 1024
S5_GROUP = 16
S5_GROUPS = 64
S5_STATE = 64
FOX_WIDTH = 1024
FOX_HEAD_DIM = 128
FOX_HEADS = 8
NEG_INF = -1e30
RET_HEADS = 8
RET_KD = 256
RET_VD = 512
RET_CHUNK = 128
ROPE_BASE = 10000.0
D_FF = 5504
EPS = 1e-6

N_PROMPT = BATCH * SEQ
N_SAMPLE = DEC_BATCH * DEC_SEQ
N_TOK = N_PROMPT + N_SAMPLE

ROW_TILE = 1088
FF_TILE = 256
LANES = 128
VMEM_LIMIT = 56 * 1024 * 1024
FFN_VMEM_LIMIT = 60 * 1024 * 1024

F32 = jnp.float32
BF16 = jnp.bfloat16
NT_DIMS = (((1,), (1,)), ((), ()))


def _rms_rows(x, gain):
    return x * lax.rsqrt(jnp.mean(x * x, axis=-1, keepdims=True) + EPS) * gain


def _whole(shape):
    return pl.BlockSpec(shape, lambda *_: (0,) * len(shape))


def _params(*semantics):
    return pltpu.CompilerParams(dimension_semantics=semantics, vmem_limit_bytes=VMEM_LIMIT)


FFN_GROUP = 2


def _ffn_kernel(x_ref, g_ref, wg_ref, wu_ref, wd_ref, o_ref, h_ref, *, d_ff):
    step = pl.program_id(1)
    tf = wg_ref.shape[1]

    @pl.when(step < FFN_GROUP)
    def _():
        rows = pl.ds(pl.multiple_of(step * ROW_TILE, ROW_TILE), ROW_TILE)
        x = x_ref[...]
        h_ref[rows, :] = _rms_rows(x, g_ref[...]).astype(BF16)
        o_ref[rows, :] = x

    @pl.when(step >= FFN_GROUP)
    def _():
        first = (step - FFN_GROUP) * tf
        wg = wg_ref[...].astype(BF16)
        wu = wu_ref[...].astype(BF16)
        col_ok = first + lax.broadcasted_iota(jnp.int32, (1, tf), 1) < d_ff
        row_ok = first + lax.broadcasted_iota(jnp.int32, (tf, 1), 0) < d_ff
        wd = jnp.where(row_ok, wd_ref[...], 0.0).astype(BF16)
        for t in range(FFN_GROUP):
            rows = slice(t * ROW_TILE, (t + 1) * ROW_TILE)
            h = h_ref[rows, :]
            a = jnp.dot(h, wg, preferred_element_type=F32)
            b = jnp.dot(h, wu, preferred_element_type=F32)
            c = jnp.where(col_ok, a * jax.nn.sigmoid(a) * b, 0.0).astype(BF16)
            o_ref[rows, :] += 0.5 * jnp.dot(c, wd, preferred_element_type=F32)


def ffn(x, gain, wg, wu, wd, layer, which):
    m, d = x.shape
    d_ff = wg.shape[3]
    group_rows = FFN_GROUP * ROW_TILE
    w_tile = lambda r, s: jnp.maximum(s - FFN_GROUP, 0)
    return pl.pallas_call(
        functools.partial(_ffn_kernel, d_ff=d_ff),
        grid=(m // group_rows, FFN_GROUP + pl.cdiv(d_ff, FF_TILE)),
        in_specs=[
            pl.BlockSpec((ROW_TILE, d), lambda r, s: (FFN_GROUP * r + jnp.minimum(s, FFN_GROUP - 1), 0),
                         pipeline_mode=pl.Buffered(1)),
            pl.BlockSpec((1, d), lambda r, s: (0, 0)),
            pl.BlockSpec((None, None, d, FF_TILE), lambda r, s: (layer, which, 0, w_tile(r, s))),
            pl.BlockSpec((None, None, d, FF_TILE), lambda r, s: (layer, which, 0, w_tile(r, s))),
            pl.BlockSpec((None, None, FF_TILE, d), lambda r, s: (layer, which, w_tile(r, s), 0)),
        ],
        out_specs=pl.BlockSpec((group_rows, d), lambda r, s: (r, 0), pipeline_mode=pl.Buffered(1)),
        out_shape=jax.ShapeDtypeStruct((m, d), F32),
        scratch_shapes=[pltpu.VMEM((group_rows, d), BF16)],
        compiler_params=pltpu.CompilerParams(
            dimension_semantics=("parallel", "arbitrary"), vmem_limit_bytes=FFN_VMEM_LIMIT),
        name="ffn",
    )(x, gain.reshape(1, d), wg, wu, wd)


SPLIT_TILE = 512
N_PROMPT_TILES = N_PROMPT // SPLIT_TILE


def _split_specs(block):
    zeros = (0,) * (len(block) - 1)
    return (pl.BlockSpec(block, lambda i: (jnp.minimum(i, N_PROMPT_TILES - 1),) + zeros),
            pl.BlockSpec(block, lambda i: (0,) + zeros))


def _final_norm_kernel(x_ref, g_ref, yp_ref, ys_ref):
    y = _rms_rows(x_ref[...], g_ref[...])

    @pl.when(pl.program_id(0) < N_PROMPT_TILES)
    def _():
        yp_ref[...] = y

    @pl.when(pl.program_id(0) == N_PROMPT_TILES)
    def _():
        ys_ref[...] = y


def final_rms(x, gain):
    d = x.shape[1]
    return pl.pallas_call(
        _final_norm_kernel,
        grid=(N_PROMPT_TILES + 1,),
        in_specs=[pl.BlockSpec((SPLIT_TILE, d), lambda i: (i, 0)), _whole((1, d))],
        out_specs=_split_specs((SPLIT_TILE, d)),
        out_shape=(jax.ShapeDtypeStruct((N_PROMPT, d), F32), jax.ShapeDtypeStruct((N_SAMPLE, d), F32)),
        compiler_params=_params("arbitrary"),
        name="final_norm",
    )(x, gain.reshape(1, d))


def _head_rows_kernel(q_ref, k_ref, v_ref, kp_ref, vp_ref, qs_ref, ks_ref, vs_ref):
    def put(pairs):
        for src, dst in pairs:
            for h in range(FOX_HEADS):
                dst[:, h, :] = src[0, :, h * FOX_HEAD_DIM:(h + 1) * FOX_HEAD_DIM]

    @pl.when(pl.program_id(0) < N_PROMPT_TILES)
    def _():
        put([(k_ref, kp_ref), (v_ref, vp_ref)])

    @pl.when(pl.program_id(0) == N_PROMPT_TILES)
    def _():
        put([(q_ref, qs_ref), (k_ref, ks_ref), (v_ref, vs_ref)])


def head_rows(z):
    block = (SPLIT_TILE, FOX_HEADS, FOX_HEAD_DIM)
    shape = lambda n: jax.ShapeDtypeStruct((n, FOX_HEADS, FOX_HEAD_DIM), F32)
    kp_spec, ks_spec = _split_specs(block)
    return pl.pallas_call(
        _head_rows_kernel,
        grid=(N_PROMPT_TILES + 1,),
        in_specs=[pl.BlockSpec((1, SPLIT_TILE, FOX_WIDTH), lambda i: (1, N_PROMPT_TILES, 0)),
                  pl.BlockSpec((1, SPLIT_TILE, FOX_WIDTH), lambda i: (2, i, 0)),
                  pl.BlockSpec((1, SPLIT_TILE, FOX_WIDTH), lambda i: (3, i, 0))],
        out_specs=(kp_spec, kp_spec, ks_spec, ks_spec, ks_spec),
        out_shape=(shape(N_PROMPT), shape(N_PROMPT), shape(N_SAMPLE), shape(N_SAMPLE), shape(N_SAMPLE)),
        compiler_params=_params("arbitrary"),
        name="head_rows",
    )(z, z, z)


PROJ_TILE = 1024


def _norm_proj_kernel(x_ref, g_ref, w_ref, *rest, narrow):
    if narrow:
        wn_ref, z_ref, zn_ref, h_ref = rest
    else:
        z_ref, h_ref = rest
    j = pl.program_id(1)

    @pl.when(j == 0)
    def _():
        h_ref[...] = _rms_rows(x_ref[...], g_ref[...]).astype(BF16)
        if narrow:
            zn_ref[...] = jnp.dot(h_ref[...], wn_ref[...].astype(BF16), preferred_element_type=F32)

    z_ref[0] = jnp.dot(h_ref[...], w_ref[...].astype(BF16), preferred_element_type=F32)


def norm_proj(x, gain, w, n_wide, narrow=False):
    m, d = x.shape
    nj = n_wide // PROJ_TILE
    in_specs = [
        pl.BlockSpec((ROW_TILE, d), lambda i, j: (i, 0), pipeline_mode=pl.Buffered(1)),
        pl.BlockSpec((1, d), lambda i, j: (0, 0)),
        pl.BlockSpec((d, PROJ_TILE), lambda i, j: (0, j)),
    ]
    args = [x, gain.reshape(1, d), w]
    out_shape = jax.ShapeDtypeStruct((nj, m, PROJ_TILE), F32)
    out_specs = pl.BlockSpec((1, ROW_TILE, PROJ_TILE), lambda i, j: (j, i, 0))
    if narrow:
        in_specs.append(pl.BlockSpec((d, LANES), lambda i, j: (0, n_wide // LANES)))
        args.append(w)
        out_shape = (out_shape, jax.ShapeDtypeStruct((m, LANES), F32))
        out_specs = (out_specs, pl.BlockSpec((ROW_TILE, LANES), lambda i, j: (i, 0)))
    return pl.pallas_call(
        functools.partial(_norm_proj_kernel, narrow=narrow),
        grid=(m // ROW_TILE, nj),
        in_specs=in_specs,
        out_specs=out_specs,
        out_shape=out_shape,
        scratch_shapes=[pltpu.VMEM((ROW_TILE, d), BF16)],
        compiler_params=_params("parallel", "arbitrary"),
        name="norm_proj_narrow" if narrow else "norm_proj",
    )(*args)


OUT_TILE = 512


def _out_proj_kernel(*refs, n_parts):
    a_refs, w_refs, (r_ref, o_ref) = refs[:n_parts], refs[n_parts:2 * n_parts], refs[2 * n_parts:]
    acc = r_ref[...]
    for a_ref, w_ref in zip(a_refs, w_refs):
        acc = acc + jnp.dot(a_ref[...], w_ref[...].astype(BF16), preferred_element_type=F32)
    o_ref[...] = acc


def out_proj(parts, w, res):
    m, k = parts[0].shape
    n = w.shape[1]
    n_parts = len(parts)
    return pl.pallas_call(
        functools.partial(_out_proj_kernel, n_parts=n_parts),
        grid=(m // ROW_TILE, n // OUT_TILE),
        in_specs=([pl.BlockSpec((ROW_TILE, k), lambda i, j: (i, 0))] * n_parts
                  + [pl.BlockSpec((k, OUT_TILE), functools.partial(lambda i, j, p: (p, j), p=p))
                     for p in range(n_parts)]
                  + [pl.BlockSpec((ROW_TILE, OUT_TILE), lambda i, j: (i, j))]),
        out_specs=pl.BlockSpec((ROW_TILE, OUT_TILE), lambda i, j: (i, j)),
        out_shape=jax.ShapeDtypeStruct((m, n), F32),
        compiler_params=_params("parallel", "arbitrary"),
        name="out_proj",
    )(*parts, *([w] * n_parts), res)


S5_LANES = S5_GROUPS * S5_STATE
S5_BLOCKS = 4
S5_BLOCK_CH = S5_WIDTH // S5_BLOCKS
S5_BLOCK_ST = S5_LANES // S5_BLOCKS
SCAN_LANES = 512
S5_TILE = 256


def _s5_param_kernel(lr_ref, li_ref, ldt_ref, bre_ref, bim_ref, tab_ref, bbr_ref, bbi_ref):
    lr = lr_ref[...]
    li = li_ref[...]
    dt = jnp.exp(ldt_ref[...])
    dec = lr * dt
    ang = li * dt
    er = jnp.exp(dec)
    xr = er * jnp.cos(ang) - 1.0
    xi = er * jnp.sin(ang)
    den = lr * lr + li * li
    cr = (xr * lr + xi * li) / den
    ci = (xi * lr - xr * li) / den
    bre = bre_ref[...]
    bim = bim_ref[...]
    bbr_ref[...] = cr * bre - ci * bim
    bbi_ref[...] = cr * bim + ci * bre

    row = lax.broadcasted_iota(jnp.int32, (8, S5_LANES), 0)

    def power(k):
        mag = jnp.exp(k * dec)
        return mag * jnp.cos(k * ang), mag * jnp.sin(k * ang)

    for idx, s in enumerate((1, 2, 4)):
        pr, pi = power(float(s))
        tab_ref[2 * idx] = jnp.where(row >= s, pr, 0.0)
        tab_ref[2 * idx + 1] = jnp.where(row >= s, pi, 0.0)
    pr, pi = power((row + 1).astype(F32))
    tab_ref[6] = pr
    tab_ref[7] = pi


def s5_params(lam_re, lam_im, log_dt, b_re, b_im):
    flat = lambda t: t.reshape(1, S5_LANES)
    ldt = jnp.broadcast_to(log_dt[:, None], (S5_GROUPS, S5_STATE))
    to_rows = lambda t: t.reshape(S5_LANES, S5_GROUP).T
    return pl.pallas_call(
        _s5_param_kernel,
        out_shape=(jax.ShapeDtypeStruct((8, 8, S5_LANES), F32),
                   jax.ShapeDtypeStruct((S5_GROUP, S5_LANES), F32),
                   jax.ShapeDtypeStruct((S5_GROUP, S5_LANES), F32)),
        name="s5_params",
    )(flat(lam_re), flat(lam_im), flat(ldt), to_rows(b_re), to_rows(b_im))


def _block_diag_in(bbr, bbi):
    bb = jnp.stack([bbr, bbi]).reshape(2, S5_GROUP, S5_BLOCKS, 16, S5_STATE)
    eye = jnp.eye(16, dtype=F32)
    t = bb[:, :, :, :, None, :] * eye[None, None, None, :, :, None]
    return t.transpose(2, 3, 1, 0, 4, 5).reshape(S5_BLOCKS, S5_BLOCK_CH, 2 * S5_BLOCK_ST).astype(BF16)


def _block_diag_out(c):
    cc = c.reshape(S5_BLOCKS, 16, S5_GROUP, S5_STATE)
    eye = jnp.eye(16, dtype=F32)
    t = cc.transpose(0, 1, 3, 2)[:, :, :, None, :] * eye[None, :, None, :, None]
    return t.reshape(S5_BLOCKS, S5_BLOCK_ST, S5_BLOCK_CH).astype(BF16)


def _s5_drive(ub, bbd_ref, hr_ref, hi_ref):
    for blk in range(S5_BLOCKS):
        bu = jnp.dot(ub[:, blk * S5_BLOCK_CH:(blk + 1) * S5_BLOCK_CH], bbd_ref[blk], preferred_element_type=F32)
        hr_ref[:, blk * S5_BLOCK_ST:(blk + 1) * S5_BLOCK_ST] = bu[:, :S5_BLOCK_ST]
        hi_ref[:, blk * S5_BLOCK_ST:(blk + 1) * S5_BLOCK_ST] = bu[:, S5_BLOCK_ST:]


def _s5_readout(u, hr_ref, hi_ref, cr_ref, ci_ref, d_ref, wg_ref, bg_ref):
    ys = []
    for blk in range(S5_BLOCKS):
        sl = slice(blk * S5_BLOCK_ST, (blk + 1) * S5_BLOCK_ST)
        yr = jnp.dot(hr_ref[:, sl].astype(BF16), cr_ref[blk], preferred_element_type=F32)
        yi = jnp.dot(hi_ref[:, sl].astype(BF16), ci_ref[blk], preferred_element_type=F32)
        ys.append(yr - yi)
    y = jax.nn.gelu(jnp.concatenate(ys, axis=1) + d_ref[...] * u)
    gate = jax.nn.sigmoid(jnp.dot(y.astype(BF16), wg_ref[...], preferred_element_type=F32) + bg_ref[...])
    return (y * gate).astype(BF16)


def _s5_seq_kernel(u_ref, tab_ref, bbd_ref, cr_ref, ci_ref, d_ref, wg_ref, bg_ref,
                   o_ref, hr_out, hi_out, hr_ref, hi_ref, car_r, car_i):
    step = pl.program_id(1)

    @pl.when(step == 0)
    def _():
        car_r[...] = jnp.zeros_like(car_r)
        car_i[...] = jnp.zeros_like(car_i)

    u = u_ref[0]
    _s5_drive(u.astype(BF16), bbd_ref, hr_ref, hi_ref)

    n_tiles = u.shape[0] // 8
    for c in range(S5_LANES // SCAN_LANES):
        sl = slice(c * SCAN_LANES, (c + 1) * SCAN_LANES)
        tabs = [tab_ref[k, :, sl] for k in range(8)]

        def tile(i, carry, sl=sl, tabs=tabs):
            cr, ci = carry
            rows = pl.ds(pl.multiple_of(i * 8, 8), 8)
            xr = hr_ref[rows, sl]
            xi = hi_ref[rows, sl]
            for k, s in enumerate((1, 2, 4)):
                ar, ai = tabs[2 * k], tabs[2 * k + 1]
                rr = pltpu.roll(xr, s, 0)
                ri = pltpu.roll(xi, s, 0)
                xr, xi = xr + ar * rr - ai * ri, xi + ar * ri + ai * rr
            pr, pi = tabs[6], tabs[7]
            hr = xr + pr * cr - pi * ci
            hi = xi + pr * ci + pi * cr
            hr_ref[rows, sl] = hr
            hi_ref[rows, sl] = hi
            return (jnp.broadcast_to(hr[7:8], hr.shape), jnp.broadcast_to(hi[7:8], hi.shape))

        cr, ci = lax.fori_loop(0, n_tiles, tile, (car_r[:, sl], car_i[:, sl]), unroll=2)
        car_r[:, sl] = cr
        car_i[:, sl] = ci

    o_ref[...] = _s5_readout(u, hr_ref, hi_ref, cr_ref, ci_ref, d_ref, wg_ref, bg_ref)

    @pl.when(step == pl.num_programs(1) - 1)
    def _():
        hr_out[0] = car_r[0:1, :]
        hi_out[0] = car_i[0:1, :]


def _s5_weight_specs():
    return [_whole((8, 8, S5_LANES)), _whole((S5_BLOCKS, S5_BLOCK_CH, 2 * S5_BLOCK_ST)),
            _whole((S5_BLOCKS, S5_BLOCK_ST, S5_BLOCK_CH)), _whole((S5_BLOCKS, S5_BLOCK_ST, S5_BLOCK_CH)),
            _whole((1, S5_WIDTH)), _whole((S5_WIDTH, S5_WIDTH)), _whole((1, S5_WIDTH))]


def s5_sequences(z, weights, bsz, seq_len):
    steps = seq_len // S5_TILE
    state = jax.ShapeDtypeStruct((bsz, 1, S5_LANES), F32)
    state_spec = pl.BlockSpec((1, 1, S5_LANES), lambda b, s: (b, 0, 0))
    return pl.pallas_call(
        _s5_seq_kernel,
        grid=(bsz, steps),
        in_specs=[pl.BlockSpec((1, S5_TILE, S5_WIDTH), lambda b, s: (0, b * steps + s, 0))] + _s5_weight_specs(),
        out_specs=(pl.BlockSpec((S5_TILE, S5_WIDTH), lambda b, s: (b * steps + s, 0)), state_spec, state_spec),
        out_shape=(jax.ShapeDtypeStruct((bsz * seq_len, S5_WIDTH), BF16), state, state),
        scratch_shapes=[pltpu.VMEM((S5_TILE, S5_LANES), F32), pltpu.VMEM((S5_TILE, S5_LANES), F32),
                        pltpu.VMEM((8, S5_LANES), F32), pltpu.VMEM((8, S5_LANES), F32)],
        compiler_params=_params("parallel", "arbitrary"),
        name="s5_sequences",
    )(z, *weights)


def _s5_step_kernel(u_ref, h0r_ref, h0i_ref, tab_ref, bbd_ref, cr_ref, ci_ref, d_ref, wg_ref, bg_ref,
                    o_ref, hr_ref, hi_ref, bur_ref, bui_ref):
    @pl.when(pl.program_id(0) == 0)
    def _():
        hr_ref[...] = h0r_ref[...]
        hi_ref[...] = h0i_ref[...]

    u = u_ref[0]
    _s5_drive(u.astype(BF16), bbd_ref, bur_ref, bui_ref)
    ar = tab_ref[6, 0:1, :]
    ai = tab_ref[7, 0:1, :]
    hr = hr_ref[...]
    hi = hi_ref[...]
    hr_ref[...] = ar * hr - ai * hi + bur_ref[...]
    hi_ref[...] = ar * hi + ai * hr + bui_ref[...]
    o_ref[0] = _s5_readout(u, hr_ref, hi_ref, cr_ref, ci_ref, d_ref, wg_ref, bg_ref)


def s5_steps(u, h0_re, h0_im, weights):
    steps, rows, _ = u.shape
    state = jax.ShapeDtypeStruct((rows, S5_LANES), F32)
    return pl.pallas_call(
        _s5_step_kernel,
        grid=(steps,),
        in_specs=[pl.BlockSpec((1, rows, S5_WIDTH), lambda t: (t, 0, 0)),
                  _whole((rows, S5_LANES)), _whole((rows, S5_LANES))] + _s5_weight_specs(),
        out_specs=(pl.BlockSpec((1, rows, S5_WIDTH), lambda t: (t, 0, 0)),
                   _whole((rows, S5_LANES)), _whole((rows, S5_LANES))),
        out_shape=(jax.ShapeDtypeStruct((steps, rows, S5_WIDTH), BF16), state, state),
        scratch_shapes=[pltpu.VMEM((rows, S5_LANES), F32), pltpu.VMEM((rows, S5_LANES), F32)],
        compiler_params=_params("arbitrary"),
        name="s5_steps",
    )(u, h0_re, h0_im, *weights)


def s5_weights(lam_re, lam_im, log_dt, b_re, b_im, c_re, c_im, d_skip, w_glu, b_glu):
    tab, bbr, bbi = s5_params(lam_re, lam_im, log_dt, b_re, b_im)
    return (tab, _block_diag_in(bbr, bbi), _block_diag_out(c_re), _block_diag_out(c_im),
            d_skip.reshape(1, S5_WIDTH), w_glu.astype(BF16), b_glu.reshape(1, S5_WIDTH))


def _lane_pad_bias(b_f):
    return jnp.pad(b_f, (0, LANES - FOX_HEADS)).reshape(1, LANES)


def _log_forget(zf_ref, bf_ref):
    lane = lax.broadcasted_iota(jnp.int32, zf_ref.shape, 1)
    return jnp.where(lane < FOX_HEADS, jax.nn.log_sigmoid(zf_ref[...] + bf_ref[...]), 0.0)


def _gate_seq_kernel(zf_ref, bf_ref, lf_ref, c_ref, ct_ref):
    lf = _log_forget(zf_ref, bf_ref)
    lf_ref[...] = lf
    row = lax.broadcasted_iota(jnp.int32, lf.shape, 0)
    c = lf
    s = 1
    while s < lf.shape[0]:
        c = c + jnp.where(row >= s, pltpu.roll(c, s, 0), 0.0)
        s *= 2
    c_ref[0] = c
    ct_ref[0] = c.T[:FOX_HEADS]


def fox_gates_seq(zf, b_f, bsz, seq_len):
    return pl.pallas_call(
        _gate_seq_kernel,
        grid=(bsz,),
        in_specs=[pl.BlockSpec((seq_len, LANES), lambda b: (b, 0)), _whole((1, LANES))],
        out_specs=(pl.BlockSpec((seq_len, LANES), lambda b: (b, 0)),
                   pl.BlockSpec((1, seq_len, LANES), lambda b: (b, 0, 0)),
                   pl.BlockSpec((1, FOX_HEADS, seq_len), lambda b: (b, 0, 0))),
        out_shape=(jax.ShapeDtypeStruct((bsz * seq_len, LANES), F32),
                   jax.ShapeDtypeStruct((bsz, seq_len, LANES), F32),
                   jax.ShapeDtypeStruct((bsz, FOX_HEADS, seq_len), F32)),
        compiler_params=_params("parallel"),
        name="fox_gates_seq",
    )(zf, _lane_pad_bias(b_f))


def _gate_kernel(zf_ref, bf_ref, lf_ref):
    lf_ref[...] = _log_forget(zf_ref, bf_ref)


def fox_gates(zf, b_f):
    return pl.pallas_call(
        _gate_kernel,
        out_shape=jax.ShapeDtypeStruct(zf.shape, F32),
        name="fox_gates",
    )(zf, _lane_pad_bias(b_f))


FOX_TILE = 1024
FOX_SCALE = FOX_HEAD_DIM ** -0.5


def _fox_seq_kernel(q_ref, k_ref, v_ref, c_ref, ct_ref, o_ref):
    h = pl.program_id(1)
    i = pl.program_id(2)
    q = q_ref[0].astype(BF16)
    lane = lax.broadcasted_iota(jnp.int32, c_ref.shape[1:], 1)
    cq = jnp.sum(jnp.where(lane == h, c_ref[0], 0.0), axis=1, keepdims=True)

    def chunk(j, carry, diagonal):
        m, l, acc = carry
        rows = pl.ds(pl.multiple_of(j * FOX_TILE, FOX_TILE), FOX_TILE)
        kc = k_ref[0, rows, :].astype(BF16)
        vc = v_ref[0, rows, :].astype(BF16)
        s = lax.dot_general(q, kc, NT_DIMS, preferred_element_type=F32) * FOX_SCALE
        s = s + cq - ct_ref[0, h, pl.ds(j, 1), :]
        if diagonal:
            qi = lax.broadcasted_iota(jnp.int32, s.shape, 0)
            ki = lax.broadcasted_iota(jnp.int32, s.shape, 1)
            s = jnp.where(qi >= ki, s, NEG_INF)
        m_new = jnp.maximum(m, jnp.max(s, axis=1, keepdims=True))
        alpha = jnp.exp(m - m_new)
        p = jnp.exp(s - m_new)
        l = alpha * l + jnp.sum(p, axis=1, keepdims=True)
        acc = alpha * acc + jnp.dot(p.astype(BF16), vc, preferred_element_type=F32)
        return m_new, l, acc

    init = (jnp.full((FOX_TILE, 1), NEG_INF, F32), jnp.zeros((FOX_TILE, 1), F32),
            jnp.zeros((FOX_TILE, FOX_HEAD_DIM), F32))
    carry = lax.fori_loop(0, i, functools.partial(chunk, diagonal=False), init)
    _, l, acc = chunk(i, carry, diagonal=True)
    o_ref[...] = (acc / l).astype(BF16)


def fox_sequences(z, c, ct, bsz, seq_len):
    nq = seq_len // FOX_TILE
    ct = ct.reshape(bsz, FOX_HEADS, nq, FOX_TILE)
    return pl.pallas_call(
        _fox_seq_kernel,
        grid=(bsz, FOX_HEADS, nq),
        in_specs=[
            pl.BlockSpec((1, FOX_TILE, FOX_HEAD_DIM), lambda b, h, i: (1, b * nq + i, h)),
            pl.BlockSpec((1, seq_len, FOX_HEAD_DIM), lambda b, h, i: (2, b, h)),
            pl.BlockSpec((1, seq_len, FOX_HEAD_DIM), lambda b, h, i: (3, b, h)),
            pl.BlockSpec((1, FOX_TILE, LANES), lambda b, h, i: (b, i, 0)),
            pl.BlockSpec((1, FOX_HEADS, nq, FOX_TILE), lambda b, h, i: (b, 0, 0, 0)),
        ],
        out_specs=pl.BlockSpec((FOX_TILE, FOX_HEAD_DIM), lambda b, h, i: (b * nq + i, h)),
        out_shape=jax.ShapeDtypeStruct((bsz * seq_len, FOX_WIDTH), BF16),
        compiler_params=_params("parallel", "parallel", "arbitrary"),
        name="fox_sequences",
    )(z, z, z, c, ct)


PAGE_ROWS = PAGE_SIZE * FOX_HEADS
NEW_ROWS = DEC_SEQ * FOX_HEADS


def _fox_paged_kernel(pt_ref, q_ref, kn_ref, vn_ref, ln_ref, *refs):
    k_pages = refs[:N_PAGES]
    v_pages = refs[N_PAGES:2 * N_PAGES]
    lf_pages = refs[2 * N_PAGES:3 * N_PAGES]
    o_ref, kpad_ref, vpad_ref = refs[3 * N_PAGES:]
    del pt_ref
    past = N_PAGES * PAGE_ROWS
    q = q_ref[0].astype(BF16)
    row = lax.broadcasted_iota(jnp.int32, (NEW_ROWS, 1), 0)
    row_head = jnp.bitwise_and(row, FOX_HEADS - 1)

    c = jnp.concatenate([r[0] for r in lf_pages], axis=1)
    lane = lax.broadcasted_iota(jnp.int32, c.shape, 1)
    s = FOX_HEADS
    while s < past:
        c = c + jnp.where(lane >= s, pltpu.roll(c, s, 1), 0.0)
        s *= 2
    tail_lane = lax.broadcasted_iota(jnp.int32, (NEW_ROWS, LANES), 1)
    tail = jnp.broadcast_to(c[:, past - LANES:], (NEW_ROWS, LANES))
    c_last = jnp.sum(jnp.where(tail_lane == LANES - FOX_HEADS + row_head, tail, 0.0), axis=1, keepdims=True)
    ln = ln_ref[0]
    blocks = [c_last[0:FOX_HEADS] + ln[0:FOX_HEADS]]
    for t in range(1, DEC_SEQ):
        blocks.append(blocks[-1] + ln[t * FOX_HEADS:(t + 1) * FOX_HEADS])
    c_new = jnp.concatenate(blocks, axis=0)
    c_new_row = jnp.sum(jnp.where(tail_lane == row, c_new, 0.0), axis=0, keepdims=True)

    kpad_ref[...] = jnp.zeros_like(kpad_ref)
    vpad_ref[...] = jnp.zeros_like(vpad_ref)
    kpad_ref[0:NEW_ROWS, :] = kn_ref[0]
    vpad_ref[0:NEW_ROWS, :] = vn_ref[0]

    s_past = jnp.concatenate(
        [lax.dot_general(q, kp[0].astype(BF16), NT_DIMS, preferred_element_type=F32) for kp in k_pages], axis=1)
    s_past = s_past * FOX_SCALE + c_new - c
    s_past = jnp.where(jnp.bitwise_and(lane, FOX_HEADS - 1) == row_head, s_past, NEG_INF)
    s_new = lax.dot_general(q, kpad_ref[...].astype(BF16), NT_DIMS, preferred_element_type=F32)
    s_new = s_new * FOX_SCALE + c_new - c_new_row
    visible = (jnp.bitwise_and(tail_lane, FOX_HEADS - 1) == row_head) & (tail_lane <= row)
    s_new = jnp.where(visible, s_new, NEG_INF)

    m = jnp.maximum(jnp.max(s_past, axis=1, keepdims=True), jnp.max(s_new, axis=1, keepdims=True))
    p_past = jnp.exp(s_past - m)
    p_new = jnp.exp(s_new - m)
    total = jnp.sum(p_past, axis=1, keepdims=True) + jnp.sum(p_new, axis=1, keepdims=True)
    p_past = p_past.astype(BF16)
    acc = jnp.dot(p_new.astype(BF16), vpad_ref[...].astype(BF16), preferred_element_type=F32)
    for j, vp in enumerate(v_pages):
        acc = acc + jnp.dot(p_past[:, j * PAGE_ROWS:(j + 1) * PAGE_ROWS], vp[0].astype(BF16),
                            preferred_element_type=F32)
    o_ref[0] = acc / total


def fox_paged(q, k_new, v_new, logf_new, cache_k, cache_v, cache_logf, page_table, first_page):
    bsz = q.shape[0]

    def page(j):
        return lambda b, pt: (first_page + pt[b, j], 0, 0)

    new_spec = pl.BlockSpec((1, NEW_ROWS, FOX_HEAD_DIM), lambda b, pt: (b, 0, 0))
    in_specs = [new_spec, new_spec, new_spec, pl.BlockSpec((1, NEW_ROWS, 1), lambda b, pt: (b, 0, 0))]
    in_specs += [pl.BlockSpec((1, PAGE_ROWS, FOX_HEAD_DIM), page(j)) for j in range(N_PAGES)]
    in_specs += [pl.BlockSpec((1, PAGE_ROWS, FOX_HEAD_DIM), page(j)) for j in range(N_PAGES)]
    in_specs += [pl.BlockSpec((1, 1, PAGE_ROWS), page(j)) for j in range(N_PAGES)]
    return pl.pallas_call(
        _fox_paged_kernel,
        grid_spec=pltpu.PrefetchScalarGridSpec(
            num_scalar_prefetch=1,
            grid=(bsz,),
            in_specs=in_specs,
            out_specs=new_spec,
            scratch_shapes=[pltpu.VMEM((LANES, FOX_HEAD_DIM), F32), pltpu.VMEM((LANES, FOX_HEAD_DIM), F32)],
        ),
        out_shape=jax.ShapeDtypeStruct((bsz, NEW_ROWS, FOX_HEAD_DIM), F32),
        compiler_params=_params("parallel"),
        name="fox_paged",
    )(page_table, q, k_new, v_new, logf_new,
      *([cache_k] * N_PAGES), *([cache_v] * N_PAGES), *([cache_logf] * N_PAGES))


ROPE_HALF = RET_KD // 2
RET_SCALE = RET_KD ** -0.5
RET_LOG_GAMMA = [float(np.log1p(-np.float32(2.0 ** (-5.0 - h)))) for h in range(RET_HEADS)]


def _rope_kernel(f_ref, cos_ref, sin_ref, *, start):
    pos = (start + lax.broadcasted_iota(jnp.int32, cos_ref.shape, 0)).astype(F32)
    ang = pos * f_ref[...]
    cos_ref[...] = jnp.cos(ang)
    sin_ref[...] = jnp.sin(ang)


def rope_table(start, n):
    inv_freq = ROPE_BASE ** (-jnp.arange(ROPE_HALF, dtype=F32) / ROPE_HALF)
    shape = jax.ShapeDtypeStruct((n, ROPE_HALF), F32)
    return pl.pallas_call(
        functools.partial(_rope_kernel, start=start), out_shape=(shape, shape), name="rope_table",
    )(inv_freq.reshape(1, ROPE_HALF))


def _rotate(x, cos, sin):
    x1, x2 = x[:, :ROPE_HALF], x[:, ROPE_HALF:]
    return jnp.concatenate([x1 * cos - x2 * sin, x1 * sin + x2 * cos], axis=1)


def _group_norm_gate(o, g, gain):
    mu = jnp.mean(o, axis=-1, keepdims=True)
    var = jnp.mean(jnp.square(o - mu), axis=-1, keepdims=True)
    return g * jax.nn.sigmoid(g) * ((o - mu) * lax.rsqrt(var + EPS) * gain)


def _head_slabs(refs, h, per_slab):
    width = PROJ_TILE // per_slab
    return refs[h // per_slab], slice((h % per_slab) * width, (h % per_slab + 1) * width)


def _ret_seq_kernel(*refs):
    q_refs, k_refs, v_refs, g_refs = refs[0:2], refs[2:4], refs[4:8], refs[8:12]
    cos_ref, sin_ref, gain_ref, o_ref, s_out, s_ref = refs[12:]
    step = pl.program_id(1)

    @pl.when(step == 0)
    def _():
        s_ref[...] = jnp.zeros_like(s_ref)

    cos = cos_ref[...]
    sin = sin_ref[...]
    cl = RET_CHUNK
    i_col = lax.broadcasted_iota(jnp.int32, (cl, 1), 0).astype(F32)
    diff = i_col - lax.broadcasted_iota(jnp.int32, (1, cl), 1).astype(F32)
    for h in range(RET_HEADS):
        lg = RET_LOG_GAMMA[h]
        ref, sl = _head_slabs(q_refs, h, 4)
        q = _rotate(ref[0, :, sl], cos, sin).astype(BF16)
        ref, sl = _head_slabs(k_refs, h, 4)
        k = _rotate(ref[0, :, sl], cos, sin) * RET_SCALE
        ref, sl = _head_slabs(v_refs, h, 2)
        v = ref[0, :, sl].astype(BF16)
        ref, sl = _head_slabs(g_refs, h, 2)
        g = ref[0, :, sl]
        dmask = jnp.where(diff >= 0, jnp.exp(jnp.maximum(diff, 0.0) * lg), 0.0)
        inner = lax.dot_general(q, k.astype(BF16), NT_DIMS, preferred_element_type=F32) * dmask
        state = s_ref[h]
        out = jnp.dot(inner.astype(BF16), v, preferred_element_type=F32)
        out = out + jnp.dot(q, state.astype(BF16), preferred_element_type=F32) * jnp.exp((i_col + 1.0) * lg)
        k_dec = k * jnp.exp((cl - 1.0 - i_col) * lg)
        s_ref[h] = math.exp(cl * lg) * state + jnp.dot(k_dec.T.astype(BF16), v, preferred_element_type=F32)
        hs = slice(h * RET_VD, (h + 1) * RET_VD)
        o_ref[:, hs] = _group_norm_gate(out, g, gain_ref[:, hs]).astype(BF16)

    @pl.when(step == pl.num_programs(1) - 1)
    def _():
        s_out[0] = s_ref[...]


def ret_sequences(z, cos, sin, gain, bsz, seq_len):
    steps = seq_len // RET_CHUNK
    vd = RET_HEADS * RET_VD
    slab = lambda j: pl.BlockSpec((1, RET_CHUNK, PROJ_TILE), lambda b, c: (j, b * steps + c, 0))
    table = pl.BlockSpec((RET_CHUNK, ROPE_HALF), lambda b, c: (c, 0))
    return pl.pallas_call(
        _ret_seq_kernel,
        grid=(bsz, steps),
        in_specs=[slab(j) for j in range(12)] + [table, table, _whole((1, vd))],
        out_specs=(pl.BlockSpec((RET_CHUNK, vd), lambda b, c: (b * steps + c, 0)),
                   pl.BlockSpec((1, RET_HEADS, RET_KD, RET_VD), lambda b, c: (b, 0, 0, 0))),
        out_shape=(jax.ShapeDtypeStruct((bsz * seq_len, vd), BF16),
                   jax.ShapeDtypeStruct((bsz, RET_HEADS, RET_KD, RET_VD), F32)),
        scratch_shapes=[pltpu.VMEM((RET_HEADS, RET_KD, RET_VD), F32)],
        compiler_params=_params("parallel", "arbitrary"),
        name="ret_sequences",
    )(*([z] * 12), cos, sin, gain.reshape(1, vd))


RET_PAD = 128


RET_STEP_SEQS = 8 // DEC_SEQ


def _ret_step_kernel(z_ref, s0_ref, cos_ref, sin_ref, gain_ref, o_ref, s_out, qpad, kpad, vpad, gpad):
    cl = DEC_SEQ
    for ref in (qpad, kpad, vpad, gpad):
        ref[...] = jnp.zeros_like(ref)
    cos = cos_ref[...]
    sin = sin_ref[...]
    i_col = lax.broadcasted_iota(jnp.int32, (RET_PAD, 1), 0).astype(F32)
    diff = i_col - lax.broadcasted_iota(jnp.int32, (1, RET_PAD), 1).astype(F32)
    valid = (diff >= 0) & (i_col < cl)
    for b in range(RET_STEP_SEQS):
        tok = slice(b * cl, (b + 1) * cl)
        outs = []
        for h in range(RET_HEADS):
            lg = RET_LOG_GAMMA[h]
            qpad[0:cl, :] = z_ref[h // 4, tok, (h % 4) * RET_KD:(h % 4 + 1) * RET_KD]
            kpad[0:cl, :] = z_ref[2 + h // 4, tok, (h % 4) * RET_KD:(h % 4 + 1) * RET_KD]
            vpad[0:cl, :] = z_ref[4 + h // 2, tok, (h % 2) * RET_VD:(h % 2 + 1) * RET_VD]
            gpad[0:cl, :] = z_ref[8 + h // 2, tok, (h % 2) * RET_VD:(h % 2 + 1) * RET_VD]
            q = _rotate(qpad[...], cos, sin).astype(BF16)
            k = _rotate(kpad[...], cos, sin) * RET_SCALE
            v = vpad[...].astype(BF16)
            dmask = jnp.where(valid, jnp.exp(jnp.maximum(diff, 0.0) * lg), 0.0)
            inner = lax.dot_general(q, k.astype(BF16), NT_DIMS, preferred_element_type=F32) * dmask
            state = s0_ref[b, h]
            out = jnp.dot(inner.astype(BF16), v, preferred_element_type=F32)
            out = out + jnp.dot(q, state.astype(BF16), preferred_element_type=F32) * jnp.exp((i_col + 1.0) * lg)
            k_dec = k * jnp.exp((cl - 1.0 - i_col) * lg)
            s_out[b, h] = math.exp(cl * lg) * state + jnp.dot(k_dec.T.astype(BF16), v, preferred_element_type=F32)
            hs = slice(h * RET_VD, (h + 1) * RET_VD)
            outs.append(_group_norm_gate(out[0:8], gpad[0:8, :], gain_ref[:, hs]))
        o_ref[tok, :] = jnp.concatenate(outs, axis=1)[0:cl]


def ret_steps(z, first_row, s0, cos, sin, gain):
    bsz = s0.shape[0]
    vd = RET_HEADS * RET_VD
    rows = RET_STEP_SEQS * DEC_SEQ
    state_spec = pl.BlockSpec((RET_STEP_SEQS, RET_HEADS, RET_KD, RET_VD), lambda i: (i, 0, 0, 0))
    return pl.pallas_call(
        _ret_step_kernel,
        grid=(bsz // RET_STEP_SEQS,),
        in_specs=[pl.BlockSpec((12, rows, PROJ_TILE), lambda i: (0, first_row // rows + i, 0)), state_spec,
                  _whole((RET_PAD, ROPE_HALF)), _whole((RET_PAD, ROPE_HALF)), _whole((1, vd))],
        out_specs=(pl.BlockSpec((rows, vd), lambda i: (i, 0)), state_spec),
        out_shape=(jax.ShapeDtypeStruct((bsz * DEC_SEQ, vd), F32),
                   jax.ShapeDtypeStruct(s0.shape, F32)),
        scratch_shapes=[pltpu.VMEM((RET_PAD, RET_KD), F32), pltpu.VMEM((RET_PAD, RET_KD), F32),
                        pltpu.VMEM((RET_PAD, RET_VD), F32), pltpu.VMEM((RET_PAD, RET_VD), F32)],
        compiler_params=_params("parallel"),
        name="ret_steps",
    )(z, s0, cos, sin, gain.reshape(1, vd))


def _even_mixer(h, gain, e, cache_k, cache_v, cache_logf, state_s5_re, state_s5_im, page_table, w_in, b_f,
                s5_w):
    z, zf = norm_proj(h, gain, w_in, 4 * PROJ_TILE, narrow=True)

    s5_p, hr_p, hi_p = s5_sequences(z, s5_w, BATCH, SEQ)
    u_s = jnp.stack([z[0, N_PROMPT + t::DEC_SEQ] for t in range(DEC_SEQ)])
    s5_s, hr_s, hi_s = s5_steps(u_s, state_s5_re.reshape(DEC_BATCH, S5_LANES),
                                state_s5_im.reshape(DEC_BATCH, S5_LANES), s5_w)
    s5_all = jnp.concatenate([s5_p, s5_s.transpose(1, 0, 2).reshape(N_SAMPLE, S5_WIDTH)], axis=0)

    lf_p, c_p, ct_p = fox_gates_seq(zf, b_f, BATCH, SEQ)
    lf_s = fox_gates(zf[N_PROMPT:], b_f)[:, :FOX_HEADS]
    k_p, v_p, q_s, k_s, v_s = head_rows(z)
    att_p = fox_sequences(z, c_p, ct_p, BATCH, SEQ)
    pool = cache_k.shape[1]
    new_rows = lambda t: t.reshape(DEC_BATCH, NEW_ROWS, FOX_HEAD_DIM)
    att_s = fox_paged(new_rows(q_s), new_rows(k_s), new_rows(v_s), lf_s.reshape(DEC_BATCH, NEW_ROWS, 1),
                      cache_k.reshape(-1, PAGE_ROWS, FOX_HEAD_DIM), cache_v.reshape(-1, PAGE_ROWS, FOX_HEAD_DIM),
                      cache_logf.reshape(-1, 1, PAGE_ROWS), page_table, e * pool)
    att_all = jnp.concatenate([att_p, att_s.reshape(N_SAMPLE, FOX_WIDTH).astype(BF16)], axis=0)

    heads = lambda t, b, s: t.reshape(b, s, FOX_HEADS, FOX_HEAD_DIM)
    grid = lambda t: t.reshape(-1, S5_GROUPS, S5_STATE)
    st_p = (heads(k_p, BATCH, SEQ), heads(v_p, BATCH, SEQ),
            lf_p[:, :FOX_HEADS].reshape(BATCH, SEQ, FOX_HEADS), grid(hr_p), grid(hi_p))
    st_s = (heads(k_s, DEC_BATCH, DEC_SEQ), heads(v_s, DEC_BATCH, DEC_SEQ),
            lf_s.reshape(DEC_BATCH, DEC_SEQ, FOX_HEADS), grid(hr_s), grid(hi_s))
    return [s5_all, att_all], st_p, st_s


def _odd_mixer(h, gain, w_in, gn_gain, state, tables):
    z = norm_proj(h, gain, w_in, 12 * PROJ_TILE)
    (cos_p, sin_p), (cos_s, sin_s) = tables
    o_p, st_p = ret_sequences(z, cos_p, sin_p, gn_gain, BATCH, SEQ)
    o_s, st_s = ret_steps(z, N_PROMPT, state, cos_s, sin_s, gn_gain)
    o_all = jnp.concatenate([o_p, o_s.astype(BF16)], axis=0)
    return [o_all], st_p, st_s


def kernel(x_prompt, x_sample, cache_k, cache_v, cache_logf, state_s5_re, state_s5_im, state_ret, page_table, norm_gain, w_ffn_gate, w_ffn_up, w_ffn_down, w_in_even, b_forget, s5_lam_re, s5_lam_im, s5_b_re, s5_b_im, s5_c_re, s5_c_im, s5_d, s5_log_dt, w_glu, b_glu, w_out_even, w_in_odd, ret_gn_gain, w_out_odd, final_norm):
    x = jnp.concatenate([x_prompt.reshape(N_PROMPT, D_MODEL), x_sample.reshape(N_SAMPLE, D_MODEL)], axis=0)
    tables = (rope_table(0, SEQ), rope_table(PAST_LEN, RET_PAD))
    ev_p, ev_s, od_p, od_s = [], [], [], []
    for layer in range(DEPTH):
        h = ffn(x, norm_gain[layer, 0], w_ffn_gate, w_ffn_up, w_ffn_down, layer, 0)
        if layer % 2 == 0:
            e = layer // 2
            s5_w = s5_weights(s5_lam_re[e], s5_lam_im[e], s5_log_dt[e], s5_b_re[e], s5_b_im[e], s5_c_re[e],
                              s5_c_im[e], s5_d[e], w_glu[e], b_glu[e])
            parts, st_p, st_s = _even_mixer(h, norm_gain[layer, 1], e, cache_k, cache_v, cache_logf,
                                            state_s5_re[e], state_s5_im[e], page_table, w_in_even[e],
                                            b_forget[e], s5_w)
            ev_p.append(st_p)
            ev_s.append(st_s)
            w_out = w_out_even[e]
        else:
            o = layer // 2
            parts, st_p, st_s = _odd_mixer(h, norm_gain[layer, 1], w_in_odd[o], ret_gn_gain[o], state_ret[o],
                                           tables)
            od_p.append(st_p)
            od_s.append(st_s)
            w_out = w_out_odd[o]
        h = out_proj(parts, w_out, h)
        x = ffn(h, norm_gain[layer, 2], w_ffn_gate, w_ffn_up, w_ffn_down, layer, 1)

    y_prompt, y_sample = final_rms(x, final_norm)
    y_prompt = y_prompt.reshape(BATCH, SEQ, D_MODEL)
    y_sample = y_sample.reshape(DEC_BATCH, DEC_SEQ, D_MODEL)
    stack = lambda sts, i: jnp.stack([s[i] for s in sts])
    return (y_prompt, y_sample,
            stack(ev_p, 0), stack(ev_p, 1), stack(ev_p, 2), stack(ev_p, 3), stack(ev_p, 4), jnp.stack(od_p),
            stack(ev_s, 0), stack(ev_s, 1), stack(ev_s, 2), stack(ev_s, 3), stack(ev_s, 4), jnp.stack(od_s))
```

```python
import functools
import math

import jax
import jax.numpy as jnp
import numpy as np
from jax import lax
from jax.experimental import pallas as pl
from jax.experimental.pallas import tpu as pltpu

D_MODEL = 2048
BATCH = 2
SEQ = 4096
DEPTH = 2
DEC_BATCH = 128
DEC_SEQ = 4
PAST_LEN = 2048
PAGE_SIZE = 128
N_PAGES = PAST_LEN // PAGE_SIZE
S5_WIDTH = 1024
S5_GROUP = 16
S5_GROUPS = 64
S5_STATE = 64
FOX_WIDTH = 1024
FOX_HEAD_DIM = 128
FOX_HEADS = 8
NEG_INF = -1e30
RET_HEADS = 8
RET_KD = 256
RET_VD = 512
RET_CHUNK = 128
ROPE_BASE = 10000.0
D_FF = 5504
EPS = 1e-6

N_PROMPT = BATCH * SEQ
N_SAMPLE = DEC_BATCH * DEC_SEQ
N_TOK = N_PROMPT + N_SAMPLE

ROW_TILE = 1088
FF_TILE = 256
LANES = 128
VMEM_LIMIT = 56 * 1024 * 1024
FFN_VMEM_LIMIT = 60 * 1024 * 1024

F32 = jnp.float32
BF16 = jnp.bfloat16
NT_DIMS = (((1,), (1,)), ((), ()))


def _rms_rows(x, gain):
    return x * lax.rsqrt(jnp.mean(x * x, axis=-1, keepdims=True) + EPS) * gain


def _whole(shape):
    return pl.BlockSpec(shape, lambda *_: (0,) * len(shape))


def _params(*semantics):
    return pltpu.CompilerParams(dimension_semantics=semantics, vmem_limit_bytes=VMEM_LIMIT)


FFN_GROUP = 2


def _ffn_kernel(x_ref, g_ref, wg_ref, wu_ref, wd_ref, o_ref, h_ref, *, d_ff):
    step = pl.program_id(1)
    tf = wg_ref.shape[1]

    @pl.when(step < FFN_GROUP)
    def _():
        rows = pl.ds(pl.multiple_of(step * ROW_TILE, ROW_TILE), ROW_TILE)
        x = x_ref[...]
        h_ref[rows, :] = _rms_rows(x, g_ref[...]).astype(BF16)
        o_ref[rows, :] = x

    @pl.when(step >= FFN_GROUP)
    def _():
        first = (step - FFN_GROUP) * tf
        wg = wg_ref[...].astype(BF16)
        wu = wu_ref[...].astype(BF16)
        col_ok = first + lax.broadcasted_iota(jnp.int32, (1, tf), 1) < d_ff
        row_ok = first + lax.broadcasted_iota(jnp.int32, (tf, 1), 0) < d_ff
        wd = jnp.where(row_ok, wd_ref[...], 0.0).astype(BF16)
        h = h_ref[...]
        a = jnp.dot(h, wg, preferred_element_type=F32)
        b = jnp.dot(h, wu, preferred_element_type=F32)
        c = jnp.where(col_ok, a * jax.nn.sigmoid(a) * b, 0.0).astype(BF16)
        o_ref[...] += 0.5 * jnp.dot(c, wd, preferred_element_type=F32)


def ffn(x, gain, wg, wu, wd, layer, which):
    m, d = x.shape
    d_ff = wg.shape[3]
    group_rows = FFN_GROUP * ROW_TILE
    w_tile = lambda r, s: jnp.maximum(s - FFN_GROUP, 0)
    return pl.pallas_call(
        functools.partial(_ffn_kernel, d_ff=d_ff),
        grid=(m // group_rows, FFN_GROUP + pl.cdiv(d_ff, FF_TILE)),
        in_specs=[
            pl.BlockSpec((ROW_TILE, d), lambda r, s: (FFN_GROUP * r + jnp.minimum(s, FFN_GROUP - 1), 0),
                         pipeline_mode=pl.Buffered(1)),
            pl.BlockSpec((1, d), lambda r, s: (0, 0)),
            pl.BlockSpec((None, None, d, FF_TILE), lambda r, s: (layer, which, 0, w_tile(r, s))),
            pl.BlockSpec((None, None, d, FF_TILE), lambda r, s: (layer, which, 0, w_tile(r, s))),
            pl.BlockSpec((None, None, FF_TILE, d), lambda r, s: (layer, which, w_tile(r, s), 0)),
        ],
        out_specs=pl.BlockSpec((group_rows, d), lambda r, s: (r, 0), pipeline_mode=pl.Buffered(1)),
        out_shape=jax.ShapeDtypeStruct((m, d), F32),
        scratch_shapes=[pltpu.VMEM((group_rows, d), BF16)],
        compiler_params=pltpu.CompilerParams(
            dimension_semantics=("parallel", "arbitrary"), vmem_limit_bytes=FFN_VMEM_LIMIT),
        name="ffn",
    )(x, gain.reshape(1, d), wg, wu, wd)


SPLIT_TILE = 512
N_PROMPT_TILES = N_PROMPT // SPLIT_TILE


def _split_specs(block):
    zeros = (0,) * (len(block) - 1)
    return (pl.BlockSpec(block, lambda i: (jnp.minimum(i, N_PROMPT_TILES - 1),) + zeros),
            pl.BlockSpec(block, lambda i: (0,) + zeros))


def _final_norm_kernel(x_ref, g_ref, yp_ref, ys_ref):
    y = _rms_rows(x_ref[...], g_ref[...])

    @pl.when(pl.program_id(0) < N_PROMPT_TILES)
    def _():
        yp_ref[...] = y

    @pl.when(pl.program_id(0) == N_PROMPT_TILES)
    def _():
        ys_ref[...] = y


def final_rms(x, gain):
    d = x.shape[1]
    return pl.pallas_call(
        _final_norm_kernel,
        grid=(N_PROMPT_TILES + 1,),
        in_specs=[pl.BlockSpec((SPLIT_TILE, d), lambda i: (i, 0)), _whole((1, d))],
        out_specs=_split_specs((SPLIT_TILE, d)),
        out_shape=(jax.ShapeDtypeStruct((N_PROMPT, d), F32), jax.ShapeDtypeStruct((N_SAMPLE, d), F32)),
        compiler_params=_params("arbitrary"),
        name="final_norm",
    )(x, gain.reshape(1, d))


def _head_rows_kernel(q_ref, k_ref, v_ref, kp_ref, vp_ref, qs_ref, ks_ref, vs_ref):
    def put(pairs):
        for src, dst in pairs:
            for h in range(FOX_HEADS):
                dst[:, h, :] = src[0, :, h * FOX_HEAD_DIM:(h + 1) * FOX_HEAD_DIM]

    @pl.when(pl.program_id(0) < N_PROMPT_TILES)
    def _():
        put([(k_ref, kp_ref), (v_ref, vp_ref)])

    @pl.when(pl.program_id(0) == N_PROMPT_TILES)
    def _():
        put([(q_ref, qs_ref), (k_ref, ks_ref), (v_ref, vs_ref)])


def head_rows(z):
    block = (SPLIT_TILE, FOX_HEADS, FOX_HEAD_DIM)
    shape = lambda n: jax.ShapeDtypeStruct((n, FOX_HEADS, FOX_HEAD_DIM), F32)
    kp_spec, ks_spec = _split_specs(block)
    return pl.pallas_call(
        _head_rows_kernel,
        grid=(N_PROMPT_TILES + 1,),
        in_specs=[pl.BlockSpec((1, SPLIT_TILE, FOX_WIDTH), lambda i: (1, N_PROMPT_TILES, 0)),
                  pl.BlockSpec((1, SPLIT_TILE, FOX_WIDTH), lambda i: (2, i, 0)),
                  pl.BlockSpec((1, SPLIT_TILE, FOX_WIDTH), lambda i: (3, i, 0))],
        out_specs=(kp_spec, kp_spec, ks_spec, ks_spec, ks_spec),
        out_shape=(shape(N_PROMPT), shape(N_PROMPT), shape(N_SAMPLE), shape(N_SAMPLE), shape(N_SAMPLE)),
        compiler_params=_params("arbitrary"),
        name="head_rows",
    )(z, z, z)


PROJ_TILE = 1024


def _norm_proj_kernel(x_ref, g_ref, w_ref, *rest, narrow):
    if narrow:
        wn_ref, z_ref, zn_ref, h_ref = rest
    else:
        z_ref, h_ref = rest
    j = pl.program_id(1)

    @pl.when(j == 0)
    def _():
        h_ref[...] = _rms_rows(x_ref[...], g_ref[...]).astype(BF16)
        if narrow:
            zn_ref[...] = jnp.dot(h_ref[...], wn_ref[...].astype(BF16), preferred_element_type=F32)

    z_ref[0] = jnp.dot(h_ref[...], w_ref[...].astype(BF16), preferred_element_type=F32)


def norm_proj(x, gain, w, n_wide, narrow=False):
    m, d = x.shape
    nj = n_wide // PROJ_TILE
    in_specs = [
        pl.BlockSpec((ROW_TILE, d), lambda i, j: (i, 0), pipeline_mode=pl.Buffered(1)),
        pl.BlockSpec((1, d), lambda i, j: (0, 0)),
        pl.BlockSpec((d, PROJ_TILE), lambda i, j: (0, j)),
    ]
    args = [x, gain.reshape(1, d), w]
    out_shape = jax.ShapeDtypeStruct((nj, m, PROJ_TILE), F32)
    out_specs = pl.BlockSpec((1, ROW_TILE, PROJ_TILE), lambda i, j: (j, i, 0))
    if narrow:
        in_specs.append(pl.BlockSpec((d, LANES), lambda i, j: (0, n_wide // LANES)))
        args.append(w)
        out_shape = (out_shape, jax.ShapeDtypeStruct((m, LANES), F32))
        out_specs = (out_specs, pl.BlockSpec((ROW_TILE, LANES), lambda i, j: (i, 0)))
    return pl.pallas_call(
        functools.partial(_norm_proj_kernel, narrow=narrow),
        grid=(m // ROW_TILE, nj),
        in_specs=in_specs,
        out_specs=out_specs,
        out_shape=out_shape,
        scratch_shapes=[pltpu.VMEM((ROW_TILE, d), BF16)],
        compiler_params=_params("parallel", "arbitrary"),
        name="norm_proj_narrow" if narrow else "norm_proj",
    )(*args)


OUT_TILE = 512


def _out_proj_kernel(*refs, n_parts):
    a_refs, w_refs, r_ref, o_ref = refs[:n_parts], refs[n_parts:2 * n_parts], refs[2 * n_parts], refs[-1]
    acc = r_ref[...]
    for a_ref, w_ref in zip(a_refs, w_refs):
        acc = acc + jnp.dot(a_ref[...].astype(BF16), w_ref[...].astype(BF16), preferred_element_type=F32)
    o_ref[...] = acc


def _out_proj_rows(parts, w, res, row_tile, first_tile, prev=None):
    rows, k = parts[0].shape
    n = w.shape[1]
    n_parts = len(parts)
    out_block = pl.BlockSpec((row_tile, OUT_TILE), lambda i, j: (first_tile + i, j))
    in_specs = ([pl.BlockSpec((row_tile, k), lambda i, j: (i, 0))] * n_parts
                + [pl.BlockSpec((k, OUT_TILE), functools.partial(lambda i, j, p: (p, j), p=p))
                   for p in range(n_parts)]
                + [out_block])
    args = [*parts, *([w] * n_parts), res]
    aliases = {}
    if prev is not None:
        in_specs.append(pl.BlockSpec(memory_space=pl.ANY))
        args.append(prev)
        aliases = {len(args) - 1: 0}
    return pl.pallas_call(
        functools.partial(_out_proj_kernel, n_parts=n_parts),
        grid=(rows // row_tile, n // OUT_TILE),
        in_specs=in_specs,
        out_specs=out_block,
        out_shape=jax.ShapeDtypeStruct((res.shape[0], n), F32),
        input_output_aliases=aliases,
        compiler_params=_params("parallel", "arbitrary"),
        name="out_proj",
    )(*args)


OUT_ROWS = 1024


def out_proj(parts_prompt, parts_sample, w, res):
    out = _out_proj_rows(parts_prompt, w, res, OUT_ROWS, 0)
    return _out_proj_rows(parts_sample, w, res, N_SAMPLE, N_PROMPT // N_SAMPLE, prev=out)


S5_LANES = S5_GROUPS * S5_STATE
S5_BLOCKS = 4
S5_BLOCK_CH = S5_WIDTH // S5_BLOCKS
S5_BLOCK_ST = S5_LANES // S5_BLOCKS
SCAN_LANES = 512
S5_TILE = 256


def _s5_param_kernel(lr_ref, li_ref, ldt_ref, bre_ref, bim_ref, tab_ref, bbr_ref, bbi_ref):
    lr = lr_ref[...]
    li = li_ref[...]
    dt = jnp.exp(ldt_ref[...])
    dec = lr * dt
    ang = li * dt
    er = jnp.exp(dec)
    xr = er * jnp.cos(ang) - 1.0
    xi = er * jnp.sin(ang)
    den = lr * lr + li * li
    cr = (xr * lr + xi * li) / den
    ci = (xi * lr - xr * li) / den
    bre = bre_ref[...]
    bim = bim_ref[...]
    bbr_ref[...] = cr * bre - ci * bim
    bbi_ref[...] = cr * bim + ci * bre

    row = lax.broadcasted_iota(jnp.int32, (8, S5_LANES), 0)

    def power(k):
        mag = jnp.exp(k * dec)
        return mag * jnp.cos(k * ang), mag * jnp.sin(k * ang)

    for idx, s in enumerate((1, 2, 4)):
        pr, pi = power(float(s))
        tab_ref[2 * idx] = jnp.where(row >= s, pr, 0.0)
        tab_ref[2 * idx + 1] = jnp.where(row >= s, pi, 0.0)
    pr, pi = power((row + 1).astype(F32))
    tab_ref[6] = pr
    tab_ref[7] = pi


def s5_params(lam_re, lam_im, log_dt, b_re, b_im):
    flat = lambda t: t.reshape(1, S5_LANES)
    ldt = jnp.broadcast_to(log_dt[:, None], (S5_GROUPS, S5_STATE))
    to_rows = lambda t: t.reshape(S5_LANES, S5_GROUP).T
    return pl.pallas_call(
        _s5_param_kernel,
        out_shape=(jax.ShapeDtypeStruct((8, 8, S5_LANES), F32),
                   jax.ShapeDtypeStruct((S5_GROUP, S5_LANES), F32),
                   jax.ShapeDtypeStruct((S5_GROUP, S5_LANES), F32)),
        name="s5_params",
    )(flat(lam_re), flat(lam_im), flat(ldt), to_rows(b_re), to_rows(b_im))


def _block_diag_in(bbr, bbi):
    bb = jnp.stack([bbr, bbi]).reshape(2, S5_GROUP, S5_BLOCKS, 16, S5_STATE)
    eye = jnp.eye(16, dtype=F32)
    t = bb[:, :, :, :, None, :] * eye[None, None, None, :, :, None]
    return t.transpose(2, 3, 1, 0, 4, 5).reshape(S5_BLOCKS, S5_BLOCK_CH, 2 * S5_BLOCK_ST).astype(BF16)


def _block_diag_out(c):
    cc = c.reshape(S5_BLOCKS, 16, S5_GROUP, S5_STATE)
    eye = jnp.eye(16, dtype=F32)
    t = cc.transpose(0, 1, 3, 2)[:, :, :, None, :] * eye[None, :, None, :, None]
    return t.reshape(S5_BLOCKS, S5_BLOCK_ST, S5_BLOCK_CH).astype(BF16)


def _s5_drive(ub, bbd_ref, hr_ref, hi_ref):
    for blk in range(S5_BLOCKS):
        bu = jnp.dot(ub[:, blk * S5_BLOCK_CH:(blk + 1) * S5_BLOCK_CH], bbd_ref[blk], preferred_element_type=F32)
        hr_ref[:, blk * S5_BLOCK_ST:(blk + 1) * S5_BLOCK_ST] = bu[:, :S5_BLOCK_ST]
        hi_ref[:, blk * S5_BLOCK_ST:(blk + 1) * S5_BLOCK_ST] = bu[:, S5_BLOCK_ST:]


def _s5_readout(u, hr_ref, hi_ref, cr_ref, ci_ref, d_ref, wg_ref, bg_ref):
    ys = []
    for blk in range(S5_BLOCKS):
        sl = slice(blk * S5_BLOCK_ST, (blk + 1) * S5_BLOCK_ST)
        yr = jnp.dot(hr_ref[:, sl].astype(BF16), cr_ref[blk], preferred_element_type=F32)
        yi = jnp.dot(hi_ref[:, sl].astype(BF16), ci_ref[blk], preferred_element_type=F32)
        ys.append(yr - yi)
    y = jax.nn.gelu(jnp.concatenate(ys, axis=1) + d_ref[...] * u)
    gate = jax.nn.sigmoid(jnp.dot(y.astype(BF16), wg_ref[...], preferred_element_type=F32) + bg_ref[...])
    return (y * gate).astype(BF16)


def _s5_seq_kernel(u_ref, tab_ref, bbd_ref, cr_ref, ci_ref, d_ref, wg_ref, bg_ref,
                   o_ref, hr_out, hi_out, hr_ref, hi_ref, car_r, car_i):
    step = pl.program_id(1)

    @pl.when(step == 0)
    def _():
        car_r[...] = jnp.zeros_like(car_r)
        car_i[...] = jnp.zeros_like(car_i)

    u = u_ref[0]
    _s5_drive(u.astype(BF16), bbd_ref, hr_ref, hi_ref)

    n_tiles = u.shape[0] // 8
    for c in range(S5_LANES // SCAN_LANES):
        sl = slice(c * SCAN_LANES, (c + 1) * SCAN_LANES)
        tabs = [tab_ref[k, :, sl] for k in range(8)]

        def tile(i, carry, sl=sl, tabs=tabs):
            cr, ci = carry
            rows = pl.ds(pl.multiple_of(i * 8, 8), 8)
            xr = hr_ref[rows, sl]
            xi = hi_ref[rows, sl]
            for k, s in enumerate((1, 2, 4)):
                ar, ai = tabs[2 * k], tabs[2 * k + 1]
                rr = pltpu.roll(xr, s, 0)
                ri = pltpu.roll(xi, s, 0)
                xr, xi = xr + ar * rr - ai * ri, xi + ar * ri + ai * rr
            pr, pi = tabs[6], tabs[7]
            hr = xr + pr * cr - pi * ci
            hi = xi + pr * ci + pi * cr
            hr_ref[rows, sl] = hr
            hi_ref[rows, sl] = hi
            return (jnp.broadcast_to(hr[7:8], hr.shape), jnp.broadcast_to(hi[7:8], hi.shape))

        cr, ci = lax.fori_loop(0, n_tiles, tile, (car_r[:, sl], car_i[:, sl]), unroll=2)
        car_r[:, sl] = cr
        car_i[:, sl] = ci

    o_ref[...] = _s5_readout(u, hr_ref, hi_ref, cr_ref, ci_ref, d_ref, wg_ref, bg_ref)

    @pl.when(step == pl.num_programs(1) - 1)
    def _():
        hr_out[0] = car_r[0:1, :]
        hi_out[0] = car_i[0:1, :]


def _s5_weight_specs():
    return [_whole((8, 8, S5_LANES)), _whole((S5_BLOCKS, S5_BLOCK_CH, 2 * S5_BLOCK_ST)),
            _whole((S5_BLOCKS, S5_BLOCK_ST, S5_BLOCK_CH)), _whole((S5_BLOCKS, S5_BLOCK_ST, S5_BLOCK_CH)),
            _whole((1, S5_WIDTH)), _whole((S5_WIDTH, S5_WIDTH)), _whole((1, S5_WIDTH))]


def s5_sequences(z, weights, bsz, seq_len):
    steps = seq_len // S5_TILE
    state = jax.ShapeDtypeStruct((bsz, 1, S5_LANES), F32)
    state_spec = pl.BlockSpec((1, 1, S5_LANES), lambda b, s: (b, 0, 0))
    return pl.pallas_call(
        _s5_seq_kernel,
        grid=(bsz, steps),
        in_specs=[pl.BlockSpec((1, S5_TILE, S5_WIDTH), lambda b, s: (0, b * steps + s, 0))] + _s5_weight_specs(),
        out_specs=(pl.BlockSpec((S5_TILE, S5_WIDTH), lambda b, s: (b * steps + s, 0)), state_spec, state_spec),
        out_shape=(jax.ShapeDtypeStruct((bsz * seq_len, S5_WIDTH), BF16), state, state),
        scratch_shapes=[pltpu.VMEM((S5_TILE, S5_LANES), F32), pltpu.VMEM((S5_TILE, S5_LANES), F32),
                        pltpu.VMEM((8, S5_LANES), F32), pltpu.VMEM((8, S5_LANES), F32)],
        compiler_params=_params("parallel", "arbitrary"),
        name="s5_sequences",
    )(z, *weights)


def _s5_step_kernel(u_ref, h0r_ref, h0i_ref, tab_ref, bbd_ref, cr_ref, ci_ref, d_ref, wg_ref, bg_ref,
                    o_ref, hr_ref, hi_ref, bur_ref, bui_ref):
    @pl.when(pl.program_id(0) == 0)
    def _():
        hr_ref[...] = h0r_ref[...]
        hi_ref[...] = h0i_ref[...]

    u = u_ref[0]
    _s5_drive(u.astype(BF16), bbd_ref, bur_ref, bui_ref)
    ar = tab_ref[6, 0:1, :]
    ai = tab_ref[7, 0:1, :]
    hr = hr_ref[...]
    hi = hi_ref[...]
    hr_ref[...] = ar * hr - ai * hi + bur_ref[...]
    hi_ref[...] = ar * hi + ai * hr + bui_ref[...]
    o_ref[0] = _s5_readout(u, hr_ref, hi_ref, cr_ref, ci_ref, d_ref, wg_ref, bg_ref)


def s5_steps(u, h0_re, h0_im, weights):
    steps, rows, _ = u.shape
    state = jax.ShapeDtypeStruct((rows, S5_LANES), F32)
    return pl.pallas_call(
        _s5_step_kernel,
        grid=(steps,),
        in_specs=[pl.BlockSpec((1, rows, S5_WIDTH), lambda t: (t, 0, 0)),
                  _whole((rows, S5_LANES)), _whole((rows, S5_LANES))] + _s5_weight_specs(),
        out_specs=(pl.BlockSpec((1, rows, S5_WIDTH), lambda t: (t, 0, 0)),
                   _whole((rows, S5_LANES)), _whole((rows, S5_LANES))),
        out_shape=(jax.ShapeDtypeStruct((steps, rows, S5_WIDTH), BF16), state, state),
        scratch_shapes=[pltpu.VMEM((rows, S5_LANES), F32), pltpu.VMEM((rows, S5_LANES), F32)],
        compiler_params=_params("arbitrary"),
        name="s5_steps",
    )(u, h0_re, h0_im, *weights)


def s5_weights(lam_re, lam_im, log_dt, b_re, b_im, c_re, c_im, d_skip, w_glu, b_glu):
    tab, bbr, bbi = s5_params(lam_re, lam_im, log_dt, b_re, b_im)
    return (tab, _block_diag_in(bbr, bbi), _block_diag_out(c_re), _block_diag_out(c_im),
            d_skip.reshape(1, S5_WIDTH), w_glu.astype(BF16), b_glu.reshape(1, S5_WIDTH))


def _lane_pad_bias(b_f):
    return jnp.pad(b_f, (0, LANES - FOX_HEADS)).reshape(1, LANES)


def _log_forget(zf_ref, bf_ref):
    lane = lax.broadcasted_iota(jnp.int32, zf_ref.shape, 1)
    return jnp.where(lane < FOX_HEADS, jax.nn.log_sigmoid(zf_ref[...] + bf_ref[...]), 0.0)


def _gate_seq_kernel(zf_ref, bf_ref, lf_ref, c_ref, ct_ref):
    lf = _log_forget(zf_ref, bf_ref)
    lf_ref[...] = lf
    row = lax.broadcasted_iota(jnp.int32, lf.shape, 0)
    c = lf
    s = 1
    while s < lf.shape[0]:
        c = c + jnp.where(row >= s, pltpu.roll(c, s, 0), 0.0)
        s *= 2
    c_ref[0] = c
    ct_ref[0] = c.T[:FOX_HEADS]


def fox_gates_seq(zf, b_f, bsz, seq_len):
    return pl.pallas_call(
        _gate_seq_kernel,
        grid=(bsz,),
        in_specs=[pl.BlockSpec((seq_len, LANES), lambda b: (b, 0)), _whole((1, LANES))],
        out_specs=(pl.BlockSpec((seq_len, LANES), lambda b: (b, 0)),
                   pl.BlockSpec((1, seq_len, LANES), lambda b: (b, 0, 0)),
                   pl.BlockSpec((1, FOX_HEADS, seq_len), lambda b: (b, 0, 0))),
        out_shape=(jax.ShapeDtypeStruct((bsz * seq_len, LANES), F32),
                   jax.ShapeDtypeStruct((bsz, seq_len, LANES), F32),
                   jax.ShapeDtypeStruct((bsz, FOX_HEADS, seq_len), F32)),
        compiler_params=_params("parallel"),
        name="fox_gates_seq",
    )(zf, _lane_pad_bias(b_f))


def _gate_kernel(zf_ref, bf_ref, lf_ref):
    lf_ref[...] = _log_forget(zf_ref, bf_ref)


def fox_gates(zf, b_f):
    return pl.pallas_call(
        _gate_kernel,
        out_shape=jax.ShapeDtypeStruct(zf.shape, F32),
        name="fox_gates",
    )(zf, _lane_pad_bias(b_f))


FOX_TILE = 1024
FOX_SCALE = FOX_HEAD_DIM ** -0.5


def _fox_seq_kernel(q_ref, k_ref, v_ref, c_ref, ct_ref, o_ref):
    h = pl.program_id(1)
    i = pl.program_id(2)
    q = q_ref[0].astype(BF16)
    lane = lax.broadcasted_iota(jnp.int32, c_ref.shape[1:], 1)
    cq = jnp.sum(jnp.where(lane == h, c_ref[0], 0.0), axis=1, keepdims=True)

    def chunk(j, carry, diagonal):
        m, l, acc = carry
        rows = pl.ds(pl.multiple_of(j * FOX_TILE, FOX_TILE), FOX_TILE)
        kc = k_ref[0, rows, :].astype(BF16)
        vc = v_ref[0, rows, :].astype(BF16)
        s = lax.dot_general(q, kc, NT_DIMS, preferred_element_type=F32) * FOX_SCALE
        s = s + cq - ct_ref[0, h, pl.ds(j, 1), :]
        if diagonal:
            qi = lax.broadcasted_iota(jnp.int32, s.shape, 0)
            ki = lax.broadcasted_iota(jnp.int32, s.shape, 1)
            s = jnp.where(qi >= ki, s, NEG_INF)
        m_new = jnp.maximum(m, jnp.max(s, axis=1, keepdims=True))
        alpha = jnp.exp(m - m_new)
        p = jnp.exp(s - m_new)
        l = alpha * l + jnp.sum(p, axis=1, keepdims=True)
        acc = alpha * acc + jnp.dot(p.astype(BF16), vc, preferred_element_type=F32)
        return m_new, l, acc

    init = (jnp.full((FOX_TILE, 1), NEG_INF, F32), jnp.zeros((FOX_TILE, 1), F32),
            jnp.zeros((FOX_TILE, FOX_HEAD_DIM), F32))
    carry = lax.fori_loop(0, i, functools.partial(chunk, diagonal=False), init)
    _, l, acc = chunk(i, carry, diagonal=True)
    o_ref[...] = (acc / l).astype(BF16)


def fox_sequences(z, c, ct, bsz, seq_len):
    nq = seq_len // FOX_TILE
    ct = ct.reshape(bsz, FOX_HEADS, nq, FOX_TILE)
    return pl.pallas_call(
        _fox_seq_kernel,
        grid=(bsz, FOX_HEADS, nq),
        in_specs=[
            pl.BlockSpec((1, FOX_TILE, FOX_HEAD_DIM), lambda b, h, i: (1, b * nq + i, h)),
            pl.BlockSpec((1, seq_len, FOX_HEAD_DIM), lambda b, h, i: (2, b, h)),
            pl.BlockSpec((1, seq_len, FOX_HEAD_DIM), lambda b, h, i: (3, b, h)),
            pl.BlockSpec((1, FOX_TILE, LANES), lambda b, h, i: (b, i, 0)),
            pl.BlockSpec((1, FOX_HEADS, nq, FOX_TILE), lambda b, h, i: (b, 0, 0, 0)),
        ],
        out_specs=pl.BlockSpec((FOX_TILE, FOX_HEAD_DIM), lambda b, h, i: (b * nq + i, h)),
        out_shape=jax.ShapeDtypeStruct((bsz * seq_len, FOX_WIDTH), BF16),
        compiler_params=_params("parallel", "parallel", "arbitrary"),
        name="fox_sequences",
    )(z, z, z, c, ct)


PAGE_ROWS = PAGE_SIZE * FOX_HEADS
NEW_ROWS = DEC_SEQ * FOX_HEADS


def _fox_paged_kernel(pt_ref, q_ref, kn_ref, vn_ref, ln_ref, *refs):
    k_pages = refs[:N_PAGES]
    v_pages = refs[N_PAGES:2 * N_PAGES]
    lf_pages = refs[2 * N_PAGES:3 * N_PAGES]
    o_ref, kpad_ref, vpad_ref = refs[3 * N_PAGES:]
    del pt_ref
    past = N_PAGES * PAGE_ROWS
    q = q_ref[0].astype(BF16)
    row = lax.broadcasted_iota(jnp.int32, (NEW_ROWS, 1), 0)
    row_head = jnp.bitwise_and(row, FOX_HEADS - 1)

    c = jnp.concatenate([r[0] for r in lf_pages], axis=1)
    lane = lax.broadcasted_iota(jnp.int32, c.shape, 1)
    s = FOX_HEADS
    while s < past:
        c = c + jnp.where(lane >= s, pltpu.roll(c, s, 1), 0.0)
        s *= 2
    tail_lane = lax.broadcasted_iota(jnp.int32, (NEW_ROWS, LANES), 1)
    tail = jnp.broadcast_to(c[:, past - LANES:], (NEW_ROWS, LANES))
    c_last = jnp.sum(jnp.where(tail_lane == LANES - FOX_HEADS + row_head, tail, 0.0), axis=1, keepdims=True)
    ln = ln_ref[0]
    blocks = [c_last[0:FOX_HEADS] + ln[0:FOX_HEADS]]
    for t in range(1, DEC_SEQ):
        blocks.append(blocks[-1] + ln[t * FOX_HEADS:(t + 1) * FOX_HEADS])
    c_new = jnp.concatenate(blocks, axis=0)
    c_new_row = jnp.sum(jnp.where(tail_lane == row, c_new, 0.0), axis=0, keepdims=True)

    kpad_ref[...] = jnp.zeros_like(kpad_ref)
    vpad_ref[...] = jnp.zeros_like(vpad_ref)
    kpad_ref[0:NEW_ROWS, :] = kn_ref[0]
    vpad_ref[0:NEW_ROWS, :] = vn_ref[0]

    s_past = jnp.concatenate(
        [lax.dot_general(q, kp[0].astype(BF16), NT_DIMS, preferred_element_type=F32) for kp in k_pages], axis=1)
    s_past = s_past * FOX_SCALE + c_new - c
    s_past = jnp.where(jnp.bitwise_and(lane, FOX_HEADS - 1) == row_head, s_past, NEG_INF)
    s_new = lax.dot_general(q, kpad_ref[...].astype(BF16), NT_DIMS, preferred_element_type=F32)
    s_new = s_new * FOX_SCALE + c_new - c_new_row
    visible = (jnp.bitwise_and(tail_lane, FOX_HEADS - 1) == row_head) & (tail_lane <= row)
    s_new = jnp.where(visible, s_new, NEG_INF)

    m = jnp.maximum(jnp.max(s_past, axis=1, keepdims=True), jnp.max(s_new, axis=1, keepdims=True))
    p_past = jnp.exp(s_past - m)
    p_new = jnp.exp(s_new - m)
    total = jnp.sum(p_past, axis=1, keepdims=True) + jnp.sum(p_new, axis=1, keepdims=True)
    p_past = p_past.astype(BF16)
    acc = jnp.dot(p_new.astype(BF16), vpad_ref[...].astype(BF16), preferred_element_type=F32)
    for j, vp in enumerate(v_pages):
        acc = acc + jnp.dot(p_past[:, j * PAGE_ROWS:(j + 1) * PAGE_ROWS], vp[0].astype(BF16),
                            preferred_element_type=F32)
    o_ref[0] = acc / total


def fox_paged(q, k_new, v_new, logf_new, cache_k, cache_v, cache_logf, page_table, first_page):
    bsz = q.shape[0]

    def page(j):
        return lambda b, pt: (first_page + pt[b, j], 0, 0)

    new_spec = pl.BlockSpec((1, NEW_ROWS, FOX_HEAD_DIM), lambda b, pt: (b, 0, 0))
    in_specs = [new_spec, new_spec, new_spec, pl.BlockSpec((1, NEW_ROWS, 1), lambda b, pt: (b, 0, 0))]
    in_specs += [pl.BlockSpec((1, PAGE_ROWS, FOX_HEAD_DIM), page(j)) for j in range(N_PAGES)]
    in_specs += [pl.BlockSpec((1, PAGE_ROWS, FOX_HEAD_DIM), page(j)) for j in range(N_PAGES)]
    in_specs += [pl.BlockSpec((1, 1, PAGE_ROWS), page(j)) for j in range(N_PAGES)]
    return pl.pallas_call(
        _fox_paged_kernel,
        grid_spec=pltpu.PrefetchScalarGridSpec(
            num_scalar_prefetch=1,
            grid=(bsz,),
            in_specs=in_specs,
            out_specs=new_spec,
            scratch_shapes=[pltpu.VMEM((LANES, FOX_HEAD_DIM), F32), pltpu.VMEM((LANES, FOX_HEAD_DIM), F32)],
        ),
        out_shape=jax.ShapeDtypeStruct((bsz, NEW_ROWS, FOX_HEAD_DIM), F32),
        compiler_params=_params("parallel"),
        name="fox_paged",
    )(page_table, q, k_new, v_new, logf_new,
      *([cache_k] * N_PAGES), *([cache_v] * N_PAGES), *([cache_logf] * N_PAGES))


ROPE_HALF = RET_KD // 2
RET_SCALE = RET_KD ** -0.5
RET_LOG_GAMMA = [float(np.log1p(-np.float32(2.0 ** (-5.0 - h)))) for h in range(RET_HEADS)]


def _rope_kernel(f_ref, cos_ref, sin_ref, *, start):
    pos = (start + lax.broadcasted_iota(jnp.int32, cos_ref.shape, 0)).astype(F32)
    ang = pos * f_ref[...]
    cos_ref[...] = jnp.cos(ang)
    sin_ref[...] = jnp.sin(ang)


def rope_table(start, n):
    inv_freq = ROPE_BASE ** (-jnp.arange(ROPE_HALF, dtype=F32) / ROPE_HALF)
    shape = jax.ShapeDtypeStruct((n, ROPE_HALF), F32)
    return pl.pallas_call(
        functools.partial(_rope_kernel, start=start), out_shape=(shape, shape), name="rope_table",
    )(inv_freq.reshape(1, ROPE_HALF))


def _rotate(x, cos, sin):
    x1, x2 = x[:, :ROPE_HALF], x[:, ROPE_HALF:]
    return jnp.concatenate([x1 * cos - x2 * sin, x1 * sin + x2 * cos], axis=1)


def _group_norm_gate(o, g, gain):
    mu = jnp.mean(o, axis=-1, keepdims=True)
    var = jnp.mean(jnp.square(o - mu), axis=-1, keepdims=True)
    return g * jax.nn.sigmoid(g) * ((o - mu) * lax.rsqrt(var + EPS) * gain)


def _head_slabs(refs, h, per_slab):
    width = PROJ_TILE // per_slab
    return refs[h // per_slab], slice((h % per_slab) * width, (h % per_slab + 1) * width)


def _ret_seq_kernel(*refs):
    q_refs, k_refs, v_refs, g_refs = refs[0:2], refs[2:4], refs[4:8], refs[8:12]
    cos_ref, sin_ref, gain_ref, o_ref, s_out, s_ref = refs[12:]
    step = pl.program_id(1)

    @pl.when(step == 0)
    def _():
        s_ref[...] = jnp.zeros_like(s_ref)

    cos = cos_ref[...]
    sin = sin_ref[...]
    cl = RET_CHUNK
    i_col = lax.broadcasted_iota(jnp.int32, (cl, 1), 0).astype(F32)
    diff = i_col - lax.broadcasted_iota(jnp.int32, (1, cl), 1).astype(F32)
    for h in range(RET_HEADS):
        lg = RET_LOG_GAMMA[h]
        ref, sl = _head_slabs(q_refs, h, 4)
        q = _rotate(ref[0, :, sl], cos, sin).astype(BF16)
        ref, sl = _head_slabs(k_refs, h, 4)
        k = _rotate(ref[0, :, sl], cos, sin) * RET_SCALE
        ref, sl = _head_slabs(v_refs, h, 2)
        v = ref[0, :, sl].astype(BF16)
        ref, sl = _head_slabs(g_refs, h, 2)
        g = ref[0, :, sl]
        dmask = jnp.where(diff >= 0, jnp.exp(jnp.maximum(diff, 0.0) * lg), 0.0)
        inner = lax.dot_general(q, k.astype(BF16), NT_DIMS, preferred_element_type=F32) * dmask
        state = s_ref[h]
        out = jnp.dot(inner.astype(BF16), v, preferred_element_type=F32)
        out = out + jnp.dot(q, state.astype(BF16), preferred_element_type=F32) * jnp.exp((i_col + 1.0) * lg)
        k_dec = k * jnp.exp((cl - 1.0 - i_col) * lg)
        s_ref[h] = math.exp(cl * lg) * state + jnp.dot(k_dec.T.astype(BF16), v, preferred_element_type=F32)
        hs = slice(h * RET_VD, (h + 1) * RET_VD)
        o_ref[:, hs] = _group_norm_gate(out, g, gain_ref[:, hs]).astype(BF16)

    @pl.when(step == pl.num_programs(1) - 1)
    def _():
        s_out[0] = s_ref[...]


def ret_sequences(z, cos, sin, gain, bsz, seq_len):
    steps = seq_len // RET_CHUNK
    vd = RET_HEADS * RET_VD
    slab = lambda j: pl.BlockSpec((1, RET_CHUNK, PROJ_TILE), lambda b, c: (j, b * steps + c, 0))
    table = pl.BlockSpec((RET_CHUNK, ROPE_HALF), lambda b, c: (c, 0))
    return pl.pallas_call(
        _ret_seq_kernel,
        grid=(bsz, steps),
        in_specs=[slab(j) for j in range(12)] + [table, table, _whole((1, vd))],
        out_specs=(pl.BlockSpec((RET_CHUNK, vd), lambda b, c: (b * steps + c, 0)),
                   pl.BlockSpec((1, RET_HEADS, RET_KD, RET_VD), lambda b, c: (b, 0, 0, 0))),
        out_shape=(jax.ShapeDtypeStruct((bsz * seq_len, vd), BF16),
                   jax.ShapeDtypeStruct((bsz, RET_HEADS, RET_KD, RET_VD), F32)),
        scratch_shapes=[pltpu.VMEM((RET_HEADS, RET_KD, RET_VD), F32)],
        compiler_params=_params("parallel", "arbitrary"),
        name="ret_sequences",
    )(*([z] * 12), cos, sin, gain.reshape(1, vd))


RET_PAD = 128


RET_STEP_SEQS = 8 // DEC_SEQ


def _ret_step_kernel(z_ref, s0_ref, cos_ref, sin_ref, gain_ref, o_ref, s_out, qpad, kpad, vpad, gpad):
    cl = DEC_SEQ
    for ref in (qpad, kpad, vpad, gpad):
        ref[...] = jnp.zeros_like(ref)
    cos = cos_ref[...]
    sin = sin_ref[...]
    i_col = lax.broadcasted_iota(jnp.int32, (RET_PAD, 1), 0).astype(F32)
    diff = i_col - lax.broadcasted_iota(jnp.int32, (1, RET_PAD), 1).astype(F32)
    valid = (diff >= 0) & (i_col < cl)
    for b in range(RET_STEP_SEQS):
        tok = slice(b * cl, (b + 1) * cl)
        outs = []
        for h in range(RET_HEADS):
            lg = RET_LOG_GAMMA[h]
            qpad[0:cl, :] = z_ref[h // 4, tok, (h % 4) * RET_KD:(h % 4 + 1) * RET_KD]
            kpad[0:cl, :] = z_ref[2 + h // 4, tok, (h % 4) * RET_KD:(h % 4 + 1) * RET_KD]
            vpad[0:cl, :] = z_ref[4 + h // 2, tok, (h % 2) * RET_VD:(h % 2 + 1) * RET_VD]
            gpad[0:cl, :] = z_ref[8 + h // 2, tok, (h % 2) * RET_VD:(h % 2 + 1) * RET_VD]
            q = _rotate(qpad[...], cos, sin).astype(BF16)
            k = _rotate(kpad[...], cos, sin) * RET_SCALE
            v = vpad[...].astype(BF16)
            dmask = jnp.where(valid, jnp.exp(jnp.maximum(diff, 0.0) * lg), 0.0)
            inner = lax.dot_general(q, k.astype(BF16), NT_DIMS, preferred_element_type=F32) * dmask
            state = s0_ref[b, h]
            out = jnp.dot(inner.astype(BF16), v, preferred_element_type=F32)
            out = out + jnp.dot(q, state.astype(BF16), preferred_element_type=F32) * jnp.exp((i_col + 1.0) * lg)
            k_dec = k * jnp.exp((cl - 1.0 - i_col) * lg)
            s_out[b, h] = math.exp(cl * lg) * state + jnp.dot(k_dec.T.astype(BF16), v, preferred_element_type=F32)
            hs = slice(h * RET_VD, (h + 1) * RET_VD)
            outs.append(_group_norm_gate(out[0:8], gpad[0:8, :], gain_ref[:, hs]))
        o_ref[tok, :] = jnp.concatenate(outs, axis=1)[0:cl]


def ret_steps(z, first_row, s0, cos, sin, gain):
    bsz = s0.shape[0]
    vd = RET_HEADS * RET_VD
    rows = RET_STEP_SEQS * DEC_SEQ
    state_spec = pl.BlockSpec((RET_STEP_SEQS, RET_HEADS, RET_KD, RET_VD), lambda i: (i, 0, 0, 0))
    return pl.pallas_call(
        _ret_step_kernel,
        grid=(bsz // RET_STEP_SEQS,),
        in_specs=[pl.BlockSpec((12, rows, PROJ_TILE), lambda i: (0, first_row // rows + i, 0)), state_spec,
                  _whole((RET_PAD, ROPE_HALF)), _whole((RET_PAD, ROPE_HALF)), _whole((1, vd))],
        out_specs=(pl.BlockSpec((rows, vd), lambda i: (i, 0)), state_spec),
        out_shape=(jax.ShapeDtypeStruct((bsz * DEC_SEQ, vd), F32),
                   jax.ShapeDtypeStruct(s0.shape, F32)),
        scratch_shapes=[pltpu.VMEM((RET_PAD, RET_KD), F32), pltpu.VMEM((RET_PAD, RET_KD), F32),
                        pltpu.VMEM((RET_PAD, RET_VD), F32), pltpu.VMEM((RET_PAD, RET_VD), F32)],
        compiler_params=_params("parallel"),
        name="ret_steps",
    )(z, s0, cos, sin, gain.reshape(1, vd))


def _even_mixer(h, gain, e, cache_k, cache_v, cache_logf, state_s5_re, state_s5_im, page_table, w_in, b_f,
                s5_w):
    z, zf = norm_proj(h, gain, w_in, 4 * PROJ_TILE, narrow=True)

    s5_p, hr_p, hi_p = s5_sequences(z, s5_w, BATCH, SEQ)
    u_s = jnp.stack([z[0, N_PROMPT + t::DEC_SEQ] for t in range(DEC_SEQ)])
    s5_s, hr_s, hi_s = s5_steps(u_s, state_s5_re.reshape(DEC_BATCH, S5_LANES),
                                state_s5_im.reshape(DEC_BATCH, S5_LANES), s5_w)
    s5_s = s5_s.transpose(1, 0, 2).reshape(N_SAMPLE, S5_WIDTH)

    lf_p, c_p, ct_p = fox_gates_seq(zf, b_f, BATCH, SEQ)
    lf_s = fox_gates(zf[N_PROMPT:], b_f)[:, :FOX_HEADS]
    k_p, v_p, q_s, k_s, v_s = head_rows(z)
    att_p = fox_sequences(z, c_p, ct_p, BATCH, SEQ)
    pool = cache_k.shape[1]
    new_rows = lambda t: t.reshape(DEC_BATCH, NEW_ROWS, FOX_HEAD_DIM)
    att_s = fox_paged(new_rows(q_s), new_rows(k_s), new_rows(v_s), lf_s.reshape(DEC_BATCH, NEW_ROWS, 1),
                      cache_k.reshape(-1, PAGE_ROWS, FOX_HEAD_DIM), cache_v.reshape(-1, PAGE_ROWS, FOX_HEAD_DIM),
                      cache_logf.reshape(-1, 1, PAGE_ROWS), page_table, e * pool)
    att_s = att_s.reshape(N_SAMPLE, FOX_WIDTH)

    heads = lambda t, b, s: t.reshape(b, s, FOX_HEADS, FOX_HEAD_DIM)
    grid = lambda t: t.reshape(-1, S5_GROUPS, S5_STATE)
    st_p = (heads(k_p, BATCH, SEQ), heads(v_p, BATCH, SEQ),
            lf_p[:, :FOX_HEADS].reshape(BATCH, SEQ, FOX_HEADS), grid(hr_p), grid(hi_p))
    st_s = (heads(k_s, DEC_BATCH, DEC_SEQ), heads(v_s, DEC_BATCH, DEC_SEQ),
            lf_s.reshape(DEC_BATCH, DEC_SEQ, FOX_HEADS), grid(hr_s), grid(hi_s))
    return ([s5_p, att_p], [s5_s, att_s]), st_p, st_s


def _odd_mixer(h, gain, w_in, gn_gain, state, tables):
    z = norm_proj(h, gain, w_in, 12 * PROJ_TILE)
    (cos_p, sin_p), (cos_s, sin_s) = tables
    o_p, st_p = ret_sequences(z, cos_p, sin_p, gn_gain, BATCH, SEQ)
    o_s, st_s = ret_steps(z, N_PROMPT, state, cos_s, sin_s, gn_gain)
    return ([o_p], [o_s]), st_p, st_s


def kernel(x_prompt, x_sample, cache_k, cache_v, cache_logf, state_s5_re, state_s5_im, state_ret, page_table, norm_gain, w_ffn_gate, w_ffn_up, w_ffn_down, w_in_even, b_forget, s5_lam_re, s5_lam_im, s5_b_re, s5_b_im, s5_c_re, s5_c_im, s5_d, s5_log_dt, w_glu, b_glu, w_out_even, w_in_odd, ret_gn_gain, w_out_odd, final_norm):
    x = jnp.concatenate([x_prompt.reshape(N_PROMPT, D_MODEL), x_sample.reshape(N_SAMPLE, D_MODEL)], axis=0)
    tables = (rope_table(0, SEQ), rope_table(PAST_LEN, RET_PAD))
    ev_p, ev_s, od_p, od_s = [], [], [], []
    for layer in range(DEPTH):
        h = ffn(x, norm_gain[layer, 0], w_ffn_gate, w_ffn_up, w_ffn_down, layer, 0)
        if layer % 2 == 0:
            e = layer // 2
            s5_w = s5_weights(s5_lam_re[e], s5_lam_im[e], s5_log_dt[e], s5_b_re[e], s5_b_im[e], s5_c_re[e],
                              s5_c_im[e], s5_d[e], w_glu[e], b_glu[e])
            parts, st_p, st_s = _even_mixer(h, norm_gain[layer, 1], e, cache_k, cache_v, cache_logf,
                                            state_s5_re[e], state_s5_im[e], page_table, w_in_even[e],
                                            b_forget[e], s5_w)
            ev_p.append(st_p)
            ev_s.append(st_s)
            w_out = w_out_even[e]
        else:
            o = layer // 2
            parts, st_p, st_s = _odd_mixer(h, norm_gain[layer, 1], w_in_odd[o], ret_gn_gain[o], state_ret[o],
                                           tables)
            od_p.append(st_p)
            od_s.append(st_s)
            w_out = w_out_odd[o]
        h = out_proj(*parts, w_out, h)
        x = ffn(h, norm_gain[layer, 2], w_ffn_gate, w_ffn_up, w_ffn_down, layer, 1)

    y_prompt, y_sample = final_rms(x, final_norm)
    y_prompt = y_prompt.reshape(BATCH, SEQ, D_MODEL)
    y_sample = y_sample.reshape(DEC_BATCH, DEC_SEQ, D_MODEL)
    stack = lambda sts, i: jnp.stack([s[i] for s in sts])
    return (y_prompt, y_sample,
            stack(ev_p, 0), stack(ev_p, 1), stack(ev_p, 2), stack(ev_p, 3), stack(ev_p, 4), jnp.stack(od_p),
            stack(ev_s, 0), stack(ev_s, 1), stack(ev_s, 2), stack(ev_s, 3), stack(ev_s, 4), jnp.stack(od_s))
```

```python
import functools
import math

import jax
import jax.numpy as jnp
import numpy as np
from jax import lax
from jax.experimental import pallas as pl
from jax.experimental.pallas import tpu as pltpu

D_MODEL = 2048
BATCH = 2
SEQ = 4096
DEPTH = 2
DEC_BATCH = 128
DEC_SEQ = 4
PAST_LEN = 2048
PAGE_SIZE = 128
N_PAGES = PAST_LEN // PAGE_SIZE
S5_WIDTH = 1024
S5_GROUP = 16
S5_GROUPS = 64
S5_STATE = 64
FOX_WIDTH = 1024
FOX_HEAD_DIM = 128
FOX_HEADS = 8
NEG_INF = -1e30
RET_HEADS = 8
RET_KD = 256
RET_VD = 512
RET_CHUNK = 128
ROPE_BASE = 10000.0
D_FF = 5504
EPS = 1e-6

N_PROMPT = BATCH * SEQ
N_SAMPLE = DEC_BATCH * DEC_SEQ
N_TOK = N_PROMPT + N_SAMPLE

ROW_TILE = 1088
FF_TILE = 256
LANES = 128
VMEM_LIMIT = 56 * 1024 * 1024
FFN_VMEM_LIMIT = 60 * 1024 * 1024

F32 = jnp.float32
BF16 = jnp.bfloat16
NT_DIMS = (((1,), (1,)), ((), ()))


def _rms_rows(x, gain):
    return x * lax.rsqrt(jnp.mean(x * x, axis=-1, keepdims=True) + EPS) * gain


def _whole(shape):
    return pl.BlockSpec(shape, lambda *_: (0,) * len(shape))


def _params(*semantics):
    return pltpu.CompilerParams(dimension_semantics=semantics, vmem_limit_bytes=VMEM_LIMIT)


FFN_GROUP = 2


def _ffn_kernel(x_ref, g_ref, wg_ref, wu_ref, wd_ref, o_ref, h_ref, *, d_ff):
    step = pl.program_id(1)
    tf = wg_ref.shape[1]

    @pl.when(step < FFN_GROUP)
    def _():
        rows = pl.ds(pl.multiple_of(step * ROW_TILE, ROW_TILE), ROW_TILE)
        x = x_ref[...]
        h_ref[rows, :] = _rms_rows(x, g_ref[...]).astype(BF16)
        o_ref[rows, :] = x

    @pl.when(step >= FFN_GROUP)
    def _():
        first = (step - FFN_GROUP) * tf
        wg = wg_ref[...].astype(BF16)
        wu = wu_ref[...].astype(BF16)
        col_ok = first + lax.broadcasted_iota(jnp.int32, (1, tf), 1) < d_ff
        row_ok = first + lax.broadcasted_iota(jnp.int32, (tf, 1), 0) < d_ff
        wd = jnp.where(row_ok, wd_ref[...], 0.0).astype(BF16)
        for t in range(FFN_GROUP):
            rows = slice(t * ROW_TILE, (t + 1) * ROW_TILE)
            h = h_ref[rows, :]
            a = jnp.dot(h, wg, preferred_element_type=F32)
            b = jnp.dot(h, wu, preferred_element_type=F32)
            c = jnp.where(col_ok, a * jax.nn.sigmoid(a) * b, 0.0).astype(BF16)
            o_ref[rows, :] += 0.5 * jnp.dot(c, wd, preferred_element_type=F32)


def ffn(x, gain, wg, wu, wd, layer, which):
    m, d = x.shape
    d_ff = wg.shape[3]
    group_rows = FFN_GROUP * ROW_TILE
    w_tile = lambda r, s: jnp.maximum(s - FFN_GROUP, 0)
    return pl.pallas_call(
        functools.partial(_ffn_kernel, d_ff=d_ff),
        grid=(m // group_rows, FFN_GROUP + pl.cdiv(d_ff, FF_TILE)),
        in_specs=[
            pl.BlockSpec((ROW_TILE, d), lambda r, s: (FFN_GROUP * r + jnp.minimum(s, FFN_GROUP - 1), 0),
                         pipeline_mode=pl.Buffered(1)),
            pl.BlockSpec((1, d), lambda r, s: (0, 0)),
            pl.BlockSpec((None, None, d, FF_TILE), lambda r, s: (layer, which, 0, w_tile(r, s))),
            pl.BlockSpec((None, None, d, FF_TILE), lambda r, s: (layer, which, 0, w_tile(r, s))),
            pl.BlockSpec((None, None, FF_TILE, d), lambda r, s: (layer, which, w_tile(r, s), 0)),
        ],
        out_specs=pl.BlockSpec((group_rows, d), lambda r, s: (r, 0), pipeline_mode=pl.Buffered(1)),
        out_shape=jax.ShapeDtypeStruct((m, d), F32),
        scratch_shapes=[pltpu.VMEM((group_rows, d), BF16)],
        compiler_params=pltpu.CompilerParams(
            dimension_semantics=("parallel", "arbitrary"), vmem_limit_bytes=FFN_VMEM_LIMIT),
        name="ffn",
    )(x, gain.reshape(1, d), wg, wu, wd)


SPLIT_TILE = 512
N_PROMPT_TILES = N_PROMPT // SPLIT_TILE


def _split_specs(block):
    zeros = (0,) * (len(block) - 1)
    return (pl.BlockSpec(block, lambda i: (jnp.minimum(i, N_PROMPT_TILES - 1),) + zeros),
            pl.BlockSpec(block, lambda i: (0,) + zeros))


def _final_norm_kernel(x_ref, g_ref, yp_ref, ys_ref):
    y = _rms_rows(x_ref[...], g_ref[...])

    @pl.when(pl.program_id(0) < N_PROMPT_TILES)
    def _():
        yp_ref[...] = y

    @pl.when(pl.program_id(0) == N_PROMPT_TILES)
    def _():
        ys_ref[...] = y


def final_rms(x, gain):
    d = x.shape[1]
    return pl.pallas_call(
        _final_norm_kernel,
        grid=(N_PROMPT_TILES + 1,),
        in_specs=[pl.BlockSpec((SPLIT_TILE, d), lambda i: (i, 0)), _whole((1, d))],
        out_specs=_split_specs((SPLIT_TILE, d)),
        out_shape=(jax.ShapeDtypeStruct((N_PROMPT, d), F32), jax.ShapeDtypeStruct((N_SAMPLE, d), F32)),
        compiler_params=_params("arbitrary"),
        name="final_norm",
    )(x, gain.reshape(1, d))


def _head_rows_kernel(q_ref, k_ref, v_ref, kp_ref, vp_ref, qs_ref, ks_ref, vs_ref):
    def put(pairs):
        for src, dst in pairs:
            for h in range(FOX_HEADS):
                dst[:, h, :] = src[0, :, h * FOX_HEAD_DIM:(h + 1) * FOX_HEAD_DIM]

    @pl.when(pl.program_id(0) < N_PROMPT_TILES)
    def _():
        put([(k_ref, kp_ref), (v_ref, vp_ref)])

    @pl.when(pl.program_id(0) == N_PROMPT_TILES)
    def _():
        put([(q_ref, qs_ref), (k_ref, ks_ref), (v_ref, vs_ref)])


def head_rows(z):
    block = (SPLIT_TILE, FOX_HEADS, FOX_HEAD_DIM)
    shape = lambda n: jax.ShapeDtypeStruct((n, FOX_HEADS, FOX_HEAD_DIM), F32)
    kp_spec, ks_spec = _split_specs(block)
    return pl.pallas_call(
        _head_rows_kernel,
        grid=(N_PROMPT_TILES + 1,),
        in_specs=[pl.BlockSpec((1, SPLIT_TILE, FOX_WIDTH), lambda i: (1, N_PROMPT_TILES, 0)),
                  pl.BlockSpec((1, SPLIT_TILE, FOX_WIDTH), lambda i: (2, i, 0)),
                  pl.BlockSpec((1, SPLIT_TILE, FOX_WIDTH), lambda i: (3, i, 0))],
        out_specs=(kp_spec, kp_spec, ks_spec, ks_spec, ks_spec),
        out_shape=(shape(N_PROMPT), shape(N_PROMPT), shape(N_SAMPLE), shape(N_SAMPLE), shape(N_SAMPLE)),
        compiler_params=_params("arbitrary"),
        name="head_rows",
    )(z, z, z)


PROJ_TILE = 1024


def _norm_proj_kernel(x_ref, g_ref, w_ref, *rest, narrow):
    if narrow:
        wn_ref, z_ref, zn_ref, h_ref = rest
    else:
        z_ref, h_ref = rest
    j = pl.program_id(1)

    @pl.when(j == 0)
    def _():
        h_ref[...] = _rms_rows(x_ref[...], g_ref[...]).astype(BF16)
        if narrow:
            zn_ref[...] = jnp.dot(h_ref[...], wn_ref[...].astype(BF16), preferred_element_type=F32)

    z_ref[0] = jnp.dot(h_ref[...], w_ref[...].astype(BF16), preferred_element_type=F32)


def norm_proj(x, gain, w, n_wide, narrow=False):
    m, d = x.shape
    nj = n_wide // PROJ_TILE
    in_specs = [
        pl.BlockSpec((ROW_TILE, d), lambda i, j: (i, 0), pipeline_mode=pl.Buffered(1)),
        pl.BlockSpec((1, d), lambda i, j: (0, 0)),
        pl.BlockSpec((d, PROJ_TILE), lambda i, j: (0, j)),
    ]
    args = [x, gain.reshape(1, d), w]
    out_shape = jax.ShapeDtypeStruct((nj, m, PROJ_TILE), F32)
    out_specs = pl.BlockSpec((1, ROW_TILE, PROJ_TILE), lambda i, j: (j, i, 0))
    if narrow:
        in_specs.append(pl.BlockSpec((d, LANES), lambda i, j: (0, n_wide // LANES)))
        args.append(w)
        out_shape = (out_shape, jax.ShapeDtypeStruct((m, LANES), F32))
        out_specs = (out_specs, pl.BlockSpec((ROW_TILE, LANES), lambda i, j: (i, 0)))
    return pl.pallas_call(
        functools.partial(_norm_proj_kernel, narrow=narrow),
        grid=(m // ROW_TILE, nj),
        in_specs=in_specs,
        out_specs=out_specs,
        out_shape=out_shape,
        scratch_shapes=[pltpu.VMEM((ROW_TILE, d), BF16)],
        compiler_params=_params("parallel", "arbitrary"),
        name="norm_proj_narrow" if narrow else "norm_proj",
    )(*args)


OUT_TILE = 512


def _out_proj_kernel(*refs, n_parts):
    a_refs, w_refs, r_ref, o_ref = refs[:n_parts], refs[n_parts:2 * n_parts], refs[2 * n_parts], refs[-1]
    acc = r_ref[...]
    for a_ref, w_ref in zip(a_refs, w_refs):
        acc = acc + jnp.dot(a_ref[...].astype(BF16), w_ref[...].astype(BF16), preferred_element_type=F32)
    o_ref[...] = acc


def _out_proj_rows(parts, w, res, row_tile, first_tile, prev=None):
    rows, k = parts[0].shape
    n = w.shape[1]
    n_parts = len(parts)
    out_block = pl.BlockSpec((row_tile, OUT_TILE), lambda i, j: (first_tile + i, j))
    in_specs = ([pl.BlockSpec((row_tile, k), lambda i, j: (i, 0))] * n_parts
                + [pl.BlockSpec((k, OUT_TILE), functools.partial(lambda i, j, p: (p, j), p=p))
                   for p in range(n_parts)]
                + [out_block])
    args = [*parts, *([w] * n_parts), res]
    aliases = {}
    if prev is not None:
        in_specs.append(pl.BlockSpec(memory_space=pl.ANY))
        args.append(prev)
        aliases = {len(args) - 1: 0}
    return pl.pallas_call(
        functools.partial(_out_proj_kernel, n_parts=n_parts),
        grid=(rows // row_tile, n // OUT_TILE),
        in_specs=in_specs,
        out_specs=out_block,
        out_shape=jax.ShapeDtypeStruct((res.shape[0], n), F32),
        input_output_aliases=aliases,
        compiler_params=_params("parallel", "arbitrary"),
        name="out_proj",
    )(*args)


OUT_ROWS = 1024


def out_proj(parts_prompt, parts_sample, w, res):
    out = _out_proj_rows(parts_prompt, w, res, OUT_ROWS, 0)
    return _out_proj_rows(parts_sample, w, res, N_SAMPLE, N_PROMPT // N_SAMPLE, prev=out)


S5_LANES = S5_GROUPS * S5_STATE
S5_BLOCKS = 4
S5_BLOCK_CH = S5_WIDTH // S5_BLOCKS
S5_BLOCK_ST = S5_LANES // S5_BLOCKS
SCAN_LANES = 512
S5_TILE = 256


def _s5_param_kernel(lr_ref, li_ref, ldt_ref, bre_ref, bim_ref, tab_ref, bbr_ref, bbi_ref):
    lr = lr_ref[...]
    li = li_ref[...]
    dt = jnp.exp(ldt_ref[...])
    dec = lr * dt
    ang = li * dt
    er = jnp.exp(dec)
    xr = er * jnp.cos(ang) - 1.0
    xi = er * jnp.sin(ang)
    den = lr * lr + li * li
    cr = (xr * lr + xi * li) / den
    ci = (xi * lr - xr * li) / den
    bre = bre_ref[...]
    bim = bim_ref[...]
    bbr_ref[...] = cr * bre - ci * bim
    bbi_ref[...] = cr * bim + ci * bre

    row = lax.broadcasted_iota(jnp.int32, (8, S5_LANES), 0)

    def power(k):
        mag = jnp.exp(k * dec)
        return mag * jnp.cos(k * ang), mag * jnp.sin(k * ang)

    for idx, s in enumerate((1, 2, 4)):
        pr, pi = power(float(s))
        tab_ref[2 * idx] = jnp.where(row >= s, pr, 0.0)
        tab_ref[2 * idx + 1] = jnp.where(row >= s, pi, 0.0)
    pr, pi = power((row + 1).astype(F32))
    tab_ref[6] = pr
    tab_ref[7] = pi


def s5_params(lam_re, lam_im, log_dt, b_re, b_im):
    flat = lambda t: t.reshape(1, S5_LANES)
    ldt = jnp.broadcast_to(log_dt[:, None], (S5_GROUPS, S5_STATE))
    to_rows = lambda t: t.reshape(S5_LANES, S5_GROUP).T
    return pl.pallas_call(
        _s5_param_kernel,
        out_shape=(jax.ShapeDtypeStruct((8, 8, S5_LANES), F32),
                   jax.ShapeDtypeStruct((S5_GROUP, S5_LANES), F32),
                   jax.ShapeDtypeStruct((S5_GROUP, S5_LANES), F32)),
        name="s5_params",
    )(flat(lam_re), flat(lam_im), flat(ldt), to_rows(b_re), to_rows(b_im))


def _block_diag_in(bbr, bbi):
    bb = jnp.stack([bbr, bbi]).reshape(2, S5_GROUP, S5_BLOCKS, 16, S5_STATE)
    eye = jnp.eye(16, dtype=F32)
    t = bb[:, :, :, :, None, :] * eye[None, None, None, :, :, None]
    return t.transpose(2, 3, 1, 0, 4, 5).reshape(S5_BLOCKS, S5_BLOCK_CH, 2 * S5_BLOCK_ST).astype(BF16)


def _block_diag_out(c):
    cc = c.reshape(S5_BLOCKS, 16, S5_GROUP, S5_STATE)
    eye = jnp.eye(16, dtype=F32)
    t = cc.transpose(0, 1, 3, 2)[:, :, :, None, :] * eye[None, :, None, :, None]
    return t.reshape(S5_BLOCKS, S5_BLOCK_ST, S5_BLOCK_CH).astype(BF16)


def _s5_drive(ub, bbd_ref, hr_ref, hi_ref):
    for blk in range(S5_BLOCKS):
        bu = jnp.dot(ub[:, blk * S5_BLOCK_CH:(blk + 1) * S5_BLOCK_CH], bbd_ref[blk], preferred_element_type=F32)
        hr_ref[:, blk * S5_BLOCK_ST:(blk + 1) * S5_BLOCK_ST] = bu[:, :S5_BLOCK_ST]
        hi_ref[:, blk * S5_BLOCK_ST:(blk + 1) * S5_BLOCK_ST] = bu[:, S5_BLOCK_ST:]


def _s5_readout(u, hr_ref, hi_ref, cr_ref, ci_ref, d_ref, wg_ref, bg_ref):
    ys = []
    for blk in range(S5_BLOCKS):
        sl = slice(blk * S5_BLOCK_ST, (blk + 1) * S5_BLOCK_ST)
        yr = jnp.dot(hr_ref[:, sl].astype(BF16), cr_ref[blk], preferred_element_type=F32)
        yi = jnp.dot(hi_ref[:, sl].astype(BF16), ci_ref[blk], preferred_element_type=F32)
        ys.append(yr - yi)
    y = jax.nn.gelu(jnp.concatenate(ys, axis=1) + d_ref[...] * u)
    gate = jax.nn.sigmoid(jnp.dot(y.astype(BF16), wg_ref[...], preferred_element_type=F32) + bg_ref[...])
    return (y * gate).astype(BF16)


def _s5_seq_kernel(u_ref, tab_ref, bbd_ref, cr_ref, ci_ref, d_ref, wg_ref, bg_ref,
                   o_ref, hr_out, hi_out, hr_ref, hi_ref, car_r, car_i):
    step = pl.program_id(1)

    @pl.when(step == 0)
    def _():
        car_r[...] = jnp.zeros_like(car_r)
        car_i[...] = jnp.zeros_like(car_i)

    u = u_ref[0]
    ub = u.astype(BF16)
    n_tiles = u.shape[0] // 8
    ys = []
    for blk in range(S5_BLOCKS):
        st = slice(blk * S5_BLOCK_ST, (blk + 1) * S5_BLOCK_ST)
        bu = jnp.dot(ub[:, blk * S5_BLOCK_CH:(blk + 1) * S5_BLOCK_CH], bbd_ref[blk], preferred_element_type=F32)
        hr_ref[:, st] = bu[:, :S5_BLOCK_ST]
        hi_ref[:, st] = bu[:, S5_BLOCK_ST:]
        for c in range(S5_BLOCK_ST // SCAN_LANES):
            lo = blk * S5_BLOCK_ST + c * SCAN_LANES
            sl = slice(lo, lo + SCAN_LANES)
            tabs = [tab_ref[k, :, sl] for k in range(8)]
            cr, ci = car_r[:, sl], car_i[:, sl]
            for i in range(n_tiles):
                rows = slice(i * 8, (i + 1) * 8)
                xr = hr_ref[rows, sl]
                xi = hi_ref[rows, sl]
                for k, s in enumerate((1, 2, 4)):
                    ar, ai = tabs[2 * k], tabs[2 * k + 1]
                    rr = pltpu.roll(xr, s, 0)
                    ri = pltpu.roll(xi, s, 0)
                    xr, xi = xr + ar * rr - ai * ri, xi + ar * ri + ai * rr
                pr, pi = tabs[6], tabs[7]
                hr = xr + pr * cr - pi * ci
                hi = xi + pr * ci + pi * cr
                hr_ref[rows, sl] = hr
                hi_ref[rows, sl] = hi
                cr, ci = jnp.broadcast_to(hr[7:8], hr.shape), jnp.broadcast_to(hi[7:8], hi.shape)
            car_r[:, sl] = cr
            car_i[:, sl] = ci
        yr = jnp.dot(hr_ref[:, st].astype(BF16), cr_ref[blk], preferred_element_type=F32)
        yi = jnp.dot(hi_ref[:, st].astype(BF16), ci_ref[blk], preferred_element_type=F32)
        ys.append(yr - yi)
    y = jax.nn.gelu(jnp.concatenate(ys, axis=1) + d_ref[...] * u)
    gate = jax.nn.sigmoid(jnp.dot(y.astype(BF16), wg_ref[...], preferred_element_type=F32) + bg_ref[...])
    o_ref[...] = (y * gate).astype(BF16)

    @pl.when(step == pl.num_programs(1) - 1)
    def _():
        hr_out[0] = car_r[0:1, :]
        hi_out[0] = car_i[0:1, :]


def _s5_weight_specs():
    return [_whole((8, 8, S5_LANES)), _whole((S5_BLOCKS, S5_BLOCK_CH, 2 * S5_BLOCK_ST)),
            _whole((S5_BLOCKS, S5_BLOCK_ST, S5_BLOCK_CH)), _whole((S5_BLOCKS, S5_BLOCK_ST, S5_BLOCK_CH)),
            _whole((1, S5_WIDTH)), _whole((S5_WIDTH, S5_WIDTH)), _whole((1, S5_WIDTH))]


def s5_sequences(z, weights, bsz, seq_len):
    steps = seq_len // S5_TILE
    state = jax.ShapeDtypeStruct((bsz, 1, S5_LANES), F32)
    state_spec = pl.BlockSpec((1, 1, S5_LANES), lambda b, s: (b, 0, 0))
    return pl.pallas_call(
        _s5_seq_kernel,
        grid=(bsz, steps),
        in_specs=[pl.BlockSpec((1, S5_TILE, S5_WIDTH), lambda b, s: (0, b * steps + s, 0))] + _s5_weight_specs(),
        out_specs=(pl.BlockSpec((S5_TILE, S5_WIDTH), lambda b, s: (b * steps + s, 0)), state_spec, state_spec),
        out_shape=(jax.ShapeDtypeStruct((bsz * seq_len, S5_WIDTH), BF16), state, state),
        scratch_shapes=[pltpu.VMEM((S5_TILE, S5_LANES), F32), pltpu.VMEM((S5_TILE, S5_LANES), F32),
                        pltpu.VMEM((8, S5_LANES), F32), pltpu.VMEM((8, S5_LANES), F32)],
        compiler_params=_params("parallel", "arbitrary"),
        name="s5_sequences",
    )(z, *weights)


def _s5_step_kernel(u_ref, h0r_ref, h0i_ref, tab_ref, bbd_ref, cr_ref, ci_ref, d_ref, wg_ref, bg_ref,
                    o_ref, hr_ref, hi_ref, bur_ref, bui_ref):
    @pl.when(pl.program_id(0) == 0)
    def _():
        hr_ref[...] = h0r_ref[...]
        hi_ref[...] = h0i_ref[...]

    u = u_ref[0]
    _s5_drive(u.astype(BF16), bbd_ref, bur_ref, bui_ref)
    ar = tab_ref[6, 0:1, :]
    ai = tab_ref[7, 0:1, :]
    hr = hr_ref[...]
    hi = hi_ref[...]
    hr_ref[...] = ar * hr - ai * hi + bur_ref[...]
    hi_ref[...] = ar * hi + ai * hr + bui_ref[...]
    o_ref[0] = _s5_readout(u, hr_ref, hi_ref, cr_ref, ci_ref, d_ref, wg_ref, bg_ref)


def s5_steps(u, h0_re, h0_im, weights):
    steps, rows, _ = u.shape
    state = jax.ShapeDtypeStruct((rows, S5_LANES), F32)
    return pl.pallas_call(
        _s5_step_kernel,
        grid=(steps,),
        in_specs=[pl.BlockSpec((1, rows, S5_WIDTH), lambda t: (t, 0, 0)),
                  _whole((rows, S5_LANES)), _whole((rows, S5_LANES))] + _s5_weight_specs(),
        out_specs=(pl.BlockSpec((1, rows, S5_WIDTH), lambda t: (t, 0, 0)),
                   _whole((rows, S5_LANES)), _whole((rows, S5_LANES))),
        out_shape=(jax.ShapeDtypeStruct((steps, rows, S5_WIDTH), BF16), state, state),
        scratch_shapes=[pltpu.VMEM((rows, S5_LANES), F32), pltpu.VMEM((rows, S5_LANES), F32)],
        compiler_params=_params("arbitrary"),
        name="s5_steps",
    )(u, h0_re, h0_im, *weights)


def s5_weights(lam_re, lam_im, log_dt, b_re, b_im, c_re, c_im, d_skip, w_glu, b_glu):
    tab, bbr, bbi = s5_params(lam_re, lam_im, log_dt, b_re, b_im)
    return (tab, _block_diag_in(bbr, bbi), _block_diag_out(c_re), _block_diag_out(c_im),
            d_skip.reshape(1, S5_WIDTH), w_glu.astype(BF16), b_glu.reshape(1, S5_WIDTH))


def _lane_pad_bias(b_f):
    return jnp.pad(b_f, (0, LANES - FOX_HEADS)).reshape(1, LANES)


def _log_forget(zf_ref, bf_ref):
    lane = lax.broadcasted_iota(jnp.int32, zf_ref.shape, 1)
    return jnp.where(lane < FOX_HEADS, jax.nn.log_sigmoid(zf_ref[...] + bf_ref[...]), 0.0)


def _gate_seq_kernel(zf_ref, bf_ref, lf_ref, c_ref, ct_ref):
    lf = _log_forget(zf_ref, bf_ref)
    lf_ref[...] = lf
    row = lax.broadcasted_iota(jnp.int32, lf.shape, 0)
    c = lf
    s = 1
    while s < lf.shape[0]:
        c = c + jnp.where(row >= s, pltpu.roll(c, s, 0), 0.0)
        s *= 2
    c_ref[0] = c
    ct_ref[0] = c.T[:FOX_HEADS]


def fox_gates_seq(zf, b_f, bsz, seq_len):
    return pl.pallas_call(
        _gate_seq_kernel,
        grid=(bsz,),
        in_specs=[pl.BlockSpec((seq_len, LANES), lambda b: (b, 0)), _whole((1, LANES))],
        out_specs=(pl.BlockSpec((seq_len, LANES), lambda b: (b, 0)),
                   pl.BlockSpec((1, seq_len, LANES), lambda b: (b, 0, 0)),
                   pl.BlockSpec((1, FOX_HEADS, seq_len), lambda b: (b, 0, 0))),
        out_shape=(jax.ShapeDtypeStruct((bsz * seq_len, LANES), F32),
                   jax.ShapeDtypeStruct((bsz, seq_len, LANES), F32),
                   jax.ShapeDtypeStruct((bsz, FOX_HEADS, seq_len), F32)),
        compiler_params=_params("parallel"),
        name="fox_gates_seq",
    )(zf, _lane_pad_bias(b_f))


def _gate_kernel(zf_ref, bf_ref, lf_ref):
    lf_ref[...] = _log_forget(zf_ref, bf_ref)


def fox_gates(zf, b_f):
    return pl.pallas_call(
        _gate_kernel,
        out_shape=jax.ShapeDtypeStruct(zf.shape, F32),
        name="fox_gates",
    )(zf, _lane_pad_bias(b_f))


FOX_TILE = 1024
FOX_SCALE = FOX_HEAD_DIM ** -0.5


def _fox_seq_kernel(q_ref, k_ref, v_ref, c_ref, ct_ref, o_ref):
    h = pl.program_id(1)
    i = pl.program_id(2)
    q = q_ref[0].astype(BF16)
    lane = lax.broadcasted_iota(jnp.int32, c_ref.shape[1:], 1)
    cq = jnp.sum(jnp.where(lane == h, c_ref[0], 0.0), axis=1, keepdims=True)

    def chunk(j, carry, diagonal):
        m, l, acc = carry
        rows = pl.ds(pl.multiple_of(j * FOX_TILE, FOX_TILE), FOX_TILE)
        kc = k_ref[0, rows, :].astype(BF16)
        vc = v_ref[0, rows, :].astype(BF16)
        s = lax.dot_general(q, kc, NT_DIMS, preferred_element_type=F32) * FOX_SCALE
        s = s + cq - ct_ref[0, h, pl.ds(j, 1), :]
        if diagonal:
            qi = lax.broadcasted_iota(jnp.int32, s.shape, 0)
            ki = lax.broadcasted_iota(jnp.int32, s.shape, 1)
            s = jnp.where(qi >= ki, s, NEG_INF)
        m_new = jnp.maximum(m, jnp.max(s, axis=1, keepdims=True))
        alpha = jnp.exp(m - m_new)
        p = jnp.exp(s - m_new)
        l = alpha * l + jnp.sum(p, axis=1, keepdims=True)
        acc = alpha * acc + jnp.dot(p.astype(BF16), vc, preferred_element_type=F32)
        return m_new, l, acc

    init = (jnp.full((FOX_TILE, 1), NEG_INF, F32), jnp.zeros((FOX_TILE, 1), F32),
            jnp.zeros((FOX_TILE, FOX_HEAD_DIM), F32))
    carry = lax.fori_loop(0, i, functools.partial(chunk, diagonal=False), init)
    _, l, acc = chunk(i, carry, diagonal=True)
    o_ref[...] = (acc / l).astype(BF16)


def fox_sequences(z, c, ct, bsz, seq_len):
    nq = seq_len // FOX_TILE
    ct = ct.reshape(bsz, FOX_HEADS, nq, FOX_TILE)
    return pl.pallas_call(
        _fox_seq_kernel,
        grid=(bsz, FOX_HEADS, nq),
        in_specs=[
            pl.BlockSpec((1, FOX_TILE, FOX_HEAD_DIM), lambda b, h, i: (1, b * nq + i, h)),
            pl.BlockSpec((1, seq_len, FOX_HEAD_DIM), lambda b, h, i: (2, b, h)),
            pl.BlockSpec((1, seq_len, FOX_HEAD_DIM), lambda b, h, i: (3, b, h)),
            pl.BlockSpec((1, FOX_TILE, LANES), lambda b, h, i: (b, i, 0)),
            pl.BlockSpec((1, FOX_HEADS, nq, FOX_TILE), lambda b, h, i: (b, 0, 0, 0)),
        ],
        out_specs=pl.BlockSpec((FOX_TILE, FOX_HEAD_DIM), lambda b, h, i: (b * nq + i, h)),
        out_shape=jax.ShapeDtypeStruct((bsz * seq_len, FOX_WIDTH), BF16),
        compiler_params=_params("parallel", "parallel", "arbitrary"),
        name="fox_sequences",
    )(z, z, z, c, ct)


PAGE_ROWS = PAGE_SIZE * FOX_HEADS
NEW_ROWS = DEC_SEQ * FOX_HEADS


def _fox_paged_kernel(pt_ref, q_ref, kn_ref, vn_ref, ln_ref, *refs):
    k_pages = refs[:N_PAGES]
    v_pages = refs[N_PAGES:2 * N_PAGES]
    lf_pages = refs[2 * N_PAGES:3 * N_PAGES]
    o_ref, kpad_ref, vpad_ref = refs[3 * N_PAGES:]
    del pt_ref
    past = N_PAGES * PAGE_ROWS
    q = q_ref[0].astype(BF16)
    row = lax.broadcasted_iota(jnp.int32, (NEW_ROWS, 1), 0)
    row_head = jnp.bitwise_and(row, FOX_HEADS - 1)

    c = jnp.concatenate([r[0] for r in lf_pages], axis=1)
    lane = lax.broadcasted_iota(jnp.int32, c.shape, 1)
    s = FOX_HEADS
    while s < past:
        c = c + jnp.where(lane >= s, pltpu.roll(c, s, 1), 0.0)
        s *= 2
    tail_lane = lax.broadcasted_iota(jnp.int32, (NEW_ROWS, LANES), 1)
    tail = jnp.broadcast_to(c[:, past - LANES:], (NEW_ROWS, LANES))
    c_last = jnp.sum(jnp.where(tail_lane == LANES - FOX_HEADS + row_head, tail, 0.0), axis=1, keepdims=True)
    ln = ln_ref[0]
    blocks = [c_last[0:FOX_HEADS] + ln[0:FOX_HEADS]]
    for t in range(1, DEC_SEQ):
        blocks.append(blocks[-1] + ln[t * FOX_HEADS:(t + 1) * FOX_HEADS])
    c_new = jnp.concatenate(blocks, axis=0)
    c_new_row = jnp.sum(jnp.where(tail_lane == row, c_new, 0.0), axis=0, keepdims=True)

    kpad_ref[...] = jnp.zeros_like(kpad_ref)
    vpad_ref[...] = jnp.zeros_like(vpad_ref)
    kpad_ref[0:NEW_ROWS, :] = kn_ref[0]
    vpad_ref[0:NEW_ROWS, :] = vn_ref[0]

    s_past = jnp.concatenate(
        [lax.dot_general(q, kp[0].astype(BF16), NT_DIMS, preferred_element_type=F32) for kp in k_pages], axis=1)
    s_past = s_past * FOX_SCALE + c_new - c
    s_past = jnp.where(jnp.bitwise_and(lane, FOX_HEADS - 1) == row_head, s_past, NEG_INF)
    s_new = lax.dot_general(q, kpad_ref[...].astype(BF16), NT_DIMS, preferred_element_type=F32)
    s_new = s_new * FOX_SCALE + c_new - c_new_row
    visible = (jnp.bitwise_and(tail_lane, FOX_HEADS - 1) == row_head) & (tail_lane <= row)
    s_new = jnp.where(visible, s_new, NEG_INF)

    m = jnp.maximum(jnp.max(s_past, axis=1, keepdims=True), jnp.max(s_new, axis=1, keepdims=True))
    p_past = jnp.exp(s_past - m)
    p_new = jnp.exp(s_new - m)
    total = jnp.sum(p_past, axis=1, keepdims=True) + jnp.sum(p_new, axis=1, keepdims=True)
    p_past = p_past.astype(BF16)
    acc = jnp.dot(p_new.astype(BF16), vpad_ref[...].astype(BF16), preferred_element_type=F32)
    for j, vp in enumerate(v_pages):
        acc = acc + jnp.dot(p_past[:, j * PAGE_ROWS:(j + 1) * PAGE_ROWS], vp[0].astype(BF16),
                            preferred_element_type=F32)
    o_ref[0] = acc / total


def fox_paged(q, k_new, v_new, logf_new, cache_k, cache_v, cache_logf, page_table, first_page):
    bsz = q.shape[0]

    def page(j):
        return lambda b, pt: (first_page + pt[b, j], 0, 0)

    new_spec = pl.BlockSpec((1, NEW_ROWS, FOX_HEAD_DIM), lambda b, pt: (b, 0, 0))
    in_specs = [new_spec, new_spec, new_spec, pl.BlockSpec((1, NEW_ROWS, 1), lambda b, pt: (b, 0, 0))]
    in_specs += [pl.BlockSpec((1, PAGE_ROWS, FOX_HEAD_DIM), page(j)) for j in range(N_PAGES)]
    in_specs += [pl.BlockSpec((1, PAGE_ROWS, FOX_HEAD_DIM), page(j)) for j in range(N_PAGES)]
    in_specs += [pl.BlockSpec((1, 1, PAGE_ROWS), page(j)) for j in range(N_PAGES)]
    return pl.pallas_call(
        _fox_paged_kernel,
        grid_spec=pltpu.PrefetchScalarGridSpec(
            num_scalar_prefetch=1,
            grid=(bsz,),
            in_specs=in_specs,
            out_specs=new_spec,
            scratch_shapes=[pltpu.VMEM((LANES, FOX_HEAD_DIM), F32), pltpu.VMEM((LANES, FOX_HEAD_DIM), F32)],
        ),
        out_shape=jax.ShapeDtypeStruct((bsz, NEW_ROWS, FOX_HEAD_DIM), F32),
        compiler_params=_params("parallel"),
        name="fox_paged",
    )(page_table, q, k_new, v_new, logf_new,
      *([cache_k] * N_PAGES), *([cache_v] * N_PAGES), *([cache_logf] * N_PAGES))


ROPE_HALF = RET_KD // 2
RET_SCALE = RET_KD ** -0.5
RET_LOG_GAMMA = [float(np.log1p(-np.float32(2.0 ** (-5.0 - h)))) for h in range(RET_HEADS)]


def _rope_kernel(f_ref, cos_ref, sin_ref, *, start):
    pos = (start + lax.broadcasted_iota(jnp.int32, cos_ref.shape, 0)).astype(F32)
    ang = pos * f_ref[...]
    cos_ref[...] = jnp.cos(ang)
    sin_ref[...] = jnp.sin(ang)


def rope_table(start, n):
    inv_freq = ROPE_BASE ** (-jnp.arange(ROPE_HALF, dtype=F32) / ROPE_HALF)
    shape = jax.ShapeDtypeStruct((n, ROPE_HALF), F32)
    return pl.pallas_call(
        functools.partial(_rope_kernel, start=start), out_shape=(shape, shape), name="rope_table",
    )(inv_freq.reshape(1, ROPE_HALF))


def _rotate(x, cos, sin):
    x1, x2 = x[:, :ROPE_HALF], x[:, ROPE_HALF:]
    return jnp.concatenate([x1 * cos - x2 * sin, x1 * sin + x2 * cos], axis=1)


def _group_norm_gate(o, g, gain):
    mu = jnp.mean(o, axis=-1, keepdims=True)
    var = jnp.mean(jnp.square(o - mu), axis=-1, keepdims=True)
    return g * jax.nn.sigmoid(g) * ((o - mu) * lax.rsqrt(var + EPS) * gain)


def _head_slabs(refs, h, per_slab):
    width = PROJ_TILE // per_slab
    return refs[h // per_slab], slice((h % per_slab) * width, (h % per_slab + 1) * width)


def _ret_seq_kernel(*refs):
    q_refs, k_refs, v_refs, g_refs = refs[0:2], refs[2:4], refs[4:8], refs[8:12]
    cos_ref, sin_ref, gain_ref, o_ref, s_out, s_ref = refs[12:]
    step = pl.program_id(1)

    @pl.when(step == 0)
    def _():
        s_ref[...] = jnp.zeros_like(s_ref)

    cos = cos_ref[...]
    sin = sin_ref[...]
    cl = RET_CHUNK
    i_col = lax.broadcasted_iota(jnp.int32, (cl, 1), 0).astype(F32)
    diff = i_col - lax.broadcasted_iota(jnp.int32, (1, cl), 1).astype(F32)
    for h in range(RET_HEADS):
        lg = RET_LOG_GAMMA[h]
        ref, sl = _head_slabs(q_refs, h, 4)
        q = _rotate(ref[0, :, sl], cos, sin).astype(BF16)
        ref, sl = _head_slabs(k_refs, h, 4)
        k = _rotate(ref[0, :, sl], cos, sin) * RET_SCALE
        ref, sl = _head_slabs(v_refs, h, 2)
        v = ref[0, :, sl].astype(BF16)
        ref, sl = _head_slabs(g_refs, h, 2)
        g = ref[0, :, sl]
        dmask = jnp.where(diff >= 0, jnp.exp(jnp.maximum(diff, 0.0) * lg), 0.0)
        inner = lax.dot_general(q, k.astype(BF16), NT_DIMS, preferred_element_type=F32) * dmask
        state = s_ref[h]
        out = jnp.dot(inner.astype(BF16), v, preferred_element_type=F32)
        out = out + jnp.dot(q, state.astype(BF16), preferred_element_type=F32) * jnp.exp((i_col + 1.0) * lg)
        k_dec = k * jnp.exp((cl - 1.0 - i_col) * lg)
        s_ref[h] = math.exp(cl * lg) * state + jnp.dot(k_dec.T.astype(BF16), v, preferred_element_type=F32)
        hs = slice(h * RET_VD, (h + 1) * RET_VD)
        o_ref[:, hs] = _group_norm_gate(out, g, gain_ref[:, hs]).astype(BF16)

    @pl.when(step == pl.num_programs(1) - 1)
    def _():
        s_out[0] = s_ref[...]


def ret_sequences(z, cos, sin, gain, bsz, seq_len):
    steps = seq_len // RET_CHUNK
    vd = RET_HEADS * RET_VD
    slab = lambda j: pl.BlockSpec((1, RET_CHUNK, PROJ_TILE), lambda b, c: (j, b * steps + c, 0))
    table = pl.BlockSpec((RET_CHUNK, ROPE_HALF), lambda b, c: (c, 0))
    return pl.pallas_call(
        _ret_seq_kernel,
        grid=(bsz, steps),
        in_specs=[slab(j) for j in range(12)] + [table, table, _whole((1, vd))],
        out_specs=(pl.BlockSpec((RET_CHUNK, vd), lambda b, c: (b * steps + c, 0)),
                   pl.BlockSpec((1, RET_HEADS, RET_KD, RET_VD), lambda b, c: (b, 0, 0, 0))),
        out_shape=(jax.ShapeDtypeStruct((bsz * seq_len, vd), BF16),
                   jax.ShapeDtypeStruct((bsz, RET_HEADS, RET_KD, RET_VD), F32)),
        scratch_shapes=[pltpu.VMEM((RET_HEADS, RET_KD, RET_VD), F32)],
        compiler_params=_params("parallel", "arbitrary"),
        name="ret_sequences",
    )(*([z] * 12), cos, sin, gain.reshape(1, vd))


RET_PAD = 128


RET_STEP_SEQS = 8 // DEC_SEQ


def _ret_step_kernel(z_ref, s0_ref, cos_ref, sin_ref, gain_ref, o_ref, s_out, qpad, kpad, vpad, gpad):
    cl = DEC_SEQ
    for ref in (qpad, kpad, vpad, gpad):
        ref[...] = jnp.zeros_like(ref)
    cos = cos_ref[...]
    sin = sin_ref[...]
    i_col = lax.broadcasted_iota(jnp.int32, (RET_PAD, 1), 0).astype(F32)
    diff = i_col - lax.broadcasted_iota(jnp.int32, (1, RET_PAD), 1).astype(F32)
    valid = (diff >= 0) & (i_col < cl)
    for b in range(RET_STEP_SEQS):
        tok = slice(b * cl, (b + 1) * cl)
        outs = []
        for h in range(RET_HEADS):
            lg = RET_LOG_GAMMA[h]
            qpad[0:cl, :] = z_ref[h // 4, tok, (h % 4) * RET_KD:(h % 4 + 1) * RET_KD]
            kpad[0:cl, :] = z_ref[2 + h // 4, tok, (h % 4) * RET_KD:(h % 4 + 1) * RET_KD]
            vpad[0:cl, :] = z_ref[4 + h // 2, tok, (h % 2) * RET_VD:(h % 2 + 1) * RET_VD]
            gpad[0:cl, :] = z_ref[8 + h // 2, tok, (h % 2) * RET_VD:(h % 2 + 1) * RET_VD]
            q = _rotate(qpad[...], cos, sin).astype(BF16)
            k = _rotate(kpad[...], cos, sin) * RET_SCALE
            v = vpad[...].astype(BF16)
            dmask = jnp.where(valid, jnp.exp(jnp.maximum(diff, 0.0) * lg), 0.0)
            inner = lax.dot_general(q, k.astype(BF16), NT_DIMS, preferred_element_type=F32) * dmask
            state = s0_ref[b, h]
            out = jnp.dot(inner.astype(BF16), v, preferred_element_type=F32)
            out = out + jnp.dot(q, state.astype(BF16), preferred_element_type=F32) * jnp.exp((i_col + 1.0) * lg)
            k_dec = k * jnp.exp((cl - 1.0 - i_col) * lg)
            s_out[b, h] = math.exp(cl * lg) * state + jnp.dot(k_dec.T.astype(BF16), v, preferred_element_type=F32)
            hs = slice(h * RET_VD, (h + 1) * RET_VD)
            outs.append(_group_norm_gate(out[0:8], gpad[0:8, :], gain_ref[:, hs]))
        o_ref[tok, :] = jnp.concatenate(outs, axis=1)[0:cl]


def ret_steps(z, first_row, s0, cos, sin, gain):
    bsz = s0.shape[0]
    vd = RET_HEADS * RET_VD
    rows = RET_STEP_SEQS * DEC_SEQ
    state_spec = pl.BlockSpec((RET_STEP_SEQS, RET_HEADS, RET_KD, RET_VD), lambda i: (i, 0, 0, 0))
    return pl.pallas_call(
        _ret_step_kernel,
        grid=(bsz // RET_STEP_SEQS,),
        in_specs=[pl.BlockSpec((12, rows, PROJ_TILE), lambda i: (0, first_row // rows + i, 0)), state_spec,
                  _whole((RET_PAD, ROPE_HALF)), _whole((RET_PAD, ROPE_HALF)), _whole((1, vd))],
        out_specs=(pl.BlockSpec((rows, vd), lambda i: (i, 0)), state_spec),
        out_shape=(jax.ShapeDtypeStruct((bsz * DEC_SEQ, vd), F32),
                   jax.ShapeDtypeStruct(s0.shape, F32)),
        scratch_shapes=[pltpu.VMEM((RET_PAD, RET_KD), F32), pltpu.VMEM((RET_PAD, RET_KD), F32),
                        pltpu.VMEM((RET_PAD, RET_VD), F32), pltpu.VMEM((RET_PAD, RET_VD), F32)],
        compiler_params=_params("parallel"),
        name="ret_steps",
    )(z, s0, cos, sin, gain.reshape(1, vd))


def _even_mixer(h, gain, e, cache_k, cache_v, cache_logf, state_s5_re, state_s5_im, page_table, w_in, b_f,
                s5_w):
    z, zf = norm_proj(h, gain, w_in, 4 * PROJ_TILE, narrow=True)

    s5_p, hr_p, hi_p = s5_sequences(z, s5_w, BATCH, SEQ)
    u_s = jnp.stack([z[0, N_PROMPT + t::DEC_SEQ] for t in range(DEC_SEQ)])
    s5_s, hr_s, hi_s = s5_steps(u_s, state_s5_re.reshape(DEC_BATCH, S5_LANES),
                                state_s5_im.reshape(DEC_BATCH, S5_LANES), s5_w)
    s5_s = s5_s.transpose(1, 0, 2).reshape(N_SAMPLE, S5_WIDTH)

    lf_p, c_p, ct_p = fox_gates_seq(zf, b_f, BATCH, SEQ)
    lf_s = fox_gates(zf[N_PROMPT:], b_f)[:, :FOX_HEADS]
    k_p, v_p, q_s, k_s, v_s = head_rows(z)
    att_p = fox_sequences(z, c_p, ct_p, BATCH, SEQ)
    pool = cache_k.shape[1]
    new_rows = lambda t: t.reshape(DEC_BATCH, NEW_ROWS, FOX_HEAD_DIM)
    att_s = fox_paged(new_rows(q_s), new_rows(k_s), new_rows(v_s), lf_s.reshape(DEC_BATCH, NEW_ROWS, 1),
                      cache_k.reshape(-1, PAGE_ROWS, FOX_HEAD_DIM), cache_v.reshape(-1, PAGE_ROWS, FOX_HEAD_DIM),
                      cache_logf.reshape(-1, 1, PAGE_ROWS), page_table, e * pool)
    att_s = att_s.reshape(N_SAMPLE, FOX_WIDTH)

    heads = lambda t, b, s: t.reshape(b, s, FOX_HEADS, FOX_HEAD_DIM)
    grid = lambda t: t.reshape(-1, S5_GROUPS, S5_STATE)
    st_p = (heads(k_p, BATCH, SEQ), heads(v_p, BATCH, SEQ),
            lf_p[:, :FOX_HEADS].reshape(BATCH, SEQ, FOX_HEADS), grid(hr_p), grid(hi_p))
    st_s = (heads(k_s, DEC_BATCH, DEC_SEQ), heads(v_s, DEC_BATCH, DEC_SEQ),
            lf_s.reshape(DEC_BATCH, DEC_SEQ, FOX_HEADS), grid(hr_s), grid(hi_s))
    return ([s5_p, att_p], [s5_s, att_s]), st_p, st_s


def _odd_mixer(h, gain, w_in, gn_gain, state, tables):
    z = norm_proj(h, gain, w_in, 12 * PROJ_TILE)
    (cos_p, sin_p), (cos_s, sin_s) = tables
    o_p, st_p = ret_sequences(z, cos_p, sin_p, gn_gain, BATCH, SEQ)
    o_s, st_s = ret_steps(z, N_PROMPT, state, cos_s, sin_s, gn_gain)
    return ([o_p], [o_s]), st_p, st_s


def kernel(x_prompt, x_sample, cache_k, cache_v, cache_logf, state_s5_re, state_s5_im, state_ret, page_table, norm_gain, w_ffn_gate, w_ffn_up, w_ffn_down, w_in_even, b_forget, s5_lam_re, s5_lam_im, s5_b_re, s5_b_im, s5_c_re, s5_c_im, s5_d, s5_log_dt, w_glu, b_glu, w_out_even, w_in_odd, ret_gn_gain, w_out_odd, final_norm):
    x = jnp.concatenate([x_prompt.reshape(N_PROMPT, D_MODEL), x_sample.reshape(N_SAMPLE, D_MODEL)], axis=0)
    tables = (rope_table(0, SEQ), rope_table(PAST_LEN, RET_PAD))
    ev_p, ev_s, od_p, od_s = [], [], [], []
    for layer in range(DEPTH):
        h = ffn(x, norm_gain[layer, 0], w_ffn_gate, w_ffn_up, w_ffn_down, layer, 0)
        if layer % 2 == 0:
            e = layer // 2
            s5_w = s5_weights(s5_lam_re[e], s5_lam_im[e], s5_log_dt[e], s5_b_re[e], s5_b_im[e], s5_c_re[e],
                              s5_c_im[e], s5_d[e], w_glu[e], b_glu[e])
            parts, st_p, st_s = _even_mixer(h, norm_gain[layer, 1], e, cache_k, cache_v, cache_logf,
                                            state_s5_re[e], state_s5_im[e], page_table, w_in_even[e],
                                            b_forget[e], s5_w)
            ev_p.append(st_p)
            ev_s.append(st_s)
            w_out = w_out_even[e]
        else:
            o = layer // 2
            parts, st_p, st_s = _odd_mixer(h, norm_gain[layer, 1], w_in_odd[o], ret_gn_gain[o], state_ret[o],
                                           tables)
            od_p.append(st_p)
            od_s.append(st_s)
            w_out = w_out_odd[o]
        h = out_proj(*parts, w_out, h)
        x = ffn(h, norm_gain[layer, 2], w_ffn_gate, w_ffn_up, w_ffn_down, layer, 1)

    y_prompt, y_sample = final_rms(x, final_norm)
    y_prompt = y_prompt.reshape(BATCH, SEQ, D_MODEL)
    y_sample = y_sample.reshape(DEC_BATCH, DEC_SEQ, D_MODEL)
    stack = lambda sts, i: jnp.stack([s[i] for s in sts])
    return (y_prompt, y_sample,
            stack(ev_p, 0), stack(ev_p, 1), stack(ev_p, 2), stack(ev_p, 3), stack(ev_p, 4), jnp.stack(od_p),
            stack(ev_s, 0), stack(ev_s, 1), stack(ev_s, 2), stack(ev_s, 3), stack(ev_s, 4), jnp.stack(od_s))
```

```python
import functools
import math

import jax
import jax.numpy as jnp
import numpy as np
from jax import lax
from jax.experimental import pallas as pl
from jax.experimental.pallas import tpu as pltpu

D_MODEL = 2048
BATCH = 2
SEQ = 4096
DEPTH = 2
DEC_BATCH = 128
DEC_SEQ = 4
PAST_LEN = 2048
PAGE_SIZE = 128
N_PAGES = PAST_LEN // PAGE_SIZE
S5_WIDTH = 1024
S5_GROUP = 16
S5_GROUPS = 64
S5_STATE = 64
FOX_WIDTH = 1024
FOX_HEAD_DIM = 128
FOX_HEADS = 8
NEG_INF = -1e30
RET_HEADS = 8
RET_KD = 256
RET_VD = 512
RET_CHUNK = 128
ROPE_BASE = 10000.0
D_FF = 5504
EPS = 1e-6

N_PROMPT = BATCH * SEQ
N_SAMPLE = DEC_BATCH * DEC_SEQ
N_TOK = N_PROMPT + N_SAMPLE

ROW_TILE = 1088
FF_TILE = 256
LANES = 128
VMEM_LIMIT = 56 * 1024 * 1024
FFN_VMEM_LIMIT = 62 * 1024 * 1024

F32 = jnp.float32
BF16 = jnp.bfloat16
NT_DIMS = (((1,), (1,)), ((), ()))


def _rms_rows(x, gain):
    return x * lax.rsqrt(jnp.mean(x * x, axis=-1, keepdims=True) + EPS) * gain


def _whole(shape):
    return pl.BlockSpec(shape, lambda *_: (0,) * len(shape))


def _params(*semantics):
    return pltpu.CompilerParams(dimension_semantics=semantics, vmem_limit_bytes=VMEM_LIMIT)


FFN_GROUP = 2


def _ffn_kernel(x_ref, g_ref, wg_ref, wu_ref, wd_ref, o_ref, h_ref, *, d_ff):
    step = pl.program_id(1)
    tf = wg_ref.shape[1]

    @pl.when(step < FFN_GROUP)
    def _():
        rows = pl.ds(pl.multiple_of(step * ROW_TILE, ROW_TILE), ROW_TILE)
        x = x_ref[...]
        h_ref[rows, :] = _rms_rows(x, g_ref[...]).astype(BF16)
        o_ref[rows, :] = x

    @pl.when(step >= FFN_GROUP)
    def _():
        first = (step - FFN_GROUP) * tf
        wg = wg_ref[...].astype(BF16)
        wu = wu_ref[...].astype(BF16)
        col_ok = first + lax.broadcasted_iota(jnp.int32, (1, tf), 1) < d_ff
        row_ok = first + lax.broadcasted_iota(jnp.int32, (tf, 1), 0) < d_ff
        wd = jnp.where(row_ok, wd_ref[...], 0.0).astype(BF16)
        for t in range(FFN_GROUP):
            rows = slice(t * ROW_TILE, (t + 1) * ROW_TILE)
            h = h_ref[rows, :]
            a = jnp.dot(h, wg, preferred_element_type=F32)
            b = jnp.dot(h, wu, preferred_element_type=F32)
            c = jnp.where(col_ok, a * jax.nn.sigmoid(a) * b, 0.0).astype(BF16)
            o_ref[rows, :] += 0.5 * jnp.dot(c, wd, preferred_element_type=F32)


def ffn(x, gain, wg, wu, wd, layer, which):
    m, d = x.shape
    d_ff = wg.shape[3]
    group_rows = FFN_GROUP * ROW_TILE
    w_tile = lambda r, s: jnp.maximum(s - FFN_GROUP, 0)
    return pl.pallas_call(
        functools.partial(_ffn_kernel, d_ff=d_ff),
        grid=(m // group_rows, FFN_GROUP + pl.cdiv(d_ff, FF_TILE)),
        in_specs=[
            pl.BlockSpec((ROW_TILE, d), lambda r, s: (FFN_GROUP * r + jnp.minimum(s, FFN_GROUP - 1), 0)),
            pl.BlockSpec((1, d), lambda r, s: (0, 0)),
            pl.BlockSpec((None, None, d, FF_TILE), lambda r, s: (layer, which, 0, w_tile(r, s))),
            pl.BlockSpec((None, None, d, FF_TILE), lambda r, s: (layer, which, 0, w_tile(r, s))),
            pl.BlockSpec((None, None, FF_TILE, d), lambda r, s: (layer, which, w_tile(r, s), 0)),
        ],
        out_specs=pl.BlockSpec((group_rows, d), lambda r, s: (r, 0), pipeline_mode=pl.Buffered(1)),
        out_shape=jax.ShapeDtypeStruct((m, d), F32),
        scratch_shapes=[pltpu.VMEM((group_rows, d), BF16)],
        compiler_params=pltpu.CompilerParams(
            dimension_semantics=("parallel", "arbitrary"), vmem_limit_bytes=FFN_VMEM_LIMIT),
        name="ffn",
    )(x, gain.reshape(1, d), wg, wu, wd)


SPLIT_TILE = 512
N_PROMPT_TILES = N_PROMPT // SPLIT_TILE


def _split_specs(block):
    zeros = (0,) * (len(block) - 1)
    return (pl.BlockSpec(block, lambda i: (jnp.minimum(i, N_PROMPT_TILES - 1),) + zeros),
            pl.BlockSpec(block, lambda i: (0,) + zeros))


def _final_norm_kernel(x_ref, g_ref, yp_ref, ys_ref):
    y = _rms_rows(x_ref[...], g_ref[...])

    @pl.when(pl.program_id(0) < N_PROMPT_TILES)
    def _():
        yp_ref[...] = y

    @pl.when(pl.program_id(0) == N_PROMPT_TILES)
    def _():
        ys_ref[...] = y


def final_rms(x, gain):
    d = x.shape[1]
    return pl.pallas_call(
        _final_norm_kernel,
        grid=(N_PROMPT_TILES + 1,),
        in_specs=[pl.BlockSpec((SPLIT_TILE, d), lambda i: (i, 0)), _whole((1, d))],
        out_specs=_split_specs((SPLIT_TILE, d)),
        out_shape=(jax.ShapeDtypeStruct((N_PROMPT, d), F32), jax.ShapeDtypeStruct((N_SAMPLE, d), F32)),
        compiler_params=_params("arbitrary"),
        name="final_norm",
    )(x, gain.reshape(1, d))


def _head_rows_kernel(q_ref, k_ref, v_ref, kp_ref, vp_ref, qs_ref, ks_ref, vs_ref):
    def put(pairs):
        for src, dst in pairs:
            dst[...] = src[0].reshape(SPLIT_TILE, FOX_HEADS, FOX_HEAD_DIM)

    @pl.when(pl.program_id(0) < N_PROMPT_TILES)
    def _():
        put([(k_ref, kp_ref), (v_ref, vp_ref)])

    @pl.when(pl.program_id(0) == N_PROMPT_TILES)
    def _():
        put([(q_ref, qs_ref), (k_ref, ks_ref), (v_ref, vs_ref)])


def head_rows(z):
    block = (SPLIT_TILE, FOX_HEADS, FOX_HEAD_DIM)
    shape = lambda n: jax.ShapeDtypeStruct((n, FOX_HEADS, FOX_HEAD_DIM), F32)
    kp_spec, ks_spec = _split_specs(block)
    return pl.pallas_call(
        _head_rows_kernel,
        grid=(N_PROMPT_TILES + 1,),
        in_specs=[pl.BlockSpec((1, SPLIT_TILE, FOX_WIDTH), lambda i: (1, N_PROMPT_TILES, 0)),
                  pl.BlockSpec((1, SPLIT_TILE, FOX_WIDTH), lambda i: (2, i, 0)),
                  pl.BlockSpec((1, SPLIT_TILE, FOX_WIDTH), lambda i: (3, i, 0))],
        out_specs=(kp_spec, kp_spec, ks_spec, ks_spec, ks_spec),
        out_shape=(shape(N_PROMPT), shape(N_PROMPT), shape(N_SAMPLE), shape(N_SAMPLE), shape(N_SAMPLE)),
        compiler_params=_params("arbitrary"),
        name="head_rows",
    )(z, z, z)


PROJ_TILE = 1024


def _norm_proj_kernel(x_ref, g_ref, w_ref, *rest, narrow):
    if narrow:
        wn_ref, z_ref, zn_ref, h_ref = rest
    else:
        z_ref, h_ref = rest
    j = pl.program_id(1)

    @pl.when(j == 0)
    def _():
        h_ref[...] = _rms_rows(x_ref[...], g_ref[...]).astype(BF16)
        if narrow:
            zn_ref[...] = jnp.dot(h_ref[...], wn_ref[...].astype(BF16), preferred_element_type=F32)

    z_ref[0] = jnp.dot(h_ref[...], w_ref[...].astype(BF16), preferred_element_type=F32)


def norm_proj(x, gain, w, n_wide, narrow=False):
    m, d = x.shape
    nj = n_wide // PROJ_TILE
    in_specs = [
        pl.BlockSpec((ROW_TILE, d), lambda i, j: (i, 0)),
        pl.BlockSpec((1, d), lambda i, j: (0, 0)),
        pl.BlockSpec((d, PROJ_TILE), lambda i, j: (0, j)),
    ]
    args = [x, gain.reshape(1, d), w]
    out_shape = jax.ShapeDtypeStruct((nj, m, PROJ_TILE), F32)
    out_specs = pl.BlockSpec((1, ROW_TILE, PROJ_TILE), lambda i, j: (j, i, 0))
    if narrow:
        in_specs.append(pl.BlockSpec((d, LANES), lambda i, j: (0, n_wide // LANES)))
        args.append(w)
        out_shape = (out_shape, jax.ShapeDtypeStruct((m, LANES), F32))
        out_specs = (out_specs, pl.BlockSpec((ROW_TILE, LANES), lambda i, j: (i, 0)))
    return pl.pallas_call(
        functools.partial(_norm_proj_kernel, narrow=narrow),
        grid=(m // ROW_TILE, nj),
        in_specs=in_specs,
        out_specs=out_specs,
        out_shape=out_shape,
        scratch_shapes=[pltpu.VMEM((ROW_TILE, d), BF16)],
        compiler_params=_params("parallel", "arbitrary"),
        name="norm_proj_narrow" if narrow else "norm_proj",
    )(*args)


OUT_TILE = 512


def _out_proj_kernel(*refs, n_parts):
    a_refs, w_refs, r_ref, o_ref = refs[:n_parts], refs[n_parts:2 * n_parts], refs[2 * n_parts], refs[-1]
    acc = r_ref[...]
    for a_ref, w_ref in zip(a_refs, w_refs):
        acc = acc + jnp.dot(a_ref[...].astype(BF16), w_ref[...].astype(BF16), preferred_element_type=F32)
    o_ref[...] = acc


def _out_proj_rows(parts, w, res, row_tile, first_tile, prev=None):
    rows, k = parts[0].shape
    n = w.shape[1]
    n_parts = len(parts)
    out_block = pl.BlockSpec((row_tile, OUT_TILE), lambda i, j: (first_tile + i, j))
    in_specs = ([pl.BlockSpec((row_tile, k), lambda i, j: (i, 0))] * n_parts
                + [pl.BlockSpec((k, OUT_TILE), functools.partial(lambda i, j, p: (p, j), p=p))
                   for p in range(n_parts)]
                + [out_block])
    args = [*parts, *([w] * n_parts), res]
    aliases = {}
    if prev is not None:
        in_specs.append(pl.BlockSpec(memory_space=pl.ANY))
        args.append(prev)
        aliases = {len(args) - 1: 0}
    return pl.pallas_call(
        functools.partial(_out_proj_kernel, n_parts=n_parts),
        grid=(rows // row_tile, n // OUT_TILE),
        in_specs=in_specs,
        out_specs=out_block,
        out_shape=jax.ShapeDtypeStruct((res.shape[0], n), F32),
        input_output_aliases=aliases,
        compiler_params=_params("parallel", "arbitrary"),
        name="out_proj",
    )(*args)


OUT_ROWS = 1024


def out_proj(parts_prompt, parts_sample, w, res):
    out = _out_proj_rows(parts_prompt, w, res, OUT_ROWS, 0)
    return _out_proj_rows(parts_sample, w, res, N_SAMPLE, N_PROMPT // N_SAMPLE, prev=out)


S5_LANES = S5_GROUPS * S5_STATE
S5_BLOCKS = 4
S5_BLOCK_CH = S5_WIDTH // S5_BLOCKS
S5_BLOCK_ST = S5_LANES // S5_BLOCKS
SCAN_LANES = 512
S5_TILE = 256


def _s5_param_kernel(lr_ref, li_ref, ldt_ref, bre_ref, bim_ref, tab_ref, bbr_ref, bbi_ref):
    lr = lr_ref[...]
    li = li_ref[...]
    dt = jnp.exp(ldt_ref[...])
    dec = lr * dt
    ang = li * dt
    er = jnp.exp(dec)
    xr = er * jnp.cos(ang) - 1.0
    xi = er * jnp.sin(ang)
    den = lr * lr + li * li
    cr = (xr * lr + xi * li) / den
    ci = (xi * lr - xr * li) / den
    bre = bre_ref[...]
    bim = bim_ref[...]
    bbr_ref[...] = cr * bre - ci * bim
    bbi_ref[...] = cr * bim + ci * bre

    row = lax.broadcasted_iota(jnp.int32, (8, S5_LANES), 0)

    def power(k):
        mag = jnp.exp(k * dec)
        return mag * jnp.cos(k * ang), mag * jnp.sin(k * ang)

    for idx, s in enumerate((1, 2, 4)):
        pr, pi = power(float(s))
        tab_ref[2 * idx] = jnp.where(row >= s, pr, 0.0)
        tab_ref[2 * idx + 1] = jnp.where(row >= s, pi, 0.0)
    pr, pi = power((row + 1).astype(F32))
    tab_ref[6] = pr
    tab_ref[7] = pi


def s5_params(lam_re, lam_im, log_dt, b_re, b_im):
    flat = lambda t: t.reshape(1, S5_LANES)
    ldt = jnp.broadcast_to(log_dt[:, None], (S5_GROUPS, S5_STATE))
    to_rows = lambda t: t.reshape(S5_LANES, S5_GROUP).T
    return pl.pallas_call(
        _s5_param_kernel,
        out_shape=(jax.ShapeDtypeStruct((8, 8, S5_LANES), F32),
                   jax.ShapeDtypeStruct((S5_GROUP, S5_LANES), F32),
                   jax.ShapeDtypeStruct((S5_GROUP, S5_LANES), F32)),
        name="s5_params",
    )(flat(lam_re), flat(lam_im), flat(ldt), to_rows(b_re), to_rows(b_im))


def _block_diag_in(bbr, bbi):
    bb = jnp.stack([bbr, bbi]).reshape(2, S5_GROUP, S5_BLOCKS, 16, S5_STATE)
    eye = jnp.eye(16, dtype=F32)
    t = bb[:, :, :, :, None, :] * eye[None, None, None, :, :, None]
    return t.transpose(2, 3, 1, 0, 4, 5).reshape(S5_BLOCKS, S5_BLOCK_CH, 2 * S5_BLOCK_ST).astype(BF16)


def _block_diag_out(c):
    cc = c.reshape(S5_BLOCKS, 16, S5_GROUP, S5_STATE)
    eye = jnp.eye(16, dtype=F32)
    t = cc.transpose(0, 1, 3, 2)[:, :, :, None, :] * eye[None, :, None, :, None]
    return t.reshape(S5_BLOCKS, S5_BLOCK_ST, S5_BLOCK_CH).astype(BF16)


def _s5_drive(ub, bbd_ref, hr_ref, hi_ref):
    for blk in range(S5_BLOCKS):
        bu = jnp.dot(ub[:, blk * S5_BLOCK_CH:(blk + 1) * S5_BLOCK_CH], bbd_ref[blk], preferred_element_type=F32)
        hr_ref[:, blk * S5_BLOCK_ST:(blk + 1) * S5_BLOCK_ST] = bu[:, :S5_BLOCK_ST]
        hi_ref[:, blk * S5_BLOCK_ST:(blk + 1) * S5_BLOCK_ST] = bu[:, S5_BLOCK_ST:]


def _s5_readout(u, hr_ref, hi_ref, cr_ref, ci_ref, d_ref, wg_ref, bg_ref):
    ys = []
    for blk in range(S5_BLOCKS):
        sl = slice(blk * S5_BLOCK_ST, (blk + 1) * S5_BLOCK_ST)
        yr = jnp.dot(hr_ref[:, sl].astype(BF16), cr_ref[blk], preferred_element_type=F32)
        yi = jnp.dot(hi_ref[:, sl].astype(BF16), ci_ref[blk], preferred_element_type=F32)
        ys.append(yr - yi)
    y = jax.nn.gelu(jnp.concatenate(ys, axis=1) + d_ref[...] * u)
    gate = jax.nn.sigmoid(jnp.dot(y.astype(BF16), wg_ref[...], preferred_element_type=F32) + bg_ref[...])
    return (y * gate).astype(BF16)


def _s5_seq_kernel(u_ref, tab_ref, bbd_ref, cr_ref, ci_ref, d_ref, wg_ref, bg_ref,
                   o_ref, hr_out, hi_out, hr_ref, hi_ref, car_r, car_i):
    step = pl.program_id(1)

    @pl.when(step == 0)
    def _():
        car_r[...] = jnp.zeros_like(car_r)
        car_i[...] = jnp.zeros_like(car_i)

    u = u_ref[0]
    ub = u.astype(BF16)
    n_tiles = u.shape[0] // 8
    ys = []
    for blk in range(S5_BLOCKS):
        st = slice(blk * S5_BLOCK_ST, (blk + 1) * S5_BLOCK_ST)
        bu = jnp.dot(ub[:, blk * S5_BLOCK_CH:(blk + 1) * S5_BLOCK_CH], bbd_ref[blk], preferred_element_type=F32)
        hr_ref[:, st] = bu[:, :S5_BLOCK_ST]
        hi_ref[:, st] = bu[:, S5_BLOCK_ST:]
        for c in range(S5_BLOCK_ST // SCAN_LANES):
            lo = blk * S5_BLOCK_ST + c * SCAN_LANES
            sl = slice(lo, lo + SCAN_LANES)
            tabs = [tab_ref[k, :, sl] for k in range(8)]
            cr, ci = car_r[:, sl], car_i[:, sl]
            for i in range(n_tiles):
                rows = slice(i * 8, (i + 1) * 8)
                xr = hr_ref[rows, sl]
                xi = hi_ref[rows, sl]
                for k, s in enumerate((1, 2, 4)):
                    ar, ai = tabs[2 * k], tabs[2 * k + 1]
                    rr = pltpu.roll(xr, s, 0)
                    ri = pltpu.roll(xi, s, 0)
                    xr, xi = xr + ar * rr - ai * ri, xi + ar * ri + ai * rr
                pr, pi = tabs[6], tabs[7]
                hr = xr + pr * cr - pi * ci
                hi = xi + pr * ci + pi * cr
                hr_ref[rows, sl] = hr
                hi_ref[rows, sl] = hi
                cr, ci = jnp.broadcast_to(hr[7:8], hr.shape), jnp.broadcast_to(hi[7:8], hi.shape)
            car_r[:, sl] = cr
            car_i[:, sl] = ci
        yr = jnp.dot(hr_ref[:, st].astype(BF16), cr_ref[blk], preferred_element_type=F32)
        yi = jnp.dot(hi_ref[:, st].astype(BF16), ci_ref[blk], preferred_element_type=F32)
        ys.append(yr - yi)
    y = jax.nn.gelu(jnp.concatenate(ys, axis=1) + d_ref[...] * u)
    gate = jax.nn.sigmoid(jnp.dot(y.astype(BF16), wg_ref[...], preferred_element_type=F32) + bg_ref[...])
    o_ref[...] = (y * gate).astype(BF16)

    @pl.when(step == pl.num_programs(1) - 1)
    def _():
        hr_out[0] = car_r[0:1, :]
        hi_out[0] = car_i[0:1, :]


def _s5_weight_specs():
    return [_whole((8, 8, S5_LANES)), _whole((S5_BLOCKS, S5_BLOCK_CH, 2 * S5_BLOCK_ST)),
            _whole((S5_BLOCKS, S5_BLOCK_ST, S5_BLOCK_CH)), _whole((S5_BLOCKS, S5_BLOCK_ST, S5_BLOCK_CH)),
            _whole((1, S5_WIDTH)), _whole((S5_WIDTH, S5_WIDTH)), _whole((1, S5_WIDTH))]


def s5_sequences(z, weights, bsz, seq_len):
    steps = seq_len // S5_TILE
    state = jax.ShapeDtypeStruct((bsz, 1, S5_LANES), F32)
    state_spec = pl.BlockSpec((1, 1, S5_LANES), lambda b, s: (b, 0, 0))
    return pl.pallas_call(
        _s5_seq_kernel,
        grid=(bsz, steps),
        in_specs=[pl.BlockSpec((1, S5_TILE, S5_WIDTH), lambda b, s: (0, b * steps + s, 0))] + _s5_weight_specs(),
        out_specs=(pl.BlockSpec((S5_TILE, S5_WIDTH), lambda b, s: (b * steps + s, 0)), state_spec, state_spec),
        out_shape=(jax.ShapeDtypeStruct((bsz * seq_len, S5_WIDTH), BF16), state, state),
        scratch_shapes=[pltpu.VMEM((S5_TILE, S5_LANES), F32), pltpu.VMEM((S5_TILE, S5_LANES), F32),
                        pltpu.VMEM((8, S5_LANES), F32), pltpu.VMEM((8, S5_LANES), F32)],
        compiler_params=_params("parallel", "arbitrary"),
        name="s5_sequences",
    )(z, *weights)


def _s5_step_kernel(u_ref, h0r_ref, h0i_ref, tab_ref, bbd_ref, cr_ref, ci_ref, d_ref, wg_ref, bg_ref,
                    o_ref, hr_ref, hi_ref, bur_ref, bui_ref):
    @pl.when(pl.program_id(0) == 0)
    def _():
        hr_ref[...] = h0r_ref[...]
        hi_ref[...] = h0i_ref[...]

    u = u_ref[0]
    _s5_drive(u.astype(BF16), bbd_ref, bur_ref, bui_ref)
    ar = tab_ref[6, 0:1, :]
    ai = tab_ref[7, 0:1, :]
    hr = hr_ref[...]
    hi = hi_ref[...]
    hr_ref[...] = ar * hr - ai * hi + bur_ref[...]
    hi_ref[...] = ar * hi + ai * hr + bui_ref[...]
    o_ref[0] = _s5_readout(u, hr_ref, hi_ref, cr_ref, ci_ref, d_ref, wg_ref, bg_ref)


def s5_steps(u, h0_re, h0_im, weights):
    steps, rows, _ = u.shape
    state = jax.ShapeDtypeStruct((rows, S5_LANES), F32)
    return pl.pallas_call(
        _s5_step_kernel,
        grid=(steps,),
        in_specs=[pl.BlockSpec((1, rows, S5_WIDTH), lambda t: (t, 0, 0)),
                  _whole((rows, S5_LANES)), _whole((rows, S5_LANES))] + _s5_weight_specs(),
        out_specs=(pl.BlockSpec((1, rows, S5_WIDTH), lambda t: (t, 0, 0)),
                   _whole((rows, S5_LANES)), _whole((rows, S5_LANES))),
        out_shape=(jax.ShapeDtypeStruct((steps, rows, S5_WIDTH), BF16), state, state),
        scratch_shapes=[pltpu.VMEM((rows, S5_LANES), F32), pltpu.VMEM((rows, S5_LANES), F32)],
        compiler_params=_params("arbitrary"),
        name="s5_steps",
    )(u, h0_re, h0_im, *weights)


def s5_weights(lam_re, lam_im, log_dt, b_re, b_im, c_re, c_im, d_skip, w_glu, b_glu):
    tab, bbr, bbi = s5_params(lam_re, lam_im, log_dt, b_re, b_im)
    return (tab, _block_diag_in(bbr, bbi), _block_diag_out(c_re), _block_diag_out(c_im),
            d_skip.reshape(1, S5_WIDTH), w_glu.astype(BF16), b_glu.reshape(1, S5_WIDTH))


def _lane_pad_bias(b_f):
    return jnp.pad(b_f, (0, LANES - FOX_HEADS)).reshape(1, LANES)


def _log_forget(zf_ref, bf_ref):
    lane = lax.broadcasted_iota(jnp.int32, zf_ref.shape, 1)
    return jnp.where(lane < FOX_HEADS, jax.nn.log_sigmoid(zf_ref[...] + bf_ref[...]), 0.0)


def _gate_seq_kernel(zf_ref, bf_ref, lf_ref, c_ref, ct_ref):
    lf = _log_forget(zf_ref, bf_ref)
    lf_ref[...] = lf
    row = lax.broadcasted_iota(jnp.int32, lf.shape, 0)
    c = lf
    s = 1
    while s < lf.shape[0]:
        c = c + jnp.where(row >= s, pltpu.roll(c, s, 0), 0.0)
        s *= 2
    c_ref[0] = c
    ct_ref[0] = c.T[:FOX_HEADS]


def fox_gates_seq(zf, b_f, bsz, seq_len):
    return pl.pallas_call(
        _gate_seq_kernel,
        grid=(bsz,),
        in_specs=[pl.BlockSpec((seq_len, LANES), lambda b: (b, 0)), _whole((1, LANES))],
        out_specs=(pl.BlockSpec((seq_len, LANES), lambda b: (b, 0)),
                   pl.BlockSpec((1, seq_len, LANES), lambda b: (b, 0, 0)),
                   pl.BlockSpec((1, FOX_HEADS, seq_len), lambda b: (b, 0, 0))),
        out_shape=(jax.ShapeDtypeStruct((bsz * seq_len, LANES), F32),
                   jax.ShapeDtypeStruct((bsz, seq_len, LANES), F32),
                   jax.ShapeDtypeStruct((bsz, FOX_HEADS, seq_len), F32)),
        compiler_params=_params("parallel"),
        name="fox_gates_seq",
    )(zf, _lane_pad_bias(b_f))


def _gate_kernel(zf_ref, bf_ref, lf_ref):
    lf_ref[...] = _log_forget(zf_ref, bf_ref)


def fox_gates(zf, b_f):
    return pl.pallas_call(
        _gate_kernel,
        out_shape=jax.ShapeDtypeStruct(zf.shape, F32),
        name="fox_gates",
    )(zf, _lane_pad_bias(b_f))


FOX_TILE = 1024
FOX_SCALE = FOX_HEAD_DIM ** -0.5


def _fox_seq_kernel(q_ref, k_ref, v_ref, c_ref, ct_ref, o_ref):
    h = pl.program_id(1)
    i = pl.program_id(2)
    q = q_ref[0].astype(BF16)
    lane = lax.broadcasted_iota(jnp.int32, c_ref.shape[1:], 1)
    cq = jnp.sum(jnp.where(lane == h, c_ref[0], 0.0), axis=1, keepdims=True)

    def chunk(j, carry, diagonal):
        m, l, acc = carry
        rows = pl.ds(pl.multiple_of(j * FOX_TILE, FOX_TILE), FOX_TILE)
        kc = k_ref[0, rows, :].astype(BF16)
        vc = v_ref[0, rows, :].astype(BF16)
        s = lax.dot_general(q, kc, NT_DIMS, preferred_element_type=F32) * FOX_SCALE
        s = s + cq - ct_ref[0, h, pl.ds(j, 1), :]
        if diagonal:
            qi = lax.broadcasted_iota(jnp.int32, s.shape, 0)
            ki = lax.broadcasted_iota(jnp.int32, s.shape, 1)
            s = jnp.where(qi >= ki, s, NEG_INF)
        m_new = jnp.maximum(m, jnp.max(s, axis=1, keepdims=True))
        alpha = jnp.exp(m - m_new)
        p = jnp.exp(s - m_new)
        l = alpha * l + jnp.sum(p, axis=1, keepdims=True)
        acc = alpha * acc + jnp.dot(p.astype(BF16), vc, preferred_element_type=F32)
        return m_new, l, acc

    init = (jnp.full((FOX_TILE, 1), NEG_INF, F32), jnp.zeros((FOX_TILE, 1), F32),
            jnp.zeros((FOX_TILE, FOX_HEAD_DIM), F32))
    carry = lax.fori_loop(0, i, functools.partial(chunk, diagonal=False), init)
    _, l, acc = chunk(i, carry, diagonal=True)
    o_ref[...] = (acc / l).astype(BF16)


def fox_sequences(z, c, ct, bsz, seq_len):
    nq = seq_len // FOX_TILE
    ct = ct.reshape(bsz, FOX_HEADS, nq, FOX_TILE)
    return pl.pallas_call(
        _fox_seq_kernel,
        grid=(bsz, FOX_HEADS, nq),
        in_specs=[
            pl.BlockSpec((1, FOX_TILE, FOX_HEAD_DIM), lambda b, h, i: (1, b * nq + i, h)),
            pl.BlockSpec((1, seq_len, FOX_HEAD_DIM), lambda b, h, i: (2, b, h)),
            pl.BlockSpec((1, seq_len, FOX_HEAD_DIM), lambda b, h, i: (3, b, h)),
            pl.BlockSpec((1, FOX_TILE, LANES), lambda b, h, i: (b, i, 0)),
            pl.BlockSpec((1, FOX_HEADS, nq, FOX_TILE), lambda b, h, i: (b, 0, 0, 0)),
        ],
        out_specs=pl.BlockSpec((FOX_TILE, FOX_HEAD_DIM), lambda b, h, i: (b * nq + i, h)),
        out_shape=jax.ShapeDtypeStruct((bsz * seq_len, FOX_WIDTH), BF16),
        compiler_params=_params("parallel", "parallel", "arbitrary"),
        name="fox_sequences",
    )(z, z, z, c, ct)


PAGE_ROWS = PAGE_SIZE * FOX_HEADS
NEW_ROWS = DEC_SEQ * FOX_HEADS


def _fox_paged_kernel(pt_ref, q_ref, kn_ref, vn_ref, ln_ref, *refs):
    k_pages = refs[:N_PAGES]
    v_pages = refs[N_PAGES:2 * N_PAGES]
    lf_pages = refs[2 * N_PAGES:3 * N_PAGES]
    o_ref, kpad_ref, vpad_ref = refs[3 * N_PAGES:]
    del pt_ref
    past = N_PAGES * PAGE_ROWS
    q = q_ref[0].astype(BF16)
    row = lax.broadcasted_iota(jnp.int32, (NEW_ROWS, 1), 0)
    row_head = jnp.bitwise_and(row, FOX_HEADS - 1)

    c = jnp.concatenate([r[0] for r in lf_pages], axis=1)
    lane = lax.broadcasted_iota(jnp.int32, c.shape, 1)
    s = FOX_HEADS
    while s < past:
        c = c + jnp.where(lane >= s, pltpu.roll(c, s, 1), 0.0)
        s *= 2
    tail_lane = lax.broadcasted_iota(jnp.int32, (NEW_ROWS, LANES), 1)
    tail = jnp.broadcast_to(c[:, past - LANES:], (NEW_ROWS, LANES))
    c_last = jnp.sum(jnp.where(tail_lane == LANES - FOX_HEADS + row_head, tail, 0.0), axis=1, keepdims=True)
    ln = ln_ref[0]
    blocks = [c_last[0:FOX_HEADS] + ln[0:FOX_HEADS]]
    for t in range(1, DEC_SEQ):
        blocks.append(blocks[-1] + ln[t * FOX_HEADS:(t + 1) * FOX_HEADS])
    c_new = jnp.concatenate(blocks, axis=0)
    c_new_row = jnp.sum(jnp.where(tail_lane == row, c_new, 0.0), axis=0, keepdims=True)

    kpad_ref[...] = jnp.zeros_like(kpad_ref)
    vpad_ref[...] = jnp.zeros_like(vpad_ref)
    kpad_ref[0:NEW_ROWS, :] = kn_ref[0]
    vpad_ref[0:NEW_ROWS, :] = vn_ref[0]

    s_past = jnp.concatenate(
        [lax.dot_general(q, kp[0].astype(BF16), NT_DIMS, preferred_element_type=F32) for kp in k_pages], axis=1)
    s_past = s_past * FOX_SCALE + c_new - c
    s_past = jnp.where(jnp.bitwise_and(lane, FOX_HEADS - 1) == row_head, s_past, NEG_INF)
    s_new = lax.dot_general(q, kpad_ref[...].astype(BF16), NT_DIMS, preferred_element_type=F32)
    s_new = s_new * FOX_SCALE + c_new - c_new_row
    visible = (jnp.bitwise_and(tail_lane, FOX_HEADS - 1) == row_head) & (tail_lane <= row)
    s_new = jnp.where(visible, s_new, NEG_INF)

    m = jnp.maximum(jnp.max(s_past, axis=1, keepdims=True), jnp.max(s_new, axis=1, keepdims=True))
    p_past = jnp.exp(s_past - m)
    p_new = jnp.exp(s_new - m)
    total = jnp.sum(p_past, axis=1, keepdims=True) + jnp.sum(p_new, axis=1, keepdims=True)
    p_past = p_past.astype(BF16)
    acc = jnp.dot(p_new.astype(BF16), vpad_ref[...].astype(BF16), preferred_element_type=F32)
    for j, vp in enumerate(v_pages):
        acc = acc + jnp.dot(p_past[:, j * PAGE_ROWS:(j + 1) * PAGE_ROWS], vp[0].astype(BF16),
                            preferred_element_type=F32)
    o_ref[0] = acc / total


def fox_paged(q, k_new, v_new, logf_new, cache_k, cache_v, cache_logf, page_table, first_page):
    bsz = q.shape[0]

    def page(j):
        return lambda b, pt: (first_page + pt[b, j], 0, 0)

    new_spec = pl.BlockSpec((1, NEW_ROWS, FOX_HEAD_DIM), lambda b, pt: (b, 0, 0))
    in_specs = [new_spec, new_spec, new_spec, pl.BlockSpec((1, NEW_ROWS, 1), lambda b, pt: (b, 0, 0))]
    in_specs += [pl.BlockSpec((1, PAGE_ROWS, FOX_HEAD_DIM), page(j)) for j in range(N_PAGES)]
    in_specs += [pl.BlockSpec((1, PAGE_ROWS, FOX_HEAD_DIM), page(j)) for j in range(N_PAGES)]
    in_specs += [pl.BlockSpec((1, 1, PAGE_ROWS), page(j)) for j in range(N_PAGES)]
    return pl.pallas_call(
        _fox_paged_kernel,
        grid_spec=pltpu.PrefetchScalarGridSpec(
            num_scalar_prefetch=1,
            grid=(bsz,),
            in_specs=in_specs,
            out_specs=new_spec,
            scratch_shapes=[pltpu.VMEM((LANES, FOX_HEAD_DIM), F32), pltpu.VMEM((LANES, FOX_HEAD_DIM), F32)],
        ),
        out_shape=jax.ShapeDtypeStruct((bsz, NEW_ROWS, FOX_HEAD_DIM), F32),
        compiler_params=_params("parallel"),
        name="fox_paged",
    )(page_table, q, k_new, v_new, logf_new,
      *([cache_k] * N_PAGES), *([cache_v] * N_PAGES), *([cache_logf] * N_PAGES))


ROPE_HALF = RET_KD // 2
RET_SCALE = RET_KD ** -0.5
RET_LOG_GAMMA = [float(np.log1p(-np.float32(2.0 ** (-5.0 - h)))) for h in range(RET_HEADS)]


def _rope_kernel(f_ref, cos_ref, sin_ref, *, start):
    pos = (start + lax.broadcasted_iota(jnp.int32, cos_ref.shape, 0)).astype(F32)
    ang = pos * f_ref[...]
    cos_ref[...] = jnp.cos(ang)
    sin_ref[...] = jnp.sin(ang)


def rope_table(start, n):
    inv_freq = ROPE_BASE ** (-jnp.arange(ROPE_HALF, dtype=F32) / ROPE_HALF)
    shape = jax.ShapeDtypeStruct((n, ROPE_HALF), F32)
    return pl.pallas_call(
        functools.partial(_rope_kernel, start=start), out_shape=(shape, shape), name="rope_table",
    )(inv_freq.reshape(1, ROPE_HALF))


def _rotate(x, cos, sin):
    x1, x2 = x[:, :ROPE_HALF], x[:, ROPE_HALF:]
    return jnp.concatenate([x1 * cos - x2 * sin, x1 * sin + x2 * cos], axis=1)


def _group_norm_gate(o, g, gain):
    mu = jnp.mean(o, axis=-1, keepdims=True)
    var = jnp.mean(jnp.square(o - mu), axis=-1, keepdims=True)
    return g * jax.nn.sigmoid(g) * ((o - mu) * lax.rsqrt(var + EPS) * gain)


def _head_slabs(refs, h, per_slab):
    width = PROJ_TILE // per_slab
    return refs[h // per_slab], slice((h % per_slab) * width, (h % per_slab + 1) * width)


def _ret_seq_kernel(*refs):
    q_refs, k_refs, v_refs, g_refs = refs[0:2], refs[2:4], refs[4:8], refs[8:12]
    cos_ref, sin_ref, gain_ref, o_ref, s_out, s_ref = refs[12:]
    step = pl.program_id(1)

    @pl.when(step == 0)
    def _():
        s_ref[...] = jnp.zeros_like(s_ref)

    cos = cos_ref[...]
    sin = sin_ref[...]
    cl = RET_CHUNK
    i_col = lax.broadcasted_iota(jnp.int32, (cl, 1), 0).astype(F32)
    diff = i_col - lax.broadcasted_iota(jnp.int32, (1, cl), 1).astype(F32)
    for h in range(RET_HEADS):
        lg = RET_LOG_GAMMA[h]
        ref, sl = _head_slabs(q_refs, h, 4)
        q = _rotate(ref[0, :, sl], cos, sin).astype(BF16)
        ref, sl = _head_slabs(k_refs, h, 4)
        k = _rotate(ref[0, :, sl], cos, sin) * RET_SCALE
        ref, sl = _head_slabs(v_refs, h, 2)
        v = ref[0, :, sl].astype(BF16)
        ref, sl = _head_slabs(g_refs, h, 2)
        g = ref[0, :, sl]
        dmask = jnp.where(diff >= 0, jnp.exp(jnp.maximum(diff, 0.0) * lg), 0.0)
        inner = lax.dot_general(q, k.astype(BF16), NT_DIMS, preferred_element_type=F32) * dmask
        state = s_ref[h]
        out = jnp.dot(inner.astype(BF16), v, preferred_element_type=F32)
        out = out + jnp.dot(q, state.astype(BF16), preferred_element_type=F32) * jnp.exp((i_col + 1.0) * lg)
        k_dec = k * jnp.exp((cl - 1.0 - i_col) * lg)
        s_ref[h] = math.exp(cl * lg) * state + jnp.dot(k_dec.T.astype(BF16), v, preferred_element_type=F32)
        hs = slice(h * RET_VD, (h + 1) * RET_VD)
        o_ref[:, hs] = _group_norm_gate(out, g, gain_ref[:, hs]).astype(BF16)

    @pl.when(step == pl.num_programs(1) - 1)
    def _():
        s_out[0] = s_ref[...]


def ret_sequences(z, cos, sin, gain, bsz, seq_len):
    steps = seq_len // RET_CHUNK
    vd = RET_HEADS * RET_VD
    slab = lambda j: pl.BlockSpec((1, RET_CHUNK, PROJ_TILE), lambda b, c: (j, b * steps + c, 0))
    table = pl.BlockSpec((RET_CHUNK, ROPE_HALF), lambda b, c: (c, 0))
    return pl.pallas_call(
        _ret_seq_kernel,
        grid=(bsz, steps),
        in_specs=[slab(j) for j in range(12)] + [table, table, _whole((1, vd))],
        out_specs=(pl.BlockSpec((RET_CHUNK, vd), lambda b, c: (b * steps + c, 0)),
                   pl.BlockSpec((1, RET_HEADS, RET_KD, RET_VD), lambda b, c: (b, 0, 0, 0))),
        out_shape=(jax.ShapeDtypeStruct((bsz * seq_len, vd), BF16),
                   jax.ShapeDtypeStruct((bsz, RET_HEADS, RET_KD, RET_VD), F32)),
        scratch_shapes=[pltpu.VMEM((RET_HEADS, RET_KD, RET_VD), F32)],
        compiler_params=_params("parallel", "arbitrary"),
        name="ret_sequences",
    )(*([z] * 12), cos, sin, gain.reshape(1, vd))


RET_PAD = 128


RET_STEP_SEQS = 8 // DEC_SEQ


def _ret_step_kernel(z_ref, s0_ref, cos_ref, sin_ref, gain_ref, o_ref, s_out, qpad, kpad, vpad, gpad):
    cl = DEC_SEQ
    for ref in (qpad, kpad, vpad, gpad):
        ref[...] = jnp.zeros_like(ref)
    cos = cos_ref[...]
    sin = sin_ref[...]
    i_col = lax.broadcasted_iota(jnp.int32, (RET_PAD, 1), 0).astype(F32)
    diff = i_col - lax.broadcasted_iota(jnp.int32, (1, RET_PAD), 1).astype(F32)
    valid = (diff >= 0) & (i_col < cl)
    for b in range(RET_STEP_SEQS):
        tok = slice(b * cl, (b + 1) * cl)
        outs = []
        for h in range(RET_HEADS):
            lg = RET_LOG_GAMMA[h]
            qpad[0:cl, :] = z_ref[h // 4, tok, (h % 4) * RET_KD:(h % 4 + 1) * RET_KD]
            kpad[0:cl, :] = z_ref[2 + h // 4, tok, (h % 4) * RET_KD:(h % 4 + 1) * RET_KD]
            vpad[0:cl, :] = z_ref[4 + h // 2, tok, (h % 2) * RET_VD:(h % 2 + 1) * RET_VD]
            gpad[0:cl, :] = z_ref[8 + h // 2, tok, (h % 2) * RET_VD:(h % 2 + 1) * RET_VD]
            q = _rotate(qpad[...], cos, sin).astype(BF16)
            k = _rotate(kpad[...], cos, sin) * RET_SCALE
            v = vpad[...].astype(BF16)
            dmask = jnp.where(valid, jnp.exp(jnp.maximum(diff, 0.0) * lg), 0.0)
            inner = lax.dot_general(q, k.astype(BF16), NT_DIMS, preferred_element_type=F32) * dmask
            state = s0_ref[b, h]
            out = jnp.dot(inner.astype(BF16), v, preferred_element_type=F32)
            out = out + jnp.dot(q, state.astype(BF16), preferred_element_type=F32) * jnp.exp((i_col + 1.0) * lg)
            k_dec = k * jnp.exp((cl - 1.0 - i_col) * lg)
            s_out[b, h] = math.exp(cl * lg) * state + jnp.dot(k_dec.T.astype(BF16), v, preferred_element_type=F32)
            hs = slice(h * RET_VD, (h + 1) * RET_VD)
            outs.append(_group_norm_gate(out[0:8], gpad[0:8, :], gain_ref[:, hs]))
        o_ref[tok, :] = jnp.concatenate(outs, axis=1)[0:cl]


def ret_steps(z, first_row, s0, cos, sin, gain):
    bsz = s0.shape[0]
    vd = RET_HEADS * RET_VD
    rows = RET_STEP_SEQS * DEC_SEQ
    state_spec = pl.BlockSpec((RET_STEP_SEQS, RET_HEADS, RET_KD, RET_VD), lambda i: (i, 0, 0, 0))
    return pl.pallas_call(
        _ret_step_kernel,
        grid=(bsz // RET_STEP_SEQS,),
        in_specs=[pl.BlockSpec((12, rows, PROJ_TILE), lambda i: (0, first_row // rows + i, 0)), state_spec,
                  _whole((RET_PAD, ROPE_HALF)), _whole((RET_PAD, ROPE_HALF)), _whole((1, vd))],
        out_specs=(pl.BlockSpec((rows, vd), lambda i: (i, 0)), state_spec),
        out_shape=(jax.ShapeDtypeStruct((bsz * DEC_SEQ, vd), F32),
                   jax.ShapeDtypeStruct(s0.shape, F32)),
        scratch_shapes=[pltpu.VMEM((RET_PAD, RET_KD), F32), pltpu.VMEM((RET_PAD, RET_KD), F32),
                        pltpu.VMEM((RET_PAD, RET_VD), F32), pltpu.VMEM((RET_PAD, RET_VD), F32)],
        compiler_params=_params("parallel"),
        name="ret_steps",
    )(z, s0, cos, sin, gain.reshape(1, vd))


def _even_mixer(h, gain, e, cache_k, cache_v, cache_logf, state_s5_re, state_s5_im, page_table, w_in, b_f,
                s5_w):
    z, zf = norm_proj(h, gain, w_in, 4 * PROJ_TILE, narrow=True)

    s5_p, hr_p, hi_p = s5_sequences(z, s5_w, BATCH, SEQ)
    u_s = jnp.stack([z[0, N_PROMPT + t::DEC_SEQ] for t in range(DEC_SEQ)])
    s5_s, hr_s, hi_s = s5_steps(u_s, state_s5_re.reshape(DEC_BATCH, S5_LANES),
                                state_s5_im.reshape(DEC_BATCH, S5_LANES), s5_w)
    s5_s = s5_s.transpose(1, 0, 2).reshape(N_SAMPLE, S5_WIDTH)

    lf_p, c_p, ct_p = fox_gates_seq(zf, b_f, BATCH, SEQ)
    lf_s = fox_gates(zf[N_PROMPT:], b_f)[:, :FOX_HEADS]
    k_p, v_p, q_s, k_s, v_s = head_rows(z)
    att_p = fox_sequences(z, c_p, ct_p, BATCH, SEQ)
    pool = cache_k.shape[1]
    new_rows = lambda t: t.reshape(DEC_BATCH, NEW_ROWS, FOX_HEAD_DIM)
    att_s = fox_paged(new_rows(q_s), new_rows(k_s), new_rows(v_s), lf_s.reshape(DEC_BATCH, NEW_ROWS, 1),
                      cache_k.reshape(-1, PAGE_ROWS, FOX_HEAD_DIM), cache_v.reshape(-1, PAGE_ROWS, FOX_HEAD_DIM),
                      cache_logf.reshape(-1, 1, PAGE_ROWS), page_table, e * pool)
    att_s = att_s.reshape(N_SAMPLE, FOX_WIDTH)

    heads = lambda t, b, s: t.reshape(b, s, FOX_HEADS, FOX_HEAD_DIM)
    grid = lambda t: t.reshape(-1, S5_GROUPS, S5_STATE)
    st_p = (heads(k_p, BATCH, SEQ), heads(v_p, BATCH, SEQ),
            lf_p[:, :FOX_HEADS].reshape(BATCH, SEQ, FOX_HEADS), grid(hr_p), grid(hi_p))
    st_s = (heads(k_s, DEC_BATCH, DEC_SEQ), heads(v_s, DEC_BATCH, DEC_SEQ),
            lf_s.reshape(DEC_BATCH, DEC_SEQ, FOX_HEADS), grid(hr_s), grid(hi_s))
    return ([s5_p, att_p], [s5_s, att_s]), st_p, st_s


def _odd_mixer(h, gain, w_in, gn_gain, state, tables):
    z = norm_proj(h, gain, w_in, 12 * PROJ_TILE)
    (cos_p, sin_p), (cos_s, sin_s) = tables
    o_p, st_p = ret_sequences(z, cos_p, sin_p, gn_gain, BATCH, SEQ)
    o_s, st_s = ret_steps(z, N_PROMPT, state, cos_s, sin_s, gn_gain)
    return ([o_p], [o_s]), st_p, st_s


def kernel(x_prompt, x_sample, cache_k, cache_v, cache_logf, state_s5_re, state_s5_im, state_ret, page_table, norm_gain, w_ffn_gate, w_ffn_up, w_ffn_down, w_in_even, b_forget, s5_lam_re, s5_lam_im, s5_b_re, s5_b_im, s5_c_re, s5_c_im, s5_d, s5_log_dt, w_glu, b_glu, w_out_even, w_in_odd, ret_gn_gain, w_out_odd, final_norm):
    x = jnp.concatenate([x_prompt.reshape(N_PROMPT, D_MODEL), x_sample.reshape(N_SAMPLE, D_MODEL)], axis=0)
    tables = (rope_table(0, SEQ), rope_table(PAST_LEN, RET_PAD))
    ev_p, ev_s, od_p, od_s = [], [], [], []
    for layer in range(DEPTH):
        h = ffn(x, norm_gain[layer, 0], w_ffn_gate, w_ffn_up, w_ffn_down, layer, 0)
        if layer % 2 == 0:
            e = layer // 2
            s5_w = s5_weights(s5_lam_re[e], s5_lam_im[e], s5_log_dt[e], s5_b_re[e], s5_b_im[e], s5_c_re[e],
                              s5_c_im[e], s5_d[e], w_glu[e], b_glu[e])
            parts, st_p, st_s = _even_mixer(h, norm_gain[layer, 1], e, cache_k, cache_v, cache_logf,
                                            state_s5_re[e], state_s5_im[e], page_table, w_in_even[e],
                                            b_forget[e], s5_w)
            ev_p.append(st_p)
            ev_s.append(st_s)
            w_out = w_out_even[e]
        else:
            o = layer // 2
            parts, st_p, st_s = _odd_mixer(h, norm_gain[layer, 1], w_in_odd[o], ret_gn_gain[o], state_ret[o],
                                           tables)
            od_p.append(st_p)
            od_s.append(st_s)
            w_out = w_out_odd[o]
        h = out_proj(*parts, w_out, h)
        x = ffn(h, norm_gain[layer, 2], w_ffn_gate, w_ffn_up, w_ffn_down, layer, 1)

    y_prompt, y_sample = final_rms(x, final_norm)
    y_prompt = y_prompt.reshape(BATCH, SEQ, D_MODEL)
    y_sample = y_sample.reshape(DEC_BATCH, DEC_SEQ, D_MODEL)
    stack = lambda sts, i: jnp.stack([s[i] for s in sts])
    return (y_prompt, y_sample,
            stack(ev_p, 0), stack(ev_p, 1), stack(ev_p, 2), stack(ev_p, 3), stack(ev_p, 4), jnp.stack(od_p),
            stack(ev_s, 0), stack(ev_s, 1), stack(ev_s, 2), stack(ev_s, 3), stack(ev_s, 4), jnp.stack(od_s))
```

```python
import functools
import math

import jax
import jax.numpy as jnp
import numpy as np
from jax import lax
from jax.experimental import pallas as pl
from jax.experimental.pallas import tpu as pltpu

D_MODEL = 2048
BATCH = 2
SEQ = 4096
DEPTH = 2
DEC_BATCH = 128
DEC_SEQ = 4
PAST_LEN = 2048
PAGE_SIZE = 128
N_PAGES = PAST_LEN // PAGE_SIZE
S5_WIDTH = 1024
S5_GROUP = 16
S5_GROUPS = 64
S5_STATE = 64
FOX_WIDTH = 1024
FOX_HEAD_DIM = 128
FOX_HEADS = 8
NEG_INF = -1e30
RET_HEADS = 8
RET_KD = 256
RET_VD = 512
RET_CHUNK = 256
ROPE_BASE = 10000.0
D_FF = 5504
EPS = 1e-6

N_PROMPT = BATCH * SEQ
N_SAMPLE = DEC_BATCH * DEC_SEQ
N_TOK = N_PROMPT + N_SAMPLE

ROW_TILE = 1088
FF_TILE = 256
LANES = 128
VMEM_LIMIT = 56 * 1024 * 1024
FFN_VMEM_LIMIT = 62 * 1024 * 1024

F32 = jnp.float32
BF16 = jnp.bfloat16
NT_DIMS = (((1,), (1,)), ((), ()))


def _rms_rows(x, gain):
    return x * lax.rsqrt(jnp.mean(x * x, axis=-1, keepdims=True) + EPS) * gain


def _whole(shape):
    return pl.BlockSpec(shape, lambda *_: (0,) * len(shape))


def _params(*semantics):
    return pltpu.CompilerParams(dimension_semantics=semantics, vmem_limit_bytes=VMEM_LIMIT)


FFN_GROUP = 2


def _ffn_kernel(x_ref, g_ref, wg_ref, wu_ref, wd_ref, o_ref, h_ref, *, d_ff):
    step = pl.program_id(1)
    tf = wg_ref.shape[1]

    @pl.when(step < FFN_GROUP)
    def _():
        rows = pl.ds(pl.multiple_of(step * ROW_TILE, ROW_TILE), ROW_TILE)
        x = x_ref[...]
        h_ref[rows, :] = _rms_rows(x, g_ref[...]).astype(BF16)
        o_ref[rows, :] = x

    @pl.when(step >= FFN_GROUP)
    def _():
        first = (step - FFN_GROUP) * tf
        wg = wg_ref[...].astype(BF16)
        wu = wu_ref[...].astype(BF16)
        col_ok = first + lax.broadcasted_iota(jnp.int32, (1, tf), 1) < d_ff
        row_ok = first + lax.broadcasted_iota(jnp.int32, (tf, 1), 0) < d_ff
        wd = jnp.where(row_ok, wd_ref[...], 0.0).astype(BF16)
        for t in range(FFN_GROUP):
            rows = slice(t * ROW_TILE, (t + 1) * ROW_TILE)
            h = h_ref[rows, :]
            a = jnp.dot(h, wg, preferred_element_type=F32)
            b = jnp.dot(h, wu, preferred_element_type=F32)
            c = jnp.where(col_ok, a * jax.nn.sigmoid(a) * b, 0.0).astype(BF16)
            o_ref[rows, :] += 0.5 * jnp.dot(c, wd, preferred_element_type=F32)


def ffn(x, gain, wg, wu, wd, layer, which):
    m, d = x.shape
    d_ff = wg.shape[3]
    group_rows = FFN_GROUP * ROW_TILE
    w_tile = lambda r, s: jnp.maximum(s - FFN_GROUP, 0)
    return pl.pallas_call(
        functools.partial(_ffn_kernel, d_ff=d_ff),
        grid=(m // group_rows, FFN_GROUP + pl.cdiv(d_ff, FF_TILE)),
        in_specs=[
            pl.BlockSpec((ROW_TILE, d), lambda r, s: (FFN_GROUP * r + jnp.minimum(s, FFN_GROUP - 1), 0)),
            pl.BlockSpec((1, d), lambda r, s: (0, 0)),
            pl.BlockSpec((None, None, d, FF_TILE), lambda r, s: (layer, which, 0, w_tile(r, s))),
            pl.BlockSpec((None, None, d, FF_TILE), lambda r, s: (layer, which, 0, w_tile(r, s))),
            pl.BlockSpec((None, None, FF_TILE, d), lambda r, s: (layer, which, w_tile(r, s), 0)),
        ],
        out_specs=pl.BlockSpec((group_rows, d), lambda r, s: (r, 0), pipeline_mode=pl.Buffered(1)),
        out_shape=jax.ShapeDtypeStruct((m, d), F32),
        scratch_shapes=[pltpu.VMEM((group_rows, d), BF16)],
        compiler_params=pltpu.CompilerParams(
            dimension_semantics=("parallel", "arbitrary"), vmem_limit_bytes=FFN_VMEM_LIMIT),
        name="ffn",
    )(x, gain.reshape(1, d), wg, wu, wd)


SPLIT_TILE = 512
N_PROMPT_TILES = N_PROMPT // SPLIT_TILE


def _split_specs(block):
    zeros = (0,) * (len(block) - 1)
    return (pl.BlockSpec(block, lambda i: (jnp.minimum(i, N_PROMPT_TILES - 1),) + zeros),
            pl.BlockSpec(block, lambda i: (0,) + zeros))


def _final_norm_kernel(x_ref, g_ref, yp_ref, ys_ref):
    y = _rms_rows(x_ref[...], g_ref[...])

    @pl.when(pl.program_id(0) < N_PROMPT_TILES)
    def _():
        yp_ref[...] = y

    @pl.when(pl.program_id(0) == N_PROMPT_TILES)
    def _():
        ys_ref[...] = y


def final_rms(x, gain):
    d = x.shape[1]
    return pl.pallas_call(
        _final_norm_kernel,
        grid=(N_PROMPT_TILES + 1,),
        in_specs=[pl.BlockSpec((SPLIT_TILE, d), lambda i: (i, 0)), _whole((1, d))],
        out_specs=_split_specs((SPLIT_TILE, d)),
        out_shape=(jax.ShapeDtypeStruct((N_PROMPT, d), F32), jax.ShapeDtypeStruct((N_SAMPLE, d), F32)),
        compiler_params=_params("arbitrary"),
        name="final_norm",
    )(x, gain.reshape(1, d))


def _head_rows_kernel(q_ref, k_ref, v_ref, kp_ref, vp_ref, qs_ref, ks_ref, vs_ref):
    def put(pairs):
        for src, dst in pairs:
            dst[...] = src[0].reshape(SPLIT_TILE, FOX_HEADS, FOX_HEAD_DIM)

    @pl.when(pl.program_id(0) < N_PROMPT_TILES)
    def _():
        put([(k_ref, kp_ref), (v_ref, vp_ref)])

    @pl.when(pl.program_id(0) == N_PROMPT_TILES)
    def _():
        put([(q_ref, qs_ref), (k_ref, ks_ref), (v_ref, vs_ref)])


def head_rows(z):
    block = (SPLIT_TILE, FOX_HEADS, FOX_HEAD_DIM)
    shape = lambda n: jax.ShapeDtypeStruct((n, FOX_HEADS, FOX_HEAD_DIM), F32)
    kp_spec, ks_spec = _split_specs(block)
    return pl.pallas_call(
        _head_rows_kernel,
        grid=(N_PROMPT_TILES + 1,),
        in_specs=[pl.BlockSpec((1, SPLIT_TILE, FOX_WIDTH), lambda i: (1, N_PROMPT_TILES, 0)),
                  pl.BlockSpec((1, SPLIT_TILE, FOX_WIDTH), lambda i: (2, i, 0)),
                  pl.BlockSpec((1, SPLIT_TILE, FOX_WIDTH), lambda i: (3, i, 0))],
        out_specs=(kp_spec, kp_spec, ks_spec, ks_spec, ks_spec),
        out_shape=(shape(N_PROMPT), shape(N_PROMPT), shape(N_SAMPLE), shape(N_SAMPLE), shape(N_SAMPLE)),
        compiler_params=_params("arbitrary"),
        name="head_rows",
    )(z, z, z)


PROJ_TILE = 1024


def _norm_proj_kernel(x_ref, g_ref, w_ref, *rest, narrow):
    if narrow:
        wn_ref, z_ref, zn_ref, h_ref = rest
    else:
        z_ref, h_ref = rest
    j = pl.program_id(1)

    @pl.when(j == 0)
    def _():
        h_ref[...] = _rms_rows(x_ref[...], g_ref[...]).astype(BF16)
        if narrow:
            zn_ref[...] = jnp.dot(h_ref[...], wn_ref[...].astype(BF16), preferred_element_type=F32)

    z_ref[0] = jnp.dot(h_ref[...], w_ref[...].astype(BF16), preferred_element_type=F32)


def norm_proj(x, gain, w, n_wide, narrow=False):
    m, d = x.shape
    nj = n_wide // PROJ_TILE
    in_specs = [
        pl.BlockSpec((ROW_TILE, d), lambda i, j: (i, 0)),
        pl.BlockSpec((1, d), lambda i, j: (0, 0)),
        pl.BlockSpec((d, PROJ_TILE), lambda i, j: (0, j)),
    ]
    args = [x, gain.reshape(1, d), w]
    out_shape = jax.ShapeDtypeStruct((nj, m, PROJ_TILE), F32)
    out_specs = pl.BlockSpec((1, ROW_TILE, PROJ_TILE), lambda i, j: (j, i, 0))
    if narrow:
        in_specs.append(pl.BlockSpec((d, LANES), lambda i, j: (0, n_wide // LANES)))
        args.append(w)
        out_shape = (out_shape, jax.ShapeDtypeStruct((m, LANES), F32))
        out_specs = (out_specs, pl.BlockSpec((ROW_TILE, LANES), lambda i, j: (i, 0)))
    return pl.pallas_call(
        functools.partial(_norm_proj_kernel, narrow=narrow),
        grid=(m // ROW_TILE, nj),
        in_specs=in_specs,
        out_specs=out_specs,
        out_shape=out_shape,
        scratch_shapes=[pltpu.VMEM((ROW_TILE, d), BF16)],
        compiler_params=_params("parallel", "arbitrary"),
        name="norm_proj_narrow" if narrow else "norm_proj",
    )(*args)


OUT_TILE = 512


def _out_proj_kernel(*refs, n_parts):
    a_refs, w_refs, r_ref, o_ref = refs[:n_parts], refs[n_parts:2 * n_parts], refs[2 * n_parts], refs[-1]
    acc = r_ref[...]
    for a_ref, w_ref in zip(a_refs, w_refs):
        acc = acc + jnp.dot(a_ref[...].astype(BF16), w_ref[...].astype(BF16), preferred_element_type=F32)
    o_ref[...] = acc


def _out_proj_rows(parts, w, res, row_tile, first_tile, prev=None):
    rows, k = parts[0].shape
    n = w.shape[1]
    n_parts = len(parts)
    out_block = pl.BlockSpec((row_tile, OUT_TILE), lambda i, j: (first_tile + i, j))
    in_specs = ([pl.BlockSpec((row_tile, k), lambda i, j: (i, 0))] * n_parts
                + [pl.BlockSpec((k, OUT_TILE), functools.partial(lambda i, j, p: (p, j), p=p))
                   for p in range(n_parts)]
                + [out_block])
    args = [*parts, *([w] * n_parts), res]
    aliases = {}
    if prev is not None:
        in_specs.append(pl.BlockSpec(memory_space=pl.ANY))
        args.append(prev)
        aliases = {len(args) - 1: 0}
    return pl.pallas_call(
        functools.partial(_out_proj_kernel, n_parts=n_parts),
        grid=(rows // row_tile, n // OUT_TILE),
        in_specs=in_specs,
        out_specs=out_block,
        out_shape=jax.ShapeDtypeStruct((res.shape[0], n), F32),
        input_output_aliases=aliases,
        compiler_params=_params("parallel", "arbitrary"),
        name="out_proj",
    )(*args)


OUT_ROWS = 1024


def out_proj(parts_prompt, parts_sample, w, res):
    out = _out_proj_rows(parts_prompt, w, res, OUT_ROWS, 0)
    return _out_proj_rows(parts_sample, w, res, N_SAMPLE, N_PROMPT // N_SAMPLE, prev=out)


S5_LANES = S5_GROUPS * S5_STATE
S5_BLOCKS = 4
S5_BLOCK_CH = S5_WIDTH // S5_BLOCKS
S5_BLOCK_ST = S5_LANES // S5_BLOCKS
SCAN_LANES = 512
S5_TILE = 512


def _s5_param_kernel(lr_ref, li_ref, ldt_ref, bre_ref, bim_ref, tab_ref, bbr_ref, bbi_ref):
    lr = lr_ref[...]
    li = li_ref[...]
    dt = jnp.exp(ldt_ref[...])
    dec = lr * dt
    ang = li * dt
    er = jnp.exp(dec)
    xr = er * jnp.cos(ang) - 1.0
    xi = er * jnp.sin(ang)
    den = lr * lr + li * li
    cr = (xr * lr + xi * li) / den
    ci = (xi * lr - xr * li) / den
    bre = bre_ref[...]
    bim = bim_ref[...]
    bbr_ref[...] = cr * bre - ci * bim
    bbi_ref[...] = cr * bim + ci * bre

    row = lax.broadcasted_iota(jnp.int32, (8, S5_LANES), 0)

    def power(k):
        mag = jnp.exp(k * dec)
        return mag * jnp.cos(k * ang), mag * jnp.sin(k * ang)

    for idx, s in enumerate((1, 2, 4)):
        pr, pi = power(float(s))
        tab_ref[2 * idx] = jnp.where(row >= s, pr, 0.0)
        tab_ref[2 * idx + 1] = jnp.where(row >= s, pi, 0.0)
    pr, pi = power((row + 1).astype(F32))
    tab_ref[6] = pr
    tab_ref[7] = pi


def s5_params(lam_re, lam_im, log_dt, b_re, b_im):
    flat = lambda t: t.reshape(1, S5_LANES)
    ldt = jnp.broadcast_to(log_dt[:, None], (S5_GROUPS, S5_STATE))
    to_rows = lambda t: t.reshape(S5_LANES, S5_GROUP).T
    return pl.pallas_call(
        _s5_param_kernel,
        out_shape=(jax.ShapeDtypeStruct((8, 8, S5_LANES), F32),
                   jax.ShapeDtypeStruct((S5_GROUP, S5_LANES), F32),
                   jax.ShapeDtypeStruct((S5_GROUP, S5_LANES), F32)),
        name="s5_params",
    )(flat(lam_re), flat(lam_im), flat(ldt), to_rows(b_re), to_rows(b_im))


def _block_diag_in(bbr, bbi):
    bb = jnp.stack([bbr, bbi]).reshape(2, S5_GROUP, S5_BLOCKS, 16, S5_STATE)
    eye = jnp.eye(16, dtype=F32)
    t = bb[:, :, :, :, None, :] * eye[None, None, None, :, :, None]
    return t.transpose(2, 3, 1, 0, 4, 5).reshape(S5_BLOCKS, S5_BLOCK_CH, 2 * S5_BLOCK_ST).astype(BF16)


def _block_diag_out(c):
    cc = c.reshape(S5_BLOCKS, 16, S5_GROUP, S5_STATE)
    eye = jnp.eye(16, dtype=F32)
    t = cc.transpose(0, 1, 3, 2)[:, :, :, None, :] * eye[None, :, None, :, None]
    return t.reshape(S5_BLOCKS, S5_BLOCK_ST, S5_BLOCK_CH).astype(BF16)


def _s5_drive(ub, bbd_ref, hr_ref, hi_ref):
    for blk in range(S5_BLOCKS):
        bu = jnp.dot(ub[:, blk * S5_BLOCK_CH:(blk + 1) * S5_BLOCK_CH], bbd_ref[blk], preferred_element_type=F32)
        hr_ref[:, blk * S5_BLOCK_ST:(blk + 1) * S5_BLOCK_ST] = bu[:, :S5_BLOCK_ST]
        hi_ref[:, blk * S5_BLOCK_ST:(blk + 1) * S5_BLOCK_ST] = bu[:, S5_BLOCK_ST:]


def _s5_readout(u, hr_ref, hi_ref, cr_ref, ci_ref, d_ref, wg_ref, bg_ref):
    ys = []
    for blk in range(S5_BLOCKS):
        sl = slice(blk * S5_BLOCK_ST, (blk + 1) * S5_BLOCK_ST)
        yr = jnp.dot(hr_ref[:, sl].astype(BF16), cr_ref[blk], preferred_element_type=F32)
        yi = jnp.dot(hi_ref[:, sl].astype(BF16), ci_ref[blk], preferred_element_type=F32)
        ys.append(yr - yi)
    y = jax.nn.gelu(jnp.concatenate(ys, axis=1) + d_ref[...] * u)
    gate = jax.nn.sigmoid(jnp.dot(y.astype(BF16), wg_ref[...], preferred_element_type=F32) + bg_ref[...])
    return (y * gate).astype(BF16)


def _s5_seq_kernel(u_ref, tab_ref, bbd_ref, cr_ref, ci_ref, d_ref, wg_ref, bg_ref,
                   o_ref, hr_out, hi_out, hr_ref, hi_ref, car_r, car_i):
    step = pl.program_id(1)

    @pl.when(step == 0)
    def _():
        car_r[...] = jnp.zeros_like(car_r)
        car_i[...] = jnp.zeros_like(car_i)

    u = u_ref[0]
    ub = u.astype(BF16)
    n_tiles = u.shape[0] // 8
    ys = []
    for blk in range(S5_BLOCKS):
        st = slice(blk * S5_BLOCK_ST, (blk + 1) * S5_BLOCK_ST)
        bu = jnp.dot(ub[:, blk * S5_BLOCK_CH:(blk + 1) * S5_BLOCK_CH], bbd_ref[blk], preferred_element_type=F32)
        hr_ref[:, st] = bu[:, :S5_BLOCK_ST]
        hi_ref[:, st] = bu[:, S5_BLOCK_ST:]
        for c in range(S5_BLOCK_ST // SCAN_LANES):
            lo = blk * S5_BLOCK_ST + c * SCAN_LANES
            sl = slice(lo, lo + SCAN_LANES)
            tabs = [tab_ref[k, :, sl] for k in range(8)]
            cr, ci = car_r[:, sl], car_i[:, sl]
            for i in range(n_tiles):
                rows = slice(i * 8, (i + 1) * 8)
                xr = hr_ref[rows, sl]
                xi = hi_ref[rows, sl]
                for k, s in enumerate((1, 2, 4)):
                    ar, ai = tabs[2 * k], tabs[2 * k + 1]
                    rr = pltpu.roll(xr, s, 0)
                    ri = pltpu.roll(xi, s, 0)
                    xr, xi = xr + ar * rr - ai * ri, xi + ar * ri + ai * rr
                pr, pi = tabs[6], tabs[7]
                hr = xr + pr * cr - pi * ci
                hi = xi + pr * ci + pi * cr
                hr_ref[rows, sl] = hr
                hi_ref[rows, sl] = hi
                cr, ci = jnp.broadcast_to(hr[7:8], hr.shape), jnp.broadcast_to(hi[7:8], hi.shape)
            car_r[:, sl] = cr
            car_i[:, sl] = ci
        yr = jnp.dot(hr_ref[:, st].astype(BF16), cr_ref[blk], preferred_element_type=F32)
        yi = jnp.dot(hi_ref[:, st].astype(BF16), ci_ref[blk], preferred_element_type=F32)
        ys.append(yr - yi)
    y = jax.nn.gelu(jnp.concatenate(ys, axis=1) + d_ref[...] * u)
    gate = jax.nn.sigmoid(jnp.dot(y.astype(BF16), wg_ref[...], preferred_element_type=F32) + bg_ref[...])
    o_ref[...] = (y * gate).astype(BF16)

    @pl.when(step == pl.num_programs(1) - 1)
    def _():
        hr_out[0] = car_r[0:1, :]
        hi_out[0] = car_i[0:1, :]


def _s5_weight_specs():
    return [_whole((8, 8, S5_LANES)), _whole((S5_BLOCKS, S5_BLOCK_CH, 2 * S5_BLOCK_ST)),
            _whole((S5_BLOCKS, S5_BLOCK_ST, S5_BLOCK_CH)), _whole((S5_BLOCKS, S5_BLOCK_ST, S5_BLOCK_CH)),
            _whole((1, S5_WIDTH)), _whole((S5_WIDTH, S5_WIDTH)), _whole((1, S5_WIDTH))]


def s5_sequences(z, weights, bsz, seq_len):
    steps = seq_len // S5_TILE
    state = jax.ShapeDtypeStruct((bsz, 1, S5_LANES), F32)
    state_spec = pl.BlockSpec((1, 1, S5_LANES), lambda b, s: (b, 0, 0))
    return pl.pallas_call(
        _s5_seq_kernel,
        grid=(bsz, steps),
        in_specs=[pl.BlockSpec((1, S5_TILE, S5_WIDTH), lambda b, s: (0, b * steps + s, 0))] + _s5_weight_specs(),
        out_specs=(pl.BlockSpec((S5_TILE, S5_WIDTH), lambda b, s: (b * steps + s, 0)), state_spec, state_spec),
        out_shape=(jax.ShapeDtypeStruct((bsz * seq_len, S5_WIDTH), BF16), state, state),
        scratch_shapes=[pltpu.VMEM((S5_TILE, S5_LANES), F32), pltpu.VMEM((S5_TILE, S5_LANES), F32),
                        pltpu.VMEM((8, S5_LANES), F32), pltpu.VMEM((8, S5_LANES), F32)],
        compiler_params=_params("parallel", "arbitrary"),
        name="s5_sequences",
    )(z, *weights)


def _s5_step_kernel(u_ref, h0r_ref, h0i_ref, tab_ref, bbd_ref, cr_ref, ci_ref, d_ref, wg_ref, bg_ref,
                    o_ref, hr_ref, hi_ref, bur_ref, bui_ref):
    @pl.when(pl.program_id(0) == 0)
    def _():
        hr_ref[...] = h0r_ref[...]
        hi_ref[...] = h0i_ref[...]

    u = u_ref[0]
    _s5_drive(u.astype(BF16), bbd_ref, bur_ref, bui_ref)
    ar = tab_ref[6, 0:1, :]
    ai = tab_ref[7, 0:1, :]
    hr = hr_ref[...]
    hi = hi_ref[...]
    hr_ref[...] = ar * hr - ai * hi + bur_ref[...]
    hi_ref[...] = ar * hi + ai * hr + bui_ref[...]
    o_ref[0] = _s5_readout(u, hr_ref, hi_ref, cr_ref, ci_ref, d_ref, wg_ref, bg_ref)


def s5_steps(u, h0_re, h0_im, weights):
    steps, rows, _ = u.shape
    state = jax.ShapeDtypeStruct((rows, S5_LANES), F32)
    return pl.pallas_call(
        _s5_step_kernel,
        grid=(steps,),
        in_specs=[pl.BlockSpec((1, rows, S5_WIDTH), lambda t: (t, 0, 0)),
                  _whole((rows, S5_LANES)), _whole((rows, S5_LANES))] + _s5_weight_specs(),
        out_specs=(pl.BlockSpec((1, rows, S5_WIDTH), lambda t: (t, 0, 0)),
                   _whole((rows, S5_LANES)), _whole((rows, S5_LANES))),
        out_shape=(jax.ShapeDtypeStruct((steps, rows, S5_WIDTH), BF16), state, state),
        scratch_shapes=[pltpu.VMEM((rows, S5_LANES), F32), pltpu.VMEM((rows, S5_LANES), F32)],
        compiler_params=_params("arbitrary"),
        name="s5_steps",
    )(u, h0_re, h0_im, *weights)


def s5_weights(lam_re, lam_im, log_dt, b_re, b_im, c_re, c_im, d_skip, w_glu, b_glu):
    tab, bbr, bbi = s5_params(lam_re, lam_im, log_dt, b_re, b_im)
    return (tab, _block_diag_in(bbr, bbi), _block_diag_out(c_re), _block_diag_out(c_im),
            d_skip.reshape(1, S5_WIDTH), w_glu.astype(BF16), b_glu.reshape(1, S5_WIDTH))


def _lane_pad_bias(b_f):
    return jnp.pad(b_f, (0, LANES - FOX_HEADS)).reshape(1, LANES)


def _log_forget(zf_ref, bf_ref):
    lane = lax.broadcasted_iota(jnp.int32, zf_ref.shape, 1)
    return jnp.where(lane < FOX_HEADS, jax.nn.log_sigmoid(zf_ref[...] + bf_ref[...]), 0.0)


def _gate_seq_kernel(zf_ref, bf_ref, lf_ref, c_ref, ct_ref):
    lf = _log_forget(zf_ref, bf_ref)
    lf_ref[...] = lf
    row = lax.broadcasted_iota(jnp.int32, lf.shape, 0)
    c = lf
    s = 1
    while s < lf.shape[0]:
        c = c + jnp.where(row >= s, pltpu.roll(c, s, 0), 0.0)
        s *= 2
    c_ref[0] = c
    ct_ref[0] = c.T[:FOX_HEADS]


def fox_gates_seq(zf, b_f, bsz, seq_len):
    return pl.pallas_call(
        _gate_seq_kernel,
        grid=(bsz,),
        in_specs=[pl.BlockSpec((seq_len, LANES), lambda b: (b, 0)), _whole((1, LANES))],
        out_specs=(pl.BlockSpec((seq_len, LANES), lambda b: (b, 0)),
                   pl.BlockSpec((1, seq_len, LANES), lambda b: (b, 0, 0)),
                   pl.BlockSpec((1, FOX_HEADS, seq_len), lambda b: (b, 0, 0))),
        out_shape=(jax.ShapeDtypeStruct((bsz * seq_len, LANES), F32),
                   jax.ShapeDtypeStruct((bsz, seq_len, LANES), F32),
                   jax.ShapeDtypeStruct((bsz, FOX_HEADS, seq_len), F32)),
        compiler_params=_params("parallel"),
        name="fox_gates_seq",
    )(zf, _lane_pad_bias(b_f))


def _gate_kernel(zf_ref, bf_ref, lf_ref):
    lf_ref[...] = _log_forget(zf_ref, bf_ref)


def fox_gates(zf, b_f):
    return pl.pallas_call(
        _gate_kernel,
        out_shape=jax.ShapeDtypeStruct(zf.shape, F32),
        name="fox_gates",
    )(zf, _lane_pad_bias(b_f))


FOX_TILE = 1024
FOX_SCALE = FOX_HEAD_DIM ** -0.5


def _fox_seq_kernel(q_ref, k_ref, v_ref, c_ref, ct_ref, o_ref):
    h = pl.program_id(1)
    i = pl.program_id(2)
    q = q_ref[0].astype(BF16)
    lane = lax.broadcasted_iota(jnp.int32, c_ref.shape[1:], 1)
    cq = jnp.sum(jnp.where(lane == h, c_ref[0], 0.0), axis=1, keepdims=True)

    def chunk(j, carry, diagonal):
        m, l, acc = carry
        rows = pl.ds(pl.multiple_of(j * FOX_TILE, FOX_TILE), FOX_TILE)
        kc = k_ref[0, rows, :].astype(BF16)
        vc = v_ref[0, rows, :].astype(BF16)
        s = lax.dot_general(q, kc, NT_DIMS, preferred_element_type=F32) * FOX_SCALE
        s = s + cq - ct_ref[0, h, pl.ds(j, 1), :]
        if diagonal:
            qi = lax.broadcasted_iota(jnp.int32, s.shape, 0)
            ki = lax.broadcasted_iota(jnp.int32, s.shape, 1)
            s = jnp.where(qi >= ki, s, NEG_INF)
        m_new = jnp.maximum(m, jnp.max(s, axis=1, keepdims=True))
        alpha = jnp.exp(m - m_new)
        p = jnp.exp(s - m_new)
        l = alpha * l + jnp.sum(p, axis=1, keepdims=True)
        acc = alpha * acc + jnp.dot(p.astype(BF16), vc, preferred_element_type=F32)
        return m_new, l, acc

    init = (jnp.full((FOX_TILE, 1), NEG_INF, F32), jnp.zeros((FOX_TILE, 1), F32),
            jnp.zeros((FOX_TILE, FOX_HEAD_DIM), F32))
    carry = lax.fori_loop(0, i, functools.partial(chunk, diagonal=False), init)
    _, l, acc = chunk(i, carry, diagonal=True)
    o_ref[...] = (acc / l).astype(BF16)


def fox_sequences(z, c, ct, bsz, seq_len):
    nq = seq_len // FOX_TILE
    ct = ct.reshape(bsz, FOX_HEADS, nq, FOX_TILE)
    return pl.pallas_call(
        _fox_seq_kernel,
        grid=(bsz, FOX_HEADS, nq),
        in_specs=[
            pl.BlockSpec((1, FOX_TILE, FOX_HEAD_DIM), lambda b, h, i: (1, b * nq + i, h)),
            pl.BlockSpec((1, seq_len, FOX_HEAD_DIM), lambda b, h, i: (2, b, h)),
            pl.BlockSpec((1, seq_len, FOX_HEAD_DIM), lambda b, h, i: (3, b, h)),
            pl.BlockSpec((1, FOX_TILE, LANES), lambda b, h, i: (b, i, 0)),
            pl.BlockSpec((1, FOX_HEADS, nq, FOX_TILE), lambda b, h, i: (b, 0, 0, 0)),
        ],
        out_specs=pl.BlockSpec((FOX_TILE, FOX_HEAD_DIM), lambda b, h, i: (b * nq + i, h)),
        out_shape=jax.ShapeDtypeStruct((bsz * seq_len, FOX_WIDTH), BF16),
        compiler_params=_params("parallel", "parallel", "arbitrary"),
        name="fox_sequences",
    )(z, z, z, c, ct)


PAGE_ROWS = PAGE_SIZE * FOX_HEADS
NEW_ROWS = DEC_SEQ * FOX_HEADS


def _fox_paged_kernel(pt_ref, q_ref, kn_ref, vn_ref, ln_ref, *refs):
    k_pages = refs[:N_PAGES]
    v_pages = refs[N_PAGES:2 * N_PAGES]
    lf_pages = refs[2 * N_PAGES:3 * N_PAGES]
    o_ref, kpad_ref, vpad_ref = refs[3 * N_PAGES:]
    del pt_ref
    past = N_PAGES * PAGE_ROWS
    q = q_ref[0].astype(BF16)
    row = lax.broadcasted_iota(jnp.int32, (NEW_ROWS, 1), 0)
    row_head = jnp.bitwise_and(row, FOX_HEADS - 1)

    c = jnp.concatenate([r[0] for r in lf_pages], axis=1)
    lane = lax.broadcasted_iota(jnp.int32, c.shape, 1)
    s = FOX_HEADS
    while s < past:
        c = c + jnp.where(lane >= s, pltpu.roll(c, s, 1), 0.0)
        s *= 2
    tail_lane = lax.broadcasted_iota(jnp.int32, (NEW_ROWS, LANES), 1)
    tail = jnp.broadcast_to(c[:, past - LANES:], (NEW_ROWS, LANES))
    c_last = jnp.sum(jnp.where(tail_lane == LANES - FOX_HEADS + row_head, tail, 0.0), axis=1, keepdims=True)
    ln = ln_ref[0]
    blocks = [c_last[0:FOX_HEADS] + ln[0:FOX_HEADS]]
    for t in range(1, DEC_SEQ):
        blocks.append(blocks[-1] + ln[t * FOX_HEADS:(t + 1) * FOX_HEADS])
    c_new = jnp.concatenate(blocks, axis=0)
    c_new_row = jnp.sum(jnp.where(tail_lane == row, c_new, 0.0), axis=0, keepdims=True)

    kpad_ref[...] = jnp.zeros_like(kpad_ref)
    vpad_ref[...] = jnp.zeros_like(vpad_ref)
    kpad_ref[0:NEW_ROWS, :] = kn_ref[0]
    vpad_ref[0:NEW_ROWS, :] = vn_ref[0]

    s_past = jnp.concatenate(
        [lax.dot_general(q, kp[0].astype(BF16), NT_DIMS, preferred_element_type=F32) for kp in k_pages], axis=1)
    s_past = s_past * FOX_SCALE + c_new - c
    s_past = jnp.where(jnp.bitwise_and(lane, FOX_HEADS - 1) == row_head, s_past, NEG_INF)
    s_new = lax.dot_general(q, kpad_ref[...].astype(BF16), NT_DIMS, preferred_element_type=F32)
    s_new = s_new * FOX_SCALE + c_new - c_new_row
    visible = (jnp.bitwise_and(tail_lane, FOX_HEADS - 1) == row_head) & (tail_lane <= row)
    s_new = jnp.where(visible, s_new, NEG_INF)

    m = jnp.maximum(jnp.max(s_past, axis=1, keepdims=True), jnp.max(s_new, axis=1, keepdims=True))
    p_past = jnp.exp(s_past - m)
    p_new = jnp.exp(s_new - m)
    total = jnp.sum(p_past, axis=1, keepdims=True) + jnp.sum(p_new, axis=1, keepdims=True)
    p_past = p_past.astype(BF16)
    acc = jnp.dot(p_new.astype(BF16), vpad_ref[...].astype(BF16), preferred_element_type=F32)
    for j, vp in enumerate(v_pages):
        acc = acc + jnp.dot(p_past[:, j * PAGE_ROWS:(j + 1) * PAGE_ROWS], vp[0].astype(BF16),
                            preferred_element_type=F32)
    o_ref[0] = acc / total


def fox_paged(q, k_new, v_new, logf_new, cache_k, cache_v, cache_logf, page_table, first_page):
    bsz = q.shape[0]

    def page(j):
        return lambda b, pt: (first_page + pt[b, j], 0, 0)

    new_spec = pl.BlockSpec((1, NEW_ROWS, FOX_HEAD_DIM), lambda b, pt: (b, 0, 0))
    in_specs = [new_spec, new_spec, new_spec, pl.BlockSpec((1, NEW_ROWS, 1), lambda b, pt: (b, 0, 0))]
    in_specs += [pl.BlockSpec((1, PAGE_ROWS, FOX_HEAD_DIM), page(j)) for j in range(N_PAGES)]
    in_specs += [pl.BlockSpec((1, PAGE_ROWS, FOX_HEAD_DIM), page(j)) for j in range(N_PAGES)]
    in_specs += [pl.BlockSpec((1, 1, PAGE_ROWS), page(j)) for j in range(N_PAGES)]
    return pl.pallas_call(
        _fox_paged_kernel,
        grid_spec=pltpu.PrefetchScalarGridSpec(
            num_scalar_prefetch=1,
            grid=(bsz,),
            in_specs=in_specs,
            out_specs=new_spec,
            scratch_shapes=[pltpu.VMEM((LANES, FOX_HEAD_DIM), F32), pltpu.VMEM((LANES, FOX_HEAD_DIM), F32)],
        ),
        out_shape=jax.ShapeDtypeStruct((bsz, NEW_ROWS, FOX_HEAD_DIM), F32),
        compiler_params=_params("parallel"),
        name="fox_paged",
    )(page_table, q, k_new, v_new, logf_new,
      *([cache_k] * N_PAGES), *([cache_v] * N_PAGES), *([cache_logf] * N_PAGES))


ROPE_HALF = RET_KD // 2
RET_SCALE = RET_KD ** -0.5
RET_LOG_GAMMA = [float(np.log1p(-np.float32(2.0 ** (-5.0 - h)))) for h in range(RET_HEADS)]


def _rope_kernel(f_ref, cos_ref, sin_ref, *, start):
    pos = (start + lax.broadcasted_iota(jnp.int32, cos_ref.shape, 0)).astype(F32)
    ang = pos * f_ref[...]
    cos_ref[...] = jnp.cos(ang)
    sin_ref[...] = jnp.sin(ang)


def rope_table(start, n):
    inv_freq = ROPE_BASE ** (-jnp.arange(ROPE_HALF, dtype=F32) / ROPE_HALF)
    shape = jax.ShapeDtypeStruct((n, ROPE_HALF), F32)
    return pl.pallas_call(
        functools.partial(_rope_kernel, start=start), out_shape=(shape, shape), name="rope_table",
    )(inv_freq.reshape(1, ROPE_HALF))


def _rotate(x, cos, sin):
    x1, x2 = x[:, :ROPE_HALF], x[:, ROPE_HALF:]
    return jnp.concatenate([x1 * cos - x2 * sin, x1 * sin + x2 * cos], axis=1)


def _group_norm_gate(o, g, gain):
    mu = jnp.mean(o, axis=-1, keepdims=True)
    var = jnp.mean(jnp.square(o - mu), axis=-1, keepdims=True)
    return g * jax.nn.sigmoid(g) * ((o - mu) * lax.rsqrt(var + EPS) * gain)


def _head_slabs(refs, h, per_slab):
    width = PROJ_TILE // per_slab
    return refs[h // per_slab], slice((h % per_slab) * width, (h % per_slab + 1) * width)


def _ret_seq_kernel(*refs):
    q_refs, k_refs, v_refs, g_refs = refs[0:2], refs[2:4], refs[4:8], refs[8:12]
    cos_ref, sin_ref, gain_ref, o_ref, s_out, s_ref = refs[12:]
    step = pl.program_id(1)

    @pl.when(step == 0)
    def _():
        s_ref[...] = jnp.zeros_like(s_ref)

    cos = cos_ref[...]
    sin = sin_ref[...]
    cl = RET_CHUNK
    i_col = lax.broadcasted_iota(jnp.int32, (cl, 1), 0).astype(F32)
    diff = i_col - lax.broadcasted_iota(jnp.int32, (1, cl), 1).astype(F32)
    for h in range(RET_HEADS):
        lg = RET_LOG_GAMMA[h]
        ref, sl = _head_slabs(q_refs, h, 4)
        q = _rotate(ref[0, :, sl], cos, sin).astype(BF16)
        ref, sl = _head_slabs(k_refs, h, 4)
        k = _rotate(ref[0, :, sl], cos, sin) * RET_SCALE
        ref, sl = _head_slabs(v_refs, h, 2)
        v = ref[0, :, sl].astype(BF16)
        ref, sl = _head_slabs(g_refs, h, 2)
        g = ref[0, :, sl]
        dmask = jnp.where(diff >= 0, jnp.exp(jnp.maximum(diff, 0.0) * lg), 0.0)
        inner = lax.dot_general(q, k.astype(BF16), NT_DIMS, preferred_element_type=F32) * dmask
        state = s_ref[h]
        out = jnp.dot(inner.astype(BF16), v, preferred_element_type=F32)
        out = out + jnp.dot(q, state.astype(BF16), preferred_element_type=F32) * jnp.exp((i_col + 1.0) * lg)
        k_dec = k * jnp.exp((cl - 1.0 - i_col) * lg)
        s_ref[h] = math.exp(cl * lg) * state + jnp.dot(k_dec.T.astype(BF16), v, preferred_element_type=F32)
        hs = slice(h * RET_VD, (h + 1) * RET_VD)
        o_ref[:, hs] = _group_norm_gate(out, g, gain_ref[:, hs]).astype(BF16)

    @pl.when(step == pl.num_programs(1) - 1)
    def _():
        s_out[0] = s_ref[...]


def ret_sequences(z, cos, sin, gain, bsz, seq_len):
    steps = seq_len // RET_CHUNK
    vd = RET_HEADS * RET_VD
    slab = lambda j: pl.BlockSpec((1, RET_CHUNK, PROJ_TILE), lambda b, c: (j, b * steps + c, 0))
    table = pl.BlockSpec((RET_CHUNK, ROPE_HALF), lambda b, c: (c, 0))
    return pl.pallas_call(
        _ret_seq_kernel,
        grid=(bsz, steps),
        in_specs=[slab(j) for j in range(12)] + [table, table, _whole((1, vd))],
        out_specs=(pl.BlockSpec((RET_CHUNK, vd), lambda b, c: (b * steps + c, 0)),
                   pl.BlockSpec((1, RET_HEADS, RET_KD, RET_VD), lambda b, c: (b, 0, 0, 0))),
        out_shape=(jax.ShapeDtypeStruct((bsz * seq_len, vd), BF16),
                   jax.ShapeDtypeStruct((bsz, RET_HEADS, RET_KD, RET_VD), F32)),
        scratch_shapes=[pltpu.VMEM((RET_HEADS, RET_KD, RET_VD), F32)],
        compiler_params=_params("parallel", "arbitrary"),
        name="ret_sequences",
    )(*([z] * 12), cos, sin, gain.reshape(1, vd))


RET_PAD = 128


RET_STEP_SEQS = 8 // DEC_SEQ


def _ret_step_kernel(z_ref, s0_ref, cos_ref, sin_ref, gain_ref, o_ref, s_out, qpad, kpad, vpad, gpad):
    cl = DEC_SEQ
    for ref in (qpad, kpad, vpad, gpad):
        ref[...] = jnp.zeros_like(ref)
    cos = cos_ref[...]
    sin = sin_ref[...]
    i_col = lax.broadcasted_iota(jnp.int32, (RET_PAD, 1), 0).astype(F32)
    diff = i_col - lax.broadcasted_iota(jnp.int32, (1, RET_PAD), 1).astype(F32)
    valid = (diff >= 0) & (i_col < cl)
    for b in range(RET_STEP_SEQS):
        tok = slice(b * cl, (b + 1) * cl)
        outs = []
        for h in range(RET_HEADS):
            lg = RET_LOG_GAMMA[h]
            qpad[0:cl, :] = z_ref[h // 4, tok, (h % 4) * RET_KD:(h % 4 + 1) * RET_KD]
            kpad[0:cl, :] = z_ref[2 + h // 4, tok, (h % 4) * RET_KD:(h % 4 + 1) * RET_KD]
            vpad[0:cl, :] = z_ref[4 + h // 2, tok, (h % 2) * RET_VD:(h % 2 + 1) * RET_VD]
            gpad[0:cl, :] = z_ref[8 + h // 2, tok, (h % 2) * RET_VD:(h % 2 + 1) * RET_VD]
            q = _rotate(qpad[...], cos, sin).astype(BF16)
            k = _rotate(kpad[...], cos, sin) * RET_SCALE
            v = vpad[...].astype(BF16)
            dmask = jnp.where(valid, jnp.exp(jnp.maximum(diff, 0.0) * lg), 0.0)
            inner = lax.dot_general(q, k.astype(BF16), NT_DIMS, preferred_element_type=F32) * dmask
            state = s0_ref[b, h]
            out = jnp.dot(inner.astype(BF16), v, preferred_element_type=F32)
            out = out + jnp.dot(q, state.astype(BF16), preferred_element_type=F32) * jnp.exp((i_col + 1.0) * lg)
            k_dec = k * jnp.exp((cl - 1.0 - i_col) * lg)
            s_out[b, h] = math.exp(cl * lg) * state + jnp.dot(k_dec.T.astype(BF16), v, preferred_element_type=F32)
            hs = slice(h * RET_VD, (h + 1) * RET_VD)
            outs.append(_group_norm_gate(out[0:8], gpad[0:8, :], gain_ref[:, hs]))
        o_ref[tok, :] = jnp.concatenate(outs, axis=1)[0:cl]


def ret_steps(z, first_row, s0, cos, sin, gain):
    bsz = s0.shape[0]
    vd = RET_HEADS * RET_VD
    rows = RET_STEP_SEQS * DEC_SEQ
    state_spec = pl.BlockSpec((RET_STEP_SEQS, RET_HEADS, RET_KD, RET_VD), lambda i: (i, 0, 0, 0))
    return pl.pallas_call(
        _ret_step_kernel,
        grid=(bsz // RET_STEP_SEQS,),
        in_specs=[pl.BlockSpec((12, rows, PROJ_TILE), lambda i: (0, first_row // rows + i, 0)), state_spec,
                  _whole((RET_PAD, ROPE_HALF)), _whole((RET_PAD, ROPE_HALF)), _whole((1, vd))],
        out_specs=(pl.BlockSpec((rows, vd), lambda i: (i, 0)), state_spec),
        out_shape=(jax.ShapeDtypeStruct((bsz * DEC_SEQ, vd), F32),
                   jax.ShapeDtypeStruct(s0.shape, F32)),
        scratch_shapes=[pltpu.VMEM((RET_PAD, RET_KD), F32), pltpu.VMEM((RET_PAD, RET_KD), F32),
                        pltpu.VMEM((RET_PAD, RET_VD), F32), pltpu.VMEM((RET_PAD, RET_VD), F32)],
        compiler_params=_params("parallel"),
        name="ret_steps",
    )(z, s0, cos, sin, gain.reshape(1, vd))


def _even_mixer(h, gain, e, cache_k, cache_v, cache_logf, state_s5_re, state_s5_im, page_table, w_in, b_f,
                s5_w):
    z, zf = norm_proj(h, gain, w_in, 4 * PROJ_TILE, narrow=True)

    s5_p, hr_p, hi_p = s5_sequences(z, s5_w, BATCH, SEQ)
    u_s = jnp.stack([z[0, N_PROMPT + t::DEC_SEQ] for t in range(DEC_SEQ)])
    s5_s, hr_s, hi_s = s5_steps(u_s, state_s5_re.reshape(DEC_BATCH, S5_LANES),
                                state_s5_im.reshape(DEC_BATCH, S5_LANES), s5_w)
    s5_s = s5_s.transpose(1, 0, 2).reshape(N_SAMPLE, S5_WIDTH)

    lf_p, c_p, ct_p = fox_gates_seq(zf, b_f, BATCH, SEQ)
    lf_s = fox_gates(zf[N_PROMPT:], b_f)[:, :FOX_HEADS]
    k_p, v_p, q_s, k_s, v_s = head_rows(z)
    att_p = fox_sequences(z, c_p, ct_p, BATCH, SEQ)
    pool = cache_k.shape[1]
    new_rows = lambda t: t.reshape(DEC_BATCH, NEW_ROWS, FOX_HEAD_DIM)
    att_s = fox_paged(new_rows(q_s), new_rows(k_s), new_rows(v_s), lf_s.reshape(DEC_BATCH, NEW_ROWS, 1),
                      cache_k.reshape(-1, PAGE_ROWS, FOX_HEAD_DIM), cache_v.reshape(-1, PAGE_ROWS, FOX_HEAD_DIM),
                      cache_logf.reshape(-1, 1, PAGE_ROWS), page_table, e * pool)
    att_s = att_s.reshape(N_SAMPLE, FOX_WIDTH)

    heads = lambda t, b, s: t.reshape(b, s, FOX_HEADS, FOX_HEAD_DIM)
    grid = lambda t: t.reshape(-1, S5_GROUPS, S5_STATE)
    st_p = (heads(k_p, BATCH, SEQ), heads(v_p, BATCH, SEQ),
            lf_p[:, :FOX_HEADS].reshape(BATCH, SEQ, FOX_HEADS), grid(hr_p), grid(hi_p))
    st_s = (heads(k_s, DEC_BATCH, DEC_SEQ), heads(v_s, DEC_BATCH, DEC_SEQ),
            lf_s.reshape(DEC_BATCH, DEC_SEQ, FOX_HEADS), grid(hr_s), grid(hi_s))
    return ([s5_p, att_p], [s5_s, att_s]), st_p, st_s


def _odd_mixer(h, gain, w_in, gn_gain, state, tables):
    z = norm_proj(h, gain, w_in, 12 * PROJ_TILE)
    (cos_p, sin_p), (cos_s, sin_s) = tables
    o_p, st_p = ret_sequences(z, cos_p, sin_p, gn_gain, BATCH, SEQ)
    o_s, st_s = ret_steps(z, N_PROMPT, state, cos_s, sin_s, gn_gain)
    return ([o_p], [o_s]), st_p, st_s


def kernel(x_prompt, x_sample, cache_k, cache_v, cache_logf, state_s5_re, state_s5_im, state_ret, page_table, norm_gain, w_ffn_gate, w_ffn_up, w_ffn_down, w_in_even, b_forget, s5_lam_re, s5_lam_im, s5_b_re, s5_b_im, s5_c_re, s5_c_im, s5_d, s5_log_dt, w_glu, b_glu, w_out_even, w_in_odd, ret_gn_gain, w_out_odd, final_norm):
    x = jnp.concatenate([x_prompt.reshape(N_PROMPT, D_MODEL), x_sample.reshape(N_SAMPLE, D_MODEL)], axis=0)
    tables = (rope_table(0, SEQ), rope_table(PAST_LEN, RET_PAD))
    ev_p, ev_s, od_p, od_s = [], [], [], []
    for layer in range(DEPTH):
        h = ffn(x, norm_gain[layer, 0], w_ffn_gate, w_ffn_up, w_ffn_down, layer, 0)
        if layer % 2 == 0:
            e = layer // 2
            s5_w = s5_weights(s5_lam_re[e], s5_lam_im[e], s5_log_dt[e], s5_b_re[e], s5_b_im[e], s5_c_re[e],
                              s5_c_im[e], s5_d[e], w_glu[e], b_glu[e])
            parts, st_p, st_s = _even_mixer(h, norm_gain[layer, 1], e, cache_k, cache_v, cache_logf,
                                            state_s5_re[e], state_s5_im[e], page_table, w_in_even[e],
                                            b_forget[e], s5_w)
            ev_p.append(st_p)
            ev_s.append(st_s)
            w_out = w_out_even[e]
        else:
            o = layer // 2
            parts, st_p, st_s = _odd_mixer(h, norm_gain[layer, 1], w_in_odd[o], ret_gn_gain[o], state_ret[o],
                                           tables)
            od_p.append(st_p)
            od_s.append(st_s)
            w_out = w_out_odd[o]
        h = out_proj(*parts, w_out, h)
        x = ffn(h, norm_gain[layer, 2], w_ffn_gate, w_ffn_up, w_ffn_down, layer, 1)

    y_prompt, y_sample = final_rms(x, final_norm)
    y_prompt = y_prompt.reshape(BATCH, SEQ, D_MODEL)
    y_sample = y_sample.reshape(DEC_BATCH, DEC_SEQ, D_MODEL)
    stack = lambda sts, i: jnp.stack([s[i] for s in sts])
    return (y_prompt, y_sample,
            stack(ev_p, 0), stack(ev_p, 1), stack(ev_p, 2), stack(ev_p, 3), stack(ev_p, 4), jnp.stack(od_p),
            stack(ev_s, 0), stack(ev_s, 1), stack(ev_s, 2), stack(ev_s, 3), stack(ev_s, 4), jnp.stack(od_s))
```
